```python
import math
import jax, jax.numpy as jnp
from jax import lax
import numpy as np

D_MODEL = 2048
BATCH = 2
SEQ = 16384
DEPTH = 4
DEC_BATCH = 16
DEC_SEQ = 16
PAST_LEN = 2048

CHUNK = 64
Q_BLOCK = 128
ROPE_THETA = 500000.0
EPS = 1e-6
NEG = -1e30

A_HEADS = 4
A_QK_DIM = 64
A_V_DIM = 128
A_ROT = A_QK_DIM // 4
B_HEADS = 8
B_Q_LORA = 512
B_KV_LORA = 256
B_NOPE = 128
B_ROPE = 64
B_V_DIM = 128
C_HEADS = 4
C_DIM = 128
C_PAST_CHUNKS = 8
C_WINDOW = C_PAST_CHUNKS * CHUNK
REL_CLIP = 128
D_FF = 4 * D_MODEL

A_WIDTH = A_HEADS * A_V_DIM
B_WIDTH = B_HEADS * B_V_DIM
C_WIDTH = C_HEADS * C_DIM

IN_SPLITS = (
    A_HEADS * 2 * A_QK_DIM,
    A_HEADS * 2 * A_QK_DIM,
    A_WIDTH,
    B_Q_LORA,
    B_KV_LORA + B_ROPE,
    C_WIDTH,
    C_WIDTH,
    C_WIDTH,
    3 * D_MODEL,
)
IN_COLS = sum(IN_SPLITS)
IN_OFFSETS = [int(v) for v in np.cumsum(IN_SPLITS)[:-1]]

kernel_name = 'hybrid_streaming_encoder_step'


def rmsnorm(x, g):
    xf = x.astype(jnp.float32)
    y = xf * lax.rsqrt(jnp.mean(xf * xf, axis=-1, keepdims=True) + EPS)
    return (y * g.astype(jnp.float32)).astype(x.dtype)


def rope(x, pos, rot_dim):
    half = rot_dim // 2
    inv = jnp.power(ROPE_THETA, -jnp.arange(half, dtype=jnp.float32) / half)
    ang = pos.astype(jnp.float32)[:, None] * inv[None, :]
    cos = jnp.cos(ang)[:, None, :]
    sin = jnp.sin(ang)[:, None, :]
    xr = x[..., :rot_dim].astype(jnp.float32)
    x1, x2 = xr[..., :half], xr[..., half:]
    rot = jnp.concatenate([x1 * cos - x2 * sin, x2 * cos + x1 * sin], axis=-1).astype(x.dtype)
    return jnp.concatenate([rot, x[..., rot_dim:]], axis=-1)


def attend(q, k, v, mask=None, bias=None):
    scale = q.shape[-1] ** -0.5
    s = jnp.einsum('bthd,blhd->bhtl', q, k).astype(jnp.float32) * scale
    if bias is not None:
        s = s + bias.astype(jnp.float32)
    if mask is not None:
        s = jnp.where(mask, s, NEG)
    p = jax.nn.softmax(s, axis=-1)
    return jnp.einsum('bhtl,blhd->bthd', p.astype(v.dtype), v)


def chunk_causal_attention(q, k, v):
    bsz, s, h, dk = q.shape
    nb = s // Q_BLOCK
    qb = q.reshape(bsz, nb, Q_BLOCK, h, dk).transpose(1, 0, 2, 3, 4)
    k_chunk = jnp.arange(s) // CHUNK

    def block(args):
        qi, i = args
        q_chunk = (i * Q_BLOCK + jnp.arange(Q_BLOCK)) // CHUNK
        mask = (k_chunk[None, :] <= q_chunk[:, None])[None, None]
        return attend(qi, k, v, mask)

    out = lax.map(block, (qb, jnp.arange(nb)))
    return out.transpose(1, 0, 2, 3, 4).reshape(bsz, s, h, v.shape[-1])


def rel_bias(table, q_pos, k_pos):
    d = jnp.clip(q_pos[:, None] - k_pos[None, :], -REL_CLIP, REL_CLIP) + REL_CLIP
    return table[:, d]


def chunk_band_attention(q, k, v, table):
    bsz, s, h, d = q.shape
    nc = s // CHUNK
    band = (C_PAST_CHUNKS + 1) * CHUNK
    pad = C_PAST_CHUNKS * CHUNK
    kp = jnp.pad(k, ((0, 0), (pad, 0), (0, 0), (0, 0)))
    vp = jnp.pad(v, ((0, 0), (pad, 0), (0, 0), (0, 0)))
    qc = q.reshape(bsz, nc, CHUNK, h, d).transpose(1, 0, 2, 3, 4)
    k_off = jnp.arange(band)
    bias = rel_bias(table, pad + jnp.arange(CHUNK), k_off)[None]

    def one(args):
        qi, n = args
        kb = lax.dynamic_slice_in_dim(kp, n * CHUNK, band, axis=1)
        vb = lax.dynamic_slice_in_dim(vp, n * CHUNK, band, axis=1)
        mask = (n * CHUNK - pad + k_off >= 0)[None, None, None, :]
        return attend(qi, kb, vb, mask, bias)

    out = lax.map(one, (qc, jnp.arange(nc)))
    return out.transpose(1, 0, 2, 3, 4).reshape(bsz, s, h, d)


def branch_inputs(h, pos, w_in_l, b_gate_l, a_q_norm_l, a_k_norm_l, b_qa_norm_l, b_kv_norm_l,
                  w_qb_l, b_q_norm_l, c_q_norm_l, c_k_norm_l):
    bsz, s, _ = h.shape
    z = h @ w_in_l
    za_q, za_k, za_v, zb_q, zb_kv, zc_q, zc_k, zc_v, z_g = jnp.split(z, IN_OFFSETS, axis=-1)
    qa = rope(rmsnorm(za_q.reshape(bsz, s, 2 * A_HEADS, A_QK_DIM), a_q_norm_l), pos, A_ROT)
    ka = rope(rmsnorm(za_k.reshape(bsz, s, 2 * A_HEADS, A_QK_DIM), a_k_norm_l), pos, A_ROT)
    qa = qa.reshape(bsz, s, A_HEADS, 2 * A_QK_DIM)
    ka = ka.reshape(bsz, s, A_HEADS, 2 * A_QK_DIM)
    va = za_v.reshape(bsz, s, A_HEADS, A_V_DIM)
    c_q = rmsnorm(zb_q, b_qa_norm_l)
    qb = (c_q @ w_qb_l).reshape(bsz, s, B_HEADS, B_NOPE + B_ROPE)
    qb = jnp.concatenate([qb[..., :B_NOPE], rope(qb[..., B_NOPE:], pos, B_ROPE)], axis=-1)
    qb = rmsnorm(qb, b_q_norm_l)
    ckv = rmsnorm(zb_kv[..., :B_KV_LORA], b_kv_norm_l)
    kpe = rope(zb_kv[..., B_KV_LORA:][:, :, None, :], pos, B_ROPE)[:, :, 0, :]
    qc = rmsnorm(zc_q.reshape(bsz, s, C_HEADS, C_DIM), c_q_norm_l)
    kc = rmsnorm(zc_k.reshape(bsz, s, C_HEADS, C_DIM), c_k_norm_l)
    vc = zc_v.reshape(bsz, s, C_HEADS, C_DIM)
    gates = jax.nn.sigmoid(z_g + b_gate_l)
    return qa, ka, va, qb, ckv, kpe, qc, kc, vc, gates


def mla_kv(ckv, kpe, w_kvb_l, b_k_norm_l):
    bsz, l = ckv.shape[:2]
    kv = (ckv @ w_kvb_l).reshape(bsz, l, B_HEADS, B_NOPE + B_V_DIM)
    k_rope = jnp.broadcast_to(kpe[:, :, None, :], (bsz, l, B_HEADS, B_ROPE))
    k = rmsnorm(jnp.concatenate([kv[..., :B_NOPE], k_rope], axis=-1), b_k_norm_l)
    return k, kv[..., B_NOPE:]


def diff_out(o1, o2, lam, lam_init, sub_gain):
    o = o1 - lam.astype(o1.dtype) * o2
    return rmsnorm(o, sub_gain) * (1.0 - lam_init)


def merge_out(oa, ob, oc, gates, w_br_a_l, w_br_b_l, w_br_c_l, w_o_l):
    bsz, s = oa.shape[:2]
    ga, gb, gc = jnp.split(gates, 3, axis=-1)
    merged = (ga * (oa.reshape(bsz, s, A_WIDTH) @ w_br_a_l)
              + gb * (ob.reshape(bsz, s, B_WIDTH) @ w_br_b_l)
              + gc * (oc.reshape(bsz, s, C_WIDTH) @ w_br_c_l))
    return merged @ w_o_l


def sq_relu_mlp(h, w_up_l, w_down_l):
    return jnp.square(jax.nn.relu(h @ w_up_l)) @ w_down_l


def setup_inputs(seed: int = 0) -> dict:
    key = jax.random.key(seed)
    ks = iter(jax.random.split(key, 40))

    def nrm(shape, scale=1.0):
        return jax.random.normal(next(ks), shape, jnp.float32) * scale

    def gain(shape):
        return 1.0 + 0.01 * nrm(shape)

    c_past = min(C_WINDOW, PAST_LEN)
    return {
        'x_prompt': nrm((BATCH, SEQ, D_MODEL)),
        'x_sample': nrm((DEC_BATCH, DEC_SEQ, D_MODEL)),
        'cache_a_k': nrm((DEPTH, DEC_BATCH, PAST_LEN, A_HEADS, 2 * A_QK_DIM)),
        'cache_a_v': nrm((DEPTH, DEC_BATCH, PAST_LEN, A_HEADS, A_V_DIM)),
        'cache_b_ckv': nrm((DEPTH, DEC_BATCH, PAST_LEN, B_KV_LORA)),
        'cache_b_kpe': nrm((DEPTH, DEC_BATCH, PAST_LEN, B_ROPE)),
        'cache_c_k': nrm((DEPTH, DEC_BATCH, c_past, C_HEADS, C_DIM)),
        'cache_c_v': nrm((DEPTH, DEC_BATCH, c_past, C_HEADS, C_DIM)),
        'attn_norm': gain((DEPTH, D_MODEL)),
        'w_in': nrm((DEPTH, D_MODEL, IN_COLS), D_MODEL ** -0.5),
        'b_gate': nrm((DEPTH, 3 * D_MODEL), 0.1),
        'a_q_norm': gain((DEPTH, A_QK_DIM)),
        'a_k_norm': gain((DEPTH, A_QK_DIM)),
        'a_lambda': nrm((DEPTH, 4, A_QK_DIM), 0.1),
        'a_sub_norm': gain((DEPTH, A_V_DIM)),
        'b_qa_norm': gain((DEPTH, B_Q_LORA)),
        'b_kv_norm': gain((DEPTH, B_KV_LORA)),
        'w_qb': nrm((DEPTH, B_Q_LORA, B_HEADS * (B_NOPE + B_ROPE)), B_Q_LORA ** -0.5),
        'w_kvb': nrm((DEPTH, B_KV_LORA, B_HEADS * (B_NOPE + B_V_DIM)), B_KV_LORA ** -0.5),
        'b_q_norm': gain((DEPTH, B_NOPE + B_ROPE)),
        'b_k_norm': gain((DEPTH, B_NOPE + B_ROPE)),
        'c_q_norm': gain((DEPTH, C_DIM)),
        'c_k_norm': gain((DEPTH, C_DIM)),
        'c_rel_bias': nrm((DEPTH, C_HEADS, 2 * REL_CLIP + 1), 0.5),
        'w_br_a': nrm((DEPTH, A_WIDTH, D_MODEL), A_WIDTH ** -0.5),
        'w_br_b': nrm((DEPTH, B_WIDTH, D_MODEL), B_WIDTH ** -0.5),
        'w_br_c': nrm((DEPTH, C_WIDTH, D_MODEL), C_WIDTH ** -0.5),
        'w_o': nrm((DEPTH, D_MODEL, D_MODEL), D_MODEL ** -0.5),
        'mlp_norm': gain((DEPTH, D_MODEL)),
        'w_up': nrm((DEPTH, D_MODEL, D_FF), D_MODEL ** -0.5),
        'w_down': nrm((DEPTH, D_FF, D_MODEL), D_FF ** -0.5),
    }


def reference(x_prompt, x_sample, cache_a_k, cache_a_v, cache_b_ckv, cache_b_kpe, cache_c_k, cache_c_v,
              attn_norm, w_in, b_gate, a_q_norm, a_k_norm, a_lambda, a_sub_norm,
              b_qa_norm, b_kv_norm, w_qb, w_kvb, b_q_norm, b_k_norm,
              c_q_norm, c_k_norm, c_rel_bias, w_br_a, w_br_b, w_br_c, w_o,
              mlp_norm, w_up, w_down):
    seq = x_prompt.shape[1]
    dec = x_sample.shape[1]
    past = cache_a_k.shape[2]
    c_past = cache_c_k.shape[2]
    c_keep = min(C_WINDOW, seq)
    pos_p = jnp.arange(seq)
    pos_s = past + jnp.arange(dec)
    kc_pos = jnp.concatenate([past - c_past + jnp.arange(c_past), pos_s])
    q_ch, k_ch = pos_s // CHUNK, kc_pos // CHUNK
    c_mask_s = ((k_ch[None, :] <= q_ch[:, None]) & (k_ch[None, :] >= q_ch[:, None] - C_PAST_CHUNKS))[None, None]

    xp, xs = x_prompt, x_sample
    ak_p, av_p, bc_p, bp_p, ck_p, cv_p = [], [], [], [], [], []
    ak_s, av_s, bc_s, bp_s, ck_s, cv_s = [], [], [], [], [], []
    for l in range(DEPTH):
        lam_init = 0.8 - 0.6 * math.exp(-0.3 * l)
        lf = a_lambda[l].astype(jnp.float32)
        lam = jnp.exp(jnp.sum(lf[0] * lf[1])) - jnp.exp(jnp.sum(lf[2] * lf[3])) + lam_init

        qa, ka, va, qb, ckv, kpe, qc, kc, vc, g = branch_inputs(
            rmsnorm(xp, attn_norm[l]), pos_p, w_in[l], b_gate[l], a_q_norm[l], a_k_norm[l],
            b_qa_norm[l], b_kv_norm[l], w_qb[l], b_q_norm[l], c_q_norm[l], c_k_norm[l])
        oa = diff_out(chunk_causal_attention(qa[..., :A_QK_DIM], ka[..., :A_QK_DIM], va),
                      chunk_causal_attention(qa[..., A_QK_DIM:], ka[..., A_QK_DIM:], va),
                      lam, lam_init, a_sub_norm[l])
        kb, vb = mla_kv(ckv, kpe, w_kvb[l], b_k_norm[l])
        ob = chunk_causal_attention(qb, kb, vb)
        oc = chunk_band_attention(qc, kc, vc, c_rel_bias[l])
        xp = xp + merge_out(oa, ob, oc, g, w_br_a[l], w_br_b[l], w_br_c[l], w_o[l])
        xp = xp + sq_relu_mlp(rmsnorm(xp, mlp_norm[l]), w_up[l], w_down[l])
        ak_p.append(ka); av_p.append(va); bc_p.append(ckv); bp_p.append(kpe)
        ck_p.append(kc[:, seq - c_keep:]); cv_p.append(vc[:, seq - c_keep:])

        qa, ka, va, qb, ckv, kpe, qc, kc, vc, g = branch_inputs(
            rmsnorm(xs, attn_norm[l]), pos_s, w_in[l], b_gate[l], a_q_norm[l], a_k_norm[l],
            b_qa_norm[l], b_kv_norm[l], w_qb[l], b_q_norm[l], c_q_norm[l], c_k_norm[l])
        ka_all = jnp.concatenate([cache_a_k[l].astype(ka.dtype), ka], axis=1)
        va_all = jnp.concatenate([cache_a_v[l].astype(va.dtype), va], axis=1)
        oa = diff_out(attend(qa[..., :A_QK_DIM], ka_all[..., :A_QK_DIM], va_all),
                      attend(qa[..., A_QK_DIM:], ka_all[..., A_QK_DIM:], va_all),
                      lam, lam_init, a_sub_norm[l])
        ckv_all = jnp.concatenate([cache_b_ckv[l].astype(ckv.dtype), ckv], axis=1)
        kpe_all = jnp.concatenate([cache_b_kpe[l].astype(kpe.dtype), kpe], axis=1)
        kb, vb = mla_kv(ckv_all, kpe_all, w_kvb[l], b_k_norm[l])
        ob = attend(qb, kb, vb)
        kc_all = jnp.concatenate([cache_c_k[l].astype(kc.dtype), kc], axis=1)
        vc_all = jnp.concatenate([cache_c_v[l].astype(vc.dtype), vc], axis=1)
        oc = attend(qc, kc_all, vc_all, c_mask_s, rel_bias(c_rel_bias[l], pos_s, kc_pos)[None])
        xs = xs + merge_out(oa, ob, oc, g, w_br_a[l], w_br_b[l], w_br_c[l], w_o[l])
        xs = xs + sq_relu_mlp(rmsnorm(xs, mlp_norm[l]), w_up[l], w_down[l])
        ak_s.append(ka); av_s.append(va); bc_s.append(ckv); bp_s.append(kpe)
        ck_s.append(kc); cv_s.append(vc)

    return (xp, xs,
            jnp.stack(ak_p), jnp.stack(av_p), jnp.stack(bc_p), jnp.stack(bp_p), jnp.stack(ck_p), jnp.stack(cv_p),
            jnp.stack(ak_s), jnp.stack(av_s), jnp.stack(bc_s), jnp.stack(bp_s), jnp.stack(ck_s), jnp.stack(cv_s))
```

```python
import functools
import math

import jax
import jax.numpy as jnp
from jax import lax
from jax.experimental import pallas as pl
from jax.experimental.pallas import tpu as pltpu

F32 = jnp.float32
BF16 = jnp.bfloat16

CHUNK = 64
ROPE_THETA = 500000.0
EPS = 1e-6
NEG = -1e30
LANE = 128

A_HEADS = 4
A_QK_DIM = 64
A_V_DIM = 128
A_ROT = A_QK_DIM // 4
B_HEADS = 8
B_Q_LORA = 512
B_KV_LORA = 256
B_NOPE = 128
B_ROPE = 64
B_V_DIM = 128
B_QK = B_NOPE + B_ROPE
B_PAD = 2 * LANE
C_HEADS = 4
C_DIM = 128
C_PAST_CHUNKS = 8
C_WINDOW = C_PAST_CHUNKS * CHUNK
REL_CLIP = 128

A_W = A_HEADS * 2 * A_QK_DIM
AV_W = A_HEADS * A_V_DIM
BV_W = B_HEADS * B_V_DIM
C_W = C_HEADS * C_DIM

OFF_AQ = 0
OFF_AK = OFF_AQ + A_W
OFF_AV = OFF_AK + A_W
OFF_BQ = OFF_AV + AV_W
OFF_CKV = OFF_BQ + B_Q_LORA
OFF_CQ = OFF_CKV + B_KV_LORA
OFF_CK = OFF_CQ + C_W
OFF_CV = OFF_CK + C_W
OFF_KPE = OFF_CV + C_W
WA_COLS = OFF_KPE + LANE

VMEM_LIMIT = 56 * 1024 * 1024


def _cparams(sem):
    return pltpu.CompilerParams(dimension_semantics=sem, vmem_limit_bytes=VMEM_LIMIT)


def _rms(x, width):
    ms = jnp.sum(x * x, axis=-1, keepdims=True) * (1.0 / width)
    return x * lax.rsqrt(ms + EPS)


def _rms_half(zc, lane_lo):
    sq = zc * zc
    lo = jnp.sum(jnp.where(lane_lo, sq, 0.0), axis=-1, keepdims=True)
    hi = jnp.sum(jnp.where(lane_lo, 0.0, sq), axis=-1, keepdims=True)
    ms = jnp.where(lane_lo, lo, hi) * (1.0 / 64.0)
    return zc * lax.rsqrt(ms + EPS)


def _rope(y, c, s1, s2, half):
    return y * c + pltpu.roll(y, LANE - half, 1) * s1 + pltpu.roll(y, half, 1) * s2


def _dot(a, b):
    return jnp.dot(a, b, preferred_element_type=F32)


def _dot_nt(a, b):
    return lax.dot_general(a, b, (((1,), (1,)), ((), ())), preferred_element_type=F32)


def _in_proj_kernel(x_ref, g_ref, wa_ref, wqb_ref, gaq_ref, gak_ref, gbqa_ref, gbkv_ref, gbqn_ref, gbqr_ref,
                    gcq_ref, gck_ref, ca_ref, s1a_ref, s2a_ref, cb_ref, s1b_ref, s2b_ref,
                    qa_ref, kaf_ref, kab_ref, vaf_ref, vab_ref, qb_ref, ckv_ref, kpe_ref,
                    qc_ref, kcb_ref, vcb_ref, kcf_ref, vcf_ref):
    d_model = x_ref.shape[1]
    h = (_rms(x_ref[...], d_model) * g_ref[...]).astype(BF16)
    lane_lo = lax.broadcasted_iota(jnp.int32, (1, LANE), 1) < 64
    ca, s1a, s2a = ca_ref[...], s1a_ref[...], s2a_ref[...]
    cb, s1b, s2b = cb_ref[...], s1b_ref[...], s2b_ref[...]

    def proj(off, width):
        return _dot(h, wa_ref[:, off:off + width])

    def chunk(c):
        return slice(c * LANE, (c + 1) * LANE)

    z = proj(OFF_AQ, A_W)
    for c in range(A_W // LANE):
        y = _rms_half(z[:, chunk(c)], lane_lo) * gaq_ref[:, chunk(c)]
        qa_ref[:, chunk(c)] = _rope(y, ca, s1a, s2a, A_ROT // 2).astype(BF16)
    z = proj(OFF_AK, A_W)
    for c in range(A_W // LANE):
        y = _rms_half(z[:, chunk(c)], lane_lo) * gak_ref[:, chunk(c)]
        y = _rope(y, ca, s1a, s2a, A_ROT // 2)
        kaf_ref[:, chunk(c)] = y
        kab_ref[:, chunk(c)] = y.astype(BF16)
    z = proj(OFF_AV, AV_W)
    vaf_ref[...] = z
    vab_ref[...] = z.astype(BF16)

    z = proj(OFF_BQ, B_Q_LORA)
    cq = (_rms(z, B_Q_LORA) * gbqa_ref[...]).astype(BF16)
    qraw = _dot(cq, wqb_ref[...])
    nope_w = B_HEADS * B_NOPE
    roped = [_rope(qraw[:, nope_w + c * LANE:nope_w + (c + 1) * LANE], cb, s1b, s2b, B_ROPE // 2)
             for c in range(B_HEADS * B_ROPE // LANE)]
    for hh in range(B_HEADS):
        nope = qraw[:, hh * B_NOPE:(hh + 1) * B_NOPE]
        in_half = lane_lo if hh % 2 == 0 else jnp.logical_not(lane_lo)
        rsel = jnp.where(in_half, roped[hh // 2], 0.0)
        ss = jnp.sum(nope * nope, axis=-1, keepdims=True) + jnp.sum(rsel * rsel, axis=-1, keepdims=True)
        inv = lax.rsqrt(ss * (1.0 / B_QK) + EPS)
        qb_ref[:, hh * B_PAD:hh * B_PAD + LANE] = (nope * inv * gbqn_ref[...]).astype(BF16)
        qb_ref[:, hh * B_PAD + LANE:(hh + 1) * B_PAD] = (rsel * inv * gbqr_ref[...]).astype(BF16)
    z = proj(OFF_CKV, B_KV_LORA)
    ckv_ref[...] = _rms(z, B_KV_LORA) * gbkv_ref[...]
    z = proj(OFF_KPE, LANE)
    kpe_ref[...] = _rope(z, cb, s1b, s2b, B_ROPE // 2)[:, :B_ROPE]

    z = proj(OFF_CQ, C_W)
    for c in range(C_HEADS):
        qc_ref[:, chunk(c)] = (_rms(z[:, chunk(c)], C_DIM) * gcq_ref[:, chunk(c)]).astype(BF16)
    z = proj(OFF_CK, C_W)
    for c in range(C_HEADS):
        y = _rms(z[:, chunk(c)], C_DIM) * gck_ref[:, chunk(c)]
        kcf_ref[:, chunk(c)] = y
        kcb_ref[:, chunk(c)] = y.astype(BF16)
    z = proj(OFF_CV, C_W)
    vcf_ref[...] = z
    vcb_ref[...] = z.astype(BF16)


def _in_proj(x, lw, tabs, *, bm, seg_rows, keep):
    n, d_model = x.shape
    nb = n // bm
    nbs = seg_rows // bm
    nbk = keep // bm
    ntab = tabs[0].shape[0] // bm

    def row(i):
        return (i, 0)

    def const(i):
        return (0, 0)

    def tab(i):
        return (i % ntab, 0)

    def kept(i):
        return ((i // nbs) * nbk + jnp.maximum(i % nbs - (nbs - nbk), 0), 0)

    def rows(width):
        return pl.BlockSpec((bm, width), row)

    def full(a):
        return pl.BlockSpec(a.shape, const, pipeline_mode=pl.Buffered(1))

    gains = [lw['gaq'], lw['gak'], lw['gbqa'], lw['gbkv'], lw['gbqn'], lw['gbqr'], lw['gcq'], lw['gck']]
    in_specs = ([rows(d_model), full(lw['g_attn']), full(lw['w_a']), full(lw['w_qb'])]
                + [full(g) for g in gains]
                + [pl.BlockSpec((bm, LANE), tab) for _ in tabs])
    out_shape = [
        jax.ShapeDtypeStruct((n, A_W), BF16),
        jax.ShapeDtypeStruct((n, A_W), F32),
        jax.ShapeDtypeStruct((n, A_W), BF16),
        jax.ShapeDtypeStruct((n, AV_W), F32),
        jax.ShapeDtypeStruct((n, AV_W), BF16),
        jax.ShapeDtypeStruct((n, B_HEADS * B_PAD), BF16),
        jax.ShapeDtypeStruct((n, B_KV_LORA), F32),
        jax.ShapeDtypeStruct((n, B_ROPE), F32),
        jax.ShapeDtypeStruct((n, C_W), BF16),
        jax.ShapeDtypeStruct((n, C_W), BF16),
        jax.ShapeDtypeStruct((n, C_W), BF16),
        jax.ShapeDtypeStruct((n // seg_rows * keep, C_W), F32),
        jax.ShapeDtypeStruct((n // seg_rows * keep, C_W), F32),
    ]
    out_specs = [rows(s.shape[1]) for s in out_shape[:11]] + [pl.BlockSpec((bm, C_W), kept)] * 2
    return pl.pallas_call(
        _in_proj_kernel,
        grid=(nb,),
        in_specs=in_specs,
        out_specs=out_specs,
        out_shape=out_shape,
        compiler_params=_cparams(("arbitrary",)),
        name="in_proj",
    )(x, lw['g_attn'], lw['w_a'], lw['w_qb'], *gains, *tabs)


def _mla_kv_kernel(ckv_ref, kpe_ref, w_ref, gn_ref, gr_ref, kb_ref, vb_ref):
    ckv = ckv_ref[...]
    kv = _dot(ckv.astype(BF16), w_ref[...])
    kpe = kpe_ref[...]
    kpe2 = jnp.concatenate([kpe, kpe], axis=1)
    ss_pe = jnp.sum(kpe * kpe, axis=-1, keepdims=True)
    lane_lo = lax.broadcasted_iota(jnp.int32, (1, LANE), 1) < 64
    nope_w = B_HEADS * B_NOPE
    for hh in range(B_HEADS):
        nope = kv[:, hh * B_NOPE:(hh + 1) * B_NOPE]
        ss = jnp.sum(nope * nope, axis=-1, keepdims=True) + ss_pe
        inv = lax.rsqrt(ss * (1.0 / B_QK) + EPS)
        in_half = lane_lo if hh % 2 == 0 else jnp.logical_not(lane_lo)
        kb_ref[:, hh * B_PAD:hh * B_PAD + LANE] = (nope * inv * gn_ref[...]).astype(BF16)
        kb_ref[:, hh * B_PAD + LANE:(hh + 1) * B_PAD] = jnp.where(in_half, kpe2 * inv * gr_ref[...], 0.0).astype(BF16)
    vb_ref[...] = kv[:, nope_w:].astype(BF16)


def _mla_kv(ckv, kpe, lw, *, bm):
    n = ckv.shape[0]

    def row(i):
        return (i, 0)

    def const(i):
        return (0, 0)

    return pl.pallas_call(
        _mla_kv_kernel,
        grid=(n // bm,),
        in_specs=[pl.BlockSpec((bm, B_KV_LORA), row), pl.BlockSpec((bm, B_ROPE), row),
                  pl.BlockSpec(lw['w_kvb'].shape, const), pl.BlockSpec((1, LANE), const),
                  pl.BlockSpec((1, LANE), const)],
        out_specs=[pl.BlockSpec((bm, B_HEADS * B_PAD), row), pl.BlockSpec((bm, BV_W), row)],
        out_shape=[jax.ShapeDtypeStruct((n, B_HEADS * B_PAD), BF16), jax.ShapeDtypeStruct((n, BV_W), BF16)],
        compiler_params=_cparams(("arbitrary",)),
        name="mla_kv",
    )(ckv, kpe, lw['w_kvb'], lw['gbkn'], lw['gbkr'])


def _split_components(q):
    lane_lo = lax.broadcasted_iota(jnp.int32, (1, LANE), 1) < 64
    zero = jnp.zeros_like(q)
    return jnp.concatenate([jnp.where(lane_lo, q, zero), jnp.where(lane_lo, zero, q)], axis=0)


def _diff_out(o, rows, lam_ref, sg_ref):
    o = o[:rows] - lam_ref[...] * o[rows:]
    return _rms(o, A_V_DIM) * sg_ref[...]


def _flash_kernel(*refs, tq, diff):
    if diff:
        q_ref, k_ref, v_ref, lam_ref, sg_ref, o_ref, m_sc, l_sc, acc_sc = refs
    else:
        q_ref, k_ref, v_ref, o_ref, m_sc, l_sc, acc_sc = refs
    i = pl.program_id(2)
    q = q_ref[...]
    if diff:
        q = _split_components(q)
    m_rows = q.shape[0]
    m_sc[...] = jnp.full(m_sc.shape, NEG, F32)
    l_sc[...] = jnp.zeros(l_sc.shape, F32)
    acc_sc[...] = jnp.zeros(acc_sc.shape, F32)

    def step(j, masked):
        start = pl.multiple_of(j * tq, tq)
        k = k_ref[pl.ds(start, tq), :]
        v = v_ref[pl.ds(start, tq), :]
        s = _dot_nt(q, k)
        if masked:
            r = lax.broadcasted_iota(jnp.int32, (m_rows, tq), 0)
            if diff:
                r = jnp.where(r >= tq, r - tq, r)
            c = lax.broadcasted_iota(jnp.int32, (m_rows, tq), 1)
            s = jnp.where((c // CHUNK) <= (r // CHUNK), s, NEG)
        m_prev = m_sc[...]
        m_new = jnp.maximum(m_prev, jnp.max(s, axis=-1, keepdims=True))
        alpha = jnp.exp(m_prev - m_new)
        p = jnp.exp(s - m_new)
        l_sc[...] = alpha * l_sc[...] + jnp.sum(p, axis=-1, keepdims=True)
        acc_sc[...] = alpha * acc_sc[...] + _dot(p.astype(BF16), v)
        m_sc[...] = m_new

    def body(j, carry):
        step(j, False)
        return carry

    lax.fori_loop(0, i, body, 0)
    step(i, True)
    o = acc_sc[...] / l_sc[...]
    if diff:
        o = _diff_out(o, tq, lam_ref, sg_ref)
    o_ref[...] = o.astype(o_ref.dtype)


def _flash(q, k, v, *, batch, heads, dk, tq, diff, lam=None, sg=None):
    n = q.shape[0]
    seq = n // batch
    nq = seq // tq
    m_rows = 2 * tq if diff else tq
    in_specs = [pl.BlockSpec((tq, dk), lambda b, h, i: (b * nq + i, h)),
                pl.BlockSpec((seq, dk), lambda b, h, i: (b, h)),
                pl.BlockSpec((seq, LANE), lambda b, h, i: (b, h))]
    args = [q, k, v]
    if diff:
        in_specs += [pl.BlockSpec((1, LANE), lambda b, h, i: (0, 0))] * 2
        args += [lam, sg]
    return pl.pallas_call(
        functools.partial(_flash_kernel, tq=tq, diff=diff),
        grid=(batch, heads, nq),
        in_specs=in_specs,
        out_specs=pl.BlockSpec((tq, LANE), lambda b, h, i: (b * nq + i, h)),
        out_shape=jax.ShapeDtypeStruct((n, heads * LANE), BF16),
        scratch_shapes=[pltpu.VMEM((m_rows, 1), F32), pltpu.VMEM((m_rows, 1), F32),
                        pltpu.VMEM((m_rows, LANE), F32)],
        compiler_params=_cparams(("parallel", "parallel", "arbitrary")),
        name="flash_diff" if diff else "flash_mla",
    )(*args)


def _band_kernel(q_ref, k_ref, v_ref, bias_ref, o_ref, m_sc, l_sc, acc_sc, *, tq, nrel):
    i = pl.program_id(2)
    q = q_ref[...]
    m_sc[...] = jnp.full(m_sc.shape, NEG, F32)
    l_sc[...] = jnp.zeros(l_sc.shape, F32)
    acc_sc[...] = jnp.zeros(acc_sc.shape, F32)
    r = lax.broadcasted_iota(jnp.int32, (tq, tq), 0)
    c = lax.broadcasted_iota(jnp.int32, (tq, tq), 1)

    def step(rel):
        start = pl.multiple_of((i - rel) * tq, tq)
        k = k_ref[pl.ds(start, tq), :]
        v = v_ref[pl.ds(start, tq), :]
        s = _dot_nt(q, k) + bias_ref[0, rel]
        q_ch = (rel * tq + r) // CHUNK
        k_ch = c // CHUNK
        s = jnp.where((k_ch <= q_ch) & (k_ch >= q_ch - C_PAST_CHUNKS), s, NEG)
        m_prev = m_sc[...]
        m_new = jnp.maximum(m_prev, jnp.max(s, axis=-1, keepdims=True))
        alpha = jnp.exp(m_prev - m_new)
        p = jnp.exp(s - m_new)
        l_sc[...] = alpha * l_sc[...] + jnp.sum(p, axis=-1, keepdims=True)
        acc_sc[...] = alpha * acc_sc[...] + _dot(p.astype(BF16), v)
        m_sc[...] = m_new

    step(0)
    for rel in range(1, nrel):
        pl.when(i >= rel)(functools.partial(step, rel))
    o_ref[...] = (acc_sc[...] / l_sc[...]).astype(o_ref.dtype)


def _band_bias(table, tq, nrel):
    rel = jnp.arange(nrel)[:, None, None] * tq
    d = rel + jnp.arange(tq)[None, :, None] - jnp.arange(tq)[None, None, :]
    return table[:, jnp.clip(d, -REL_CLIP, REL_CLIP) + REL_CLIP]


def _band(q, k, v, bias, *, batch, tq):
    n = q.shape[0]
    seq = n // batch
    nq = seq // tq
    nrel = bias.shape[1]
    return pl.pallas_call(
        functools.partial(_band_kernel, tq=tq, nrel=nrel),
        grid=(batch, C_HEADS, nq),
        in_specs=[pl.BlockSpec((tq, LANE), lambda b, h, i: (b * nq + i, h)),
                  pl.BlockSpec((seq, LANE), lambda b, h, i: (b, h)),
                  pl.BlockSpec((seq, LANE), lambda b, h, i: (b, h)),
                  pl.BlockSpec((1, nrel, tq, tq), lambda b, h, i: (h, 0, 0, 0))],
        out_specs=pl.BlockSpec((tq, LANE), lambda b, h, i: (b * nq + i, h)),
        out_shape=jax.ShapeDtypeStruct((n, C_W), BF16),
        scratch_shapes=[pltpu.VMEM((tq, 1), F32), pltpu.VMEM((tq, 1), F32), pltpu.VMEM((tq, LANE), F32)],
        compiler_params=_cparams(("parallel", "parallel", "arbitrary")),
        name="band",
    )(q, k, v, bias)


def _decode_kernel(*refs, diff, biased):
    refs = list(refs)
    q_ref, kc_ref, vc_ref, kn_ref, vn_ref = refs[:5]
    rest = refs[5:]
    if biased:
        bc_ref, bn_ref, mc_ref, mn_ref = rest[:4]
        rest = rest[4:]
    if diff:
        lam_ref, sg_ref = rest[:2]
        rest = rest[2:]
    o_ref = rest[0]
    q = q_ref[...]
    rows = q.shape[0]
    if diff:
        q = _split_components(q)
    s_c = _dot_nt(q, kc_ref[0].astype(BF16))
    s_n = _dot_nt(q, kn_ref[...])
    if biased:
        s_c = jnp.where(mc_ref[...] > 0, s_c + bc_ref[0], NEG)
        s_n = jnp.where(mn_ref[...] > 0, s_n + bn_ref[0], NEG)
    m = jnp.maximum(jnp.max(s_c, axis=-1, keepdims=True), jnp.max(s_n, axis=-1, keepdims=True))
    p_c = jnp.exp(s_c - m)
    p_n = jnp.exp(s_n - m)
    l = jnp.sum(p_c, axis=-1, keepdims=True) + jnp.sum(p_n, axis=-1, keepdims=True)
    o = (_dot(p_c.astype(BF16), vc_ref[0].astype(BF16)) + _dot(p_n.astype(BF16), vn_ref[...])) / l
    if diff:
        o = _diff_out(o, rows, lam_ref, sg_ref)
    o_ref[...] = o.astype(o_ref.dtype)


def _decode(q, kc, vc, kn, vn, *, heads, dk, rows, diff=False, lam=None, sg=None, bias=None):
    n = q.shape[0]
    streams, past = kc.shape[0], kc.shape[1]
    in_specs = [pl.BlockSpec((rows, dk), lambda b, h: (b, h)),
                pl.BlockSpec((1, past, dk), lambda b, h: (b, 0, h)),
                pl.BlockSpec((1, past, LANE), lambda b, h: (b, 0, h)),
                pl.BlockSpec((rows, dk), lambda b, h: (b, h)),
                pl.BlockSpec((rows, LANE), lambda b, h: (b, h))]
    args = [q, kc, vc, kn, vn]
    if bias is not None:
        bias_c, bias_n, mask_c, mask_n = bias
        in_specs += [pl.BlockSpec((1, rows, past), lambda b, h: (h, 0, 0)),
                     pl.BlockSpec((1, rows, rows), lambda b, h: (h, 0, 0)),
                     pl.BlockSpec((rows, past), lambda b, h: (0, 0)),
                     pl.BlockSpec((rows, rows), lambda b, h: (0, 0))]
        args += [bias_c, bias_n, mask_c, mask_n]
    if diff:
        in_specs += [pl.BlockSpec((1, LANE), lambda b, h: (0, 0))] * 2
        args += [lam, sg]
    return pl.pallas_call(
        functools.partial(_decode_kernel, diff=diff, biased=bias is not None),
        grid=(streams, heads),
        in_specs=in_specs,
        out_specs=pl.BlockSpec((rows, LANE), lambda b, h: (b, h)),
        out_shape=jax.ShapeDtypeStruct((n, heads * LANE), BF16),
        compiler_params=_cparams(("parallel", "parallel")),
        name="decode",
    )(*args)


def _merge_kernel(x_ref, g_ref, oa_ref, ob_ref, oc_ref, wga_ref, wgb_ref, wgc_ref, bga_ref, bgb_ref, bgc_ref,
                  wba_ref, wbb_ref, wbc_ref, wo_ref, out_ref, h_sc):
    j = pl.program_id(1)

    @pl.when(j == 0)
    def _():
        x = x_ref[...]
        h_sc[...] = (_rms(x, x.shape[1]) * g_ref[...]).astype(BF16)
        out_ref[...] = x

    h = h_sc[...]

    def branch(o_ref, wg_ref, bg_ref, wb_ref):
        gate = jax.nn.sigmoid(_dot(h, wg_ref[...]) + bg_ref[...])
        return gate * _dot(o_ref[...], wb_ref[...])

    merged = (branch(oa_ref, wga_ref, bga_ref, wba_ref) + branch(ob_ref, wgb_ref, bgb_ref, wbb_ref)
              + branch(oc_ref, wgc_ref, bgc_ref, wbc_ref))
    out_ref[...] += _dot(merged.astype(BF16), wo_ref[...])


def _merge(x, oa, ob, oc, lw, *, bm, bn):
    n, d_model = x.shape
    nbn = d_model // bn

    def row(i, j):
        return (i, 0)

    def const(i, j):
        return (0, 0)

    def col(k):
        return lambda i, j: (0, k * nbn + j)

    in_specs = [pl.BlockSpec((bm, d_model), row), pl.BlockSpec((1, d_model), const),
                pl.BlockSpec((bm, AV_W), row), pl.BlockSpec((bm, BV_W), row), pl.BlockSpec((bm, C_W), row),
                pl.BlockSpec((d_model, bn), col(0)), pl.BlockSpec((d_model, bn), col(1)),
                pl.BlockSpec((d_model, bn), col(2)),
                pl.BlockSpec((1, bn), col(0)), pl.BlockSpec((1, bn), col(1)), pl.BlockSpec((1, bn), col(2)),
                pl.BlockSpec((AV_W, bn), col(0)), pl.BlockSpec((BV_W, bn), col(0)), pl.BlockSpec((C_W, bn), col(0)),
                pl.BlockSpec((bn, d_model), lambda i, j: (j, 0))]
    return pl.pallas_call(
        _merge_kernel,
        grid=(n // bm, nbn),
        in_specs=in_specs,
        out_specs=pl.BlockSpec((bm, d_model), row),
        out_shape=jax.ShapeDtypeStruct((n, d_model), F32),
        scratch_shapes=[pltpu.VMEM((bm, d_model), BF16)],
        compiler_params=_cparams(("parallel", "arbitrary")),
        name="merge",
    )(x, lw['g_attn'], oa, ob, oc, lw['w_g'], lw['w_g'], lw['w_g'], lw['b_g'], lw['b_g'], lw['b_g'],
      lw['w_br_a'], lw['w_br_b'], lw['w_br_c'], lw['w_o'])


def _mlp_kernel(x_ref, g_ref, wu_ref, wd_ref, out_ref, h_sc):
    j = pl.program_id(1)

    @pl.when(j == 0)
    def _():
        x = x_ref[...]
        h_sc[...] = (_rms(x, x.shape[1]) * g_ref[...]).astype(BF16)
        out_ref[...] = x

    u = jnp.maximum(_dot(h_sc[...], wu_ref[...]), 0.0)
    out_ref[...] += _dot((u * u).astype(BF16), wd_ref[...])


def _mlp(x, lw, *, bm, bf):
    n, d_model = x.shape
    d_ff = lw['w_up'].shape[1]
    return pl.pallas_call(
        _mlp_kernel,
        grid=(n // bm, d_ff // bf),
        in_specs=[pl.BlockSpec((bm, d_model), lambda i, j: (i, 0)), pl.BlockSpec((1, d_model), lambda i, j: (0, 0)),
                  pl.BlockSpec((d_model, bf), lambda i, j: (0, j)), pl.BlockSpec((bf, d_model), lambda i, j: (j, 0))],
        out_specs=pl.BlockSpec((bm, d_model), lambda i, j: (i, 0)),
        out_shape=jax.ShapeDtypeStruct((n, d_model), F32),
        scratch_shapes=[pltpu.VMEM((bm, d_model), BF16)],
        compiler_params=_cparams(("parallel", "arbitrary")),
        name="mlp",
    )(x, lw['g_mlp'], lw['w_up'], lw['w_down'])


def _rope_tables(pos, rot):
    half = rot // 2
    inv = jnp.power(ROPE_THETA, -jnp.arange(half, dtype=F32) / half)
    ang = pos.astype(F32)[:, None] * inv[None, :]
    cos, sin = jnp.cos(ang), jnp.sin(ang)
    npos = pos.shape[0]
    one = jnp.ones((npos, 64 - rot), F32)
    c64 = jnp.concatenate([cos, cos, one], axis=1)
    s1 = jnp.concatenate([-sin, jnp.zeros((npos, 64 - half), F32)], axis=1)
    s2 = jnp.concatenate([jnp.zeros((npos, half), F32), sin, jnp.zeros((npos, 64 - rot), F32)], axis=1)
    return [jnp.tile(t, (1, 2)) for t in (c64, s1, s2)]


def _layer_weights(l, p):
    d_model = p['w_in'].shape[1]
    w_in = p['w_in'][l]
    o = [0]
    for width in (A_W, A_W, AV_W, B_Q_LORA, B_KV_LORA + B_ROPE, C_W, C_W, C_W):
        o.append(o[-1] + width)
    cols = lambda a, b: w_in[:, a:b]
    w_a = jnp.concatenate([
        cols(o[0], o[1]), cols(o[1], o[2]), cols(o[2], o[3]), cols(o[3], o[4]),
        cols(o[4], o[4] + B_KV_LORA), cols(o[5], o[6]), cols(o[6], o[7]), cols(o[7], o[8]),
        cols(o[4] + B_KV_LORA, o[5]), jnp.zeros((d_model, LANE - B_ROPE), F32)], axis=1).astype(BF16)
    w_qb = p['w_qb'][l].reshape(B_Q_LORA, B_HEADS, B_QK)
    w_qb = jnp.concatenate([w_qb[:, :, :B_NOPE].reshape(B_Q_LORA, -1),
                            w_qb[:, :, B_NOPE:].reshape(B_Q_LORA, -1)], axis=1).astype(BF16)
    w_kvb = p['w_kvb'][l].reshape(B_KV_LORA, B_HEADS, B_NOPE + B_V_DIM)
    w_kvb = jnp.concatenate([w_kvb[:, :, :B_NOPE].reshape(B_KV_LORA, -1),
                             w_kvb[:, :, B_NOPE:].reshape(B_KV_LORA, -1)], axis=1).astype(BF16)
    lam_init = 0.8 - 0.6 * math.exp(-0.3 * l)
    lf = p['a_lambda'][l].astype(F32)
    lam = jnp.exp(jnp.sum(lf[0] * lf[1])) - jnp.exp(jnp.sum(lf[2] * lf[3])) + lam_init

    def tiled(g, reps, scale=1.0):
        return jnp.tile(g * scale, reps)[None, :]

    bqn = p['b_q_norm'][l]
    bkn = p['b_k_norm'][l]
    return {
        'g_attn': p['attn_norm'][l][None, :],
        'g_mlp': p['mlp_norm'][l][None, :],
        'w_a': w_a,
        'w_g': w_in[:, o[8]:].astype(BF16),
        'b_g': p['b_gate'][l][None, :],
        'w_qb': w_qb,
        'w_kvb': w_kvb,
        'gaq': tiled(p['a_q_norm'][l], 2 * A_HEADS, A_QK_DIM ** -0.5),
        'gak': tiled(p['a_k_norm'][l], 2 * A_HEADS),
        'gbqa': p['b_qa_norm'][l][None, :],
        'gbkv': p['b_kv_norm'][l][None, :],
        'gbqn': tiled(bqn[:B_NOPE], 1, B_QK ** -0.5),
        'gbqr': tiled(bqn[B_NOPE:], 2, B_QK ** -0.5),
        'gbkn': tiled(bkn[:B_NOPE], 1),
        'gbkr': tiled(bkn[B_NOPE:], 2),
        'gcq': tiled(p['c_q_norm'][l], C_HEADS, C_DIM ** -0.5),
        'gck': tiled(p['c_k_norm'][l], C_HEADS),
        'lam': jnp.broadcast_to(lam, (1, LANE)).astype(F32),
        'sg': (p['a_sub_norm'][l] * (1.0 - lam_init))[None, :],
        'w_br_a': p['w_br_a'][l].astype(BF16),
        'w_br_b': p['w_br_b'][l].astype(BF16),
        'w_br_c': p['w_br_c'][l].astype(BF16),
        'w_o': p['w_o'][l].astype(BF16),
        'w_up': p['w_up'][l].astype(BF16),
        'w_down': p['w_down'][l].astype(BF16),
    }


def _pick(n, prefs):
    for b in prefs:
        if n % b == 0:
            return b
    raise ValueError(f"no block size in {prefs} divides {n}")


def kernel(x_prompt, x_sample, cache_a_k, cache_a_v, cache_b_ckv, cache_b_kpe, cache_c_k, cache_c_v,
           attn_norm, w_in, b_gate, a_q_norm, a_k_norm, a_lambda, a_sub_norm,
           b_qa_norm, b_kv_norm, w_qb, w_kvb, b_q_norm, b_k_norm,
           c_q_norm, c_k_norm, c_rel_bias, w_br_a, w_br_b, w_br_c, w_o,
           mlp_norm, w_up, w_down):
    p = dict(attn_norm=attn_norm, w_in=w_in, b_gate=b_gate, a_q_norm=a_q_norm, a_k_norm=a_k_norm,
             a_lambda=a_lambda, a_sub_norm=a_sub_norm, b_qa_norm=b_qa_norm, b_kv_norm=b_kv_norm,
             w_qb=w_qb, w_kvb=w_kvb, b_q_norm=b_q_norm, b_k_norm=b_k_norm, c_q_norm=c_q_norm,
             c_k_norm=c_k_norm, w_br_a=w_br_a, w_br_b=w_br_b, w_br_c=w_br_c, w_o=w_o, mlp_norm=mlp_norm,
             w_up=w_up, w_down=w_down)
    depth = w_in.shape[0]
    batch, seq, d_model = x_prompt.shape
    streams, dec, _ = x_sample.shape
    past = cache_a_k.shape[2]
    c_past = cache_c_k.shape[2]
    c_keep = min(C_WINDOW, seq)
    n_p, n_s = batch * seq, streams * dec
    assert seq % CHUNK == 0 and c_keep % CHUNK == 0

    pos_p = jnp.arange(seq)
    pos_s = past + jnp.arange(dec)
    tabs_p = _rope_tables(pos_p, A_ROT) + _rope_tables(pos_p, B_ROPE)
    tabs_s = [jnp.tile(t, (streams, 1)) for t in _rope_tables(pos_s, A_ROT) + _rope_tables(pos_s, B_ROPE)]

    kc_pos = jnp.concatenate([past - c_past + jnp.arange(c_past), pos_s])
    q_ch, k_ch = pos_s // CHUNK, kc_pos // CHUNK
    c_mask = ((k_ch[None, :] <= q_ch[:, None]) & (k_ch[None, :] >= q_ch[:, None] - C_PAST_CHUNKS)).astype(F32)
    rel_s = jnp.clip(pos_s[:, None] - kc_pos[None, :], -REL_CLIP, REL_CLIP) + REL_CLIP

    bm_in = _pick(seq, (256, 128, 64))
    bm_tok = _pick(seq, (512, 256, 128, 64))
    t_a = _pick(seq, (256, 128, 64))
    t_b = _pick(seq, (512, 256, 128, 64))
    t_c = _pick(seq, (256, 128, 64))
    nrel_c = C_WINDOW // t_c + 1
    bn_merge = 512
    bf_mlp = 512
    bm_cache = _pick(streams * past, (512, 256, 128, 64, 16))

    xp = x_prompt.reshape(n_p, d_model)
    xs = x_sample.reshape(n_s, d_model)
    outs = [[] for _ in range(12)]
    for l in range(depth):
        lw = _layer_weights(l, p)

        (qa, kaf, kab, vaf, vab, qb, ckv, kpe, qc, kcb, vcb, kcf, vcf) = _in_proj(
            xp, lw, tabs_p, bm=bm_in, seg_rows=seq, keep=c_keep)
        kb, vb = _mla_kv(ckv, kpe, lw, bm=bm_tok)
        oa = _flash(qa, kab, vab, batch=batch, heads=A_HEADS, dk=LANE, tq=t_a, diff=True,
                    lam=lw['lam'], sg=lw['sg'])
        ob = _flash(qb, kb, vb, batch=batch, heads=B_HEADS, dk=B_PAD, tq=t_b, diff=False)
        oc = _band(qc, kcb, vcb, _band_bias(c_rel_bias[l], t_c, nrel_c), batch=batch, tq=t_c)
        xp = _merge(xp, oa, ob, oc, lw, bm=bm_tok, bn=bn_merge)
        xp = _mlp(xp, lw, bm=bm_tok, bf=bf_mlp)
        for dst, val in zip(outs[:6], (kaf, vaf, ckv, kpe, kcf, vcf)):
            dst.append(val)

        (qa, kaf, kab, vaf, vab, qb, ckv, kpe, qc, kcb, vcb, kcf, vcf) = _in_proj(
            xs, lw, tabs_s, bm=n_s, seg_rows=n_s, keep=n_s)
        kb, vb = _mla_kv(ckv, kpe, lw, bm=n_s)
        kb_c, vb_c = _mla_kv(cache_b_ckv[l].reshape(streams * past, B_KV_LORA),
                             cache_b_kpe[l].reshape(streams * past, B_ROPE), lw, bm=bm_cache)
        oa = _decode(qa, cache_a_k[l].reshape(streams, past, A_W), cache_a_v[l].reshape(streams, past, AV_W),
                     kab, vab, heads=A_HEADS, dk=LANE, rows=dec, diff=True, lam=lw['lam'], sg=lw['sg'])
        ob = _decode(qb, kb_c.reshape(streams, past, B_HEADS * B_PAD), vb_c.reshape(streams, past, BV_W),
                     kb, vb, heads=B_HEADS, dk=B_PAD, rows=dec)
        bias_s = c_rel_bias[l][:, rel_s]
        oc = _decode(qc, cache_c_k[l].reshape(streams, c_past, C_W), cache_c_v[l].reshape(streams, c_past, C_W),
                     kcb, vcb, heads=C_HEADS, dk=LANE, rows=dec,
                     bias=(bias_s[:, :, :c_past], bias_s[:, :, c_past:], c_mask[:, :c_past], c_mask[:, c_past:]))
        xs = _merge(xs, oa, ob, oc, lw, bm=n_s, bn=bn_merge)
        xs = _mlp(xs, lw, bm=n_s, bf=bf_mlp)
        for dst, val in zip(outs[6:], (kaf, vaf, ckv, kpe, kcf, vcf)):
            dst.append(val)

    st = [jnp.stack(o) for o in outs]
    return (xp.reshape(batch, seq, d_model), xs.reshape(streams, dec, d_model),
            st[0].reshape(depth, batch, seq, A_HEADS, 2 * A_QK_DIM),
            st[1].reshape(depth, batch, seq, A_HEADS, A_V_DIM),
            st[2].reshape(depth, batch, seq, B_KV_LORA),
            st[3].reshape(depth, batch, seq, B_ROPE),
            st[4].reshape(depth, batch, c_keep, C_HEADS, C_DIM),
            st[5].reshape(depth, batch, c_keep, C_HEADS, C_DIM),
            st[6].reshape(depth, streams, dec, A_HEADS, 2 * A_QK_DIM),
            st[7].reshape(depth, streams, dec, A_HEADS, A_V_DIM),
            st[8].reshape(depth, streams, dec, B_KV_LORA),
            st[9].reshape(depth, streams, dec, B_ROPE),
            st[10].reshape(depth, streams, dec, C_HEADS, C_DIM),
            st[11].reshape(depth, streams, dec, C_HEADS, C_DIM))
```

```python
import functools
import math

import jax
import jax.numpy as jnp
from jax import lax
from jax.experimental import pallas as pl
from jax.experimental.pallas import tpu as pltpu

F32 = jnp.float32
BF16 = jnp.bfloat16

CHUNK = 64
ROPE_THETA = 500000.0
EPS = 1e-6
NEG = -1e30
LANE = 128

A_HEADS = 4
A_QK_DIM = 64
A_V_DIM = 128
A_ROT = A_QK_DIM // 4
B_HEADS = 8
B_Q_LORA = 512
B_KV_LORA = 256
B_NOPE = 128
B_ROPE = 64
B_V_DIM = 128
B_QK = B_NOPE + B_ROPE
B_PAD = 2 * LANE
C_HEADS = 4
C_DIM = 128
C_PAST_CHUNKS = 8
C_WINDOW = C_PAST_CHUNKS * CHUNK
REL_CLIP = 128

A_W = A_HEADS * 2 * A_QK_DIM
AV_W = A_HEADS * A_V_DIM
BV_W = B_HEADS * B_V_DIM
C_W = C_HEADS * C_DIM

OFF_AQ = 0
OFF_AK = OFF_AQ + A_W
OFF_AV = OFF_AK + A_W
OFF_BQ = OFF_AV + AV_W
OFF_CKV = OFF_BQ + B_Q_LORA
OFF_CQ = OFF_CKV + B_KV_LORA
OFF_CK = OFF_CQ + C_W
OFF_CV = OFF_CK + C_W
OFF_KPE = OFF_CV + C_W
WA_COLS = OFF_KPE + LANE

VMEM_LIMIT = 56 * 1024 * 1024


def _cparams(sem):
    return pltpu.CompilerParams(dimension_semantics=sem, vmem_limit_bytes=VMEM_LIMIT)


def _rms(x, width):
    ms = jnp.sum(x * x, axis=-1, keepdims=True) * (1.0 / width)
    return x * lax.rsqrt(ms + EPS)


def _rms_half(zc, lane_lo):
    sq = zc * zc
    lo = jnp.sum(jnp.where(lane_lo, sq, 0.0), axis=-1, keepdims=True)
    hi = jnp.sum(jnp.where(lane_lo, 0.0, sq), axis=-1, keepdims=True)
    ms = jnp.where(lane_lo, lo, hi) * (1.0 / 64.0)
    return zc * lax.rsqrt(ms + EPS)


def _rope(y, c, s1, s2, half):
    return y * c + pltpu.roll(y, LANE - half, 1) * s1 + pltpu.roll(y, half, 1) * s2


def _dot(a, b):
    return jnp.dot(a, b, preferred_element_type=F32)


def _dot_nt(a, b):
    return lax.dot_general(a, b, (((1,), (1,)), ((), ())), preferred_element_type=F32)


def _in_proj_kernel(x_ref, g_ref, wa_ref, wqb_ref, gaq_ref, gak_ref, gbqa_ref, gbkv_ref, gbqn_ref, gbqr_ref,
                    gcq_ref, gck_ref, ca_ref, s1a_ref, s2a_ref, cb_ref, s1b_ref, s2b_ref,
                    qa_ref, kaf_ref, kab_ref, vaf_ref, vab_ref, qb_ref, ckv_ref, kpe_ref,
                    qc_ref, kcb_ref, vcb_ref, kcf_ref, vcf_ref, *, transposed):
    d_model = x_ref.shape[1]

    def put(ref, c, val):
        if transposed:
            ref[c * LANE:(c + 1) * LANE, :] = val.T.astype(BF16)
        else:
            ref[:, c * LANE:(c + 1) * LANE] = val.astype(BF16)

    h = (_rms(x_ref[...], d_model) * g_ref[...]).astype(BF16)
    lane_lo = lax.broadcasted_iota(jnp.int32, (1, LANE), 1) < 64
    ca, s1a, s2a = ca_ref[...], s1a_ref[...], s2a_ref[...]
    cb, s1b, s2b = cb_ref[...], s1b_ref[...], s2b_ref[...]

    def proj(off, width):
        return _dot(h, wa_ref[:, off:off + width])

    def chunk(c):
        return slice(c * LANE, (c + 1) * LANE)

    z = proj(OFF_AQ, A_W)
    for c in range(A_W // LANE):
        y = _rms_half(z[:, chunk(c)], lane_lo) * gaq_ref[:, chunk(c)]
        put(qa_ref, c, _rope(y, ca, s1a, s2a, A_ROT // 2))
    z = proj(OFF_AK, A_W)
    for c in range(A_W // LANE):
        y = _rms_half(z[:, chunk(c)], lane_lo) * gak_ref[:, chunk(c)]
        y = _rope(y, ca, s1a, s2a, A_ROT // 2)
        kaf_ref[:, chunk(c)] = y
        kab_ref[:, chunk(c)] = y.astype(BF16)
    z = proj(OFF_AV, AV_W)
    vaf_ref[...] = z
    for c in range(AV_W // LANE):
        put(vab_ref, c, z[:, chunk(c)])

    z = proj(OFF_BQ, B_Q_LORA)
    cq = (_rms(z, B_Q_LORA) * gbqa_ref[...]).astype(BF16)
    qraw = _dot(cq, wqb_ref[...])
    nope_w = B_HEADS * B_NOPE
    roped = [_rope(qraw[:, nope_w + c * LANE:nope_w + (c + 1) * LANE], cb, s1b, s2b, B_ROPE // 2)
             for c in range(B_HEADS * B_ROPE // LANE)]
    for hh in range(B_HEADS):
        nope = qraw[:, hh * B_NOPE:(hh + 1) * B_NOPE]
        in_half = lane_lo if hh % 2 == 0 else jnp.logical_not(lane_lo)
        rsel = jnp.where(in_half, roped[hh // 2], 0.0)
        ss = jnp.sum(nope * nope, axis=-1, keepdims=True) + jnp.sum(rsel * rsel, axis=-1, keepdims=True)
        inv = lax.rsqrt(ss * (1.0 / B_QK) + EPS)
        put(qb_ref, 2 * hh, nope * inv * gbqn_ref[...])
        put(qb_ref, 2 * hh + 1, rsel * inv * gbqr_ref[...])
    z = proj(OFF_CKV, B_KV_LORA)
    ckv_ref[...] = _rms(z, B_KV_LORA) * gbkv_ref[...]
    z = proj(OFF_KPE, LANE)
    kpe_ref[...] = _rope(z, cb, s1b, s2b, B_ROPE // 2)[:, :B_ROPE]

    z = proj(OFF_CQ, C_W)
    for c in range(C_HEADS):
        qc_ref[:, chunk(c)] = (_rms(z[:, chunk(c)], C_DIM) * gcq_ref[:, chunk(c)]).astype(BF16)
    z = proj(OFF_CK, C_W)
    for c in range(C_HEADS):
        y = _rms(z[:, chunk(c)], C_DIM) * gck_ref[:, chunk(c)]
        kcf_ref[:, chunk(c)] = y
        kcb_ref[:, chunk(c)] = y.astype(BF16)
    z = proj(OFF_CV, C_W)
    vcf_ref[...] = z
    vcb_ref[...] = z.astype(BF16)


def _in_proj(x, lw, tabs, *, bm, seg_rows, keep, transposed):
    n, d_model = x.shape
    nb = n // bm
    nbs = seg_rows // bm
    nbk = keep // bm
    ntab = tabs[0].shape[0] // bm

    def row(i):
        return (i, 0)

    def const(i):
        return (0, 0)

    def tab(i):
        return (i % ntab, 0)

    def kept(i):
        return ((i // nbs) * nbk + jnp.maximum(i % nbs - (nbs - nbk), 0), 0)

    def rows(width):
        return pl.BlockSpec((bm, width), row)

    def full(a):
        return pl.BlockSpec(a.shape, const, pipeline_mode=pl.Buffered(1))

    gains = [lw['gaq'], lw['gak'], lw['gbqa'], lw['gbkv'], lw['gbqn'], lw['gbqr'], lw['gcq'], lw['gck']]
    in_specs = ([rows(d_model), full(lw['g_attn']), full(lw['w_a']), full(lw['w_qb'])]
                + [full(g) for g in gains]
                + [pl.BlockSpec((bm, LANE), tab) for _ in tabs])
    out_shape = [
        jax.ShapeDtypeStruct((n, A_W), BF16),
        jax.ShapeDtypeStruct((n, A_W), F32),
        jax.ShapeDtypeStruct((n, A_W), BF16),
        jax.ShapeDtypeStruct((n, AV_W), F32),
        jax.ShapeDtypeStruct((n, AV_W), BF16),
        jax.ShapeDtypeStruct((n, B_HEADS * B_PAD), BF16),
        jax.ShapeDtypeStruct((n, B_KV_LORA), F32),
        jax.ShapeDtypeStruct((n, B_ROPE), F32),
        jax.ShapeDtypeStruct((n, C_W), BF16),
        jax.ShapeDtypeStruct((n, C_W), BF16),
        jax.ShapeDtypeStruct((n, C_W), BF16),
        jax.ShapeDtypeStruct((n // seg_rows * keep, C_W), F32),
        jax.ShapeDtypeStruct((n // seg_rows * keep, C_W), F32),
    ]
    out_specs = [rows(s.shape[1]) for s in out_shape[:11]] + [pl.BlockSpec((bm, C_W), kept)] * 2
    if transposed:
        for k in (0, 4, 5):
            width = out_shape[k].shape[1]
            out_shape[k] = jax.ShapeDtypeStruct((width, n), BF16)
            out_specs[k] = pl.BlockSpec((width, bm), lambda i: (0, i))
    return pl.pallas_call(
        functools.partial(_in_proj_kernel, transposed=transposed),
        grid=(nb,),
        in_specs=in_specs,
        out_specs=out_specs,
        out_shape=out_shape,
        compiler_params=_cparams(("arbitrary",)),
        name="in_proj",
    )(x, lw['g_attn'], lw['w_a'], lw['w_qb'], *gains, *tabs)


def _mla_kv_kernel(ckv_ref, kpe_ref, w_ref, gn_ref, gr_ref, kb_ref, vb_ref, *, transposed):
    ckv = ckv_ref[...]
    kv = _dot(ckv.astype(BF16), w_ref[...])
    kpe = kpe_ref[...]
    kpe2 = jnp.concatenate([kpe, kpe], axis=1)
    ss_pe = jnp.sum(kpe * kpe, axis=-1, keepdims=True)
    lane_lo = lax.broadcasted_iota(jnp.int32, (1, LANE), 1) < 64
    nope_w = B_HEADS * B_NOPE
    for hh in range(B_HEADS):
        nope = kv[:, hh * B_NOPE:(hh + 1) * B_NOPE]
        ss = jnp.sum(nope * nope, axis=-1, keepdims=True) + ss_pe
        inv = lax.rsqrt(ss * (1.0 / B_QK) + EPS)
        in_half = lane_lo if hh % 2 == 0 else jnp.logical_not(lane_lo)
        kb_ref[:, hh * B_PAD:hh * B_PAD + LANE] = (nope * inv * gn_ref[...]).astype(BF16)
        kb_ref[:, hh * B_PAD + LANE:(hh + 1) * B_PAD] = jnp.where(in_half, kpe2 * inv * gr_ref[...], 0.0).astype(BF16)
    if transposed:
        for c in range(B_HEADS):
            vb_ref[c * LANE:(c + 1) * LANE, :] = kv[:, nope_w + c * LANE:nope_w + (c + 1) * LANE].T.astype(BF16)
    else:
        vb_ref[...] = kv[:, nope_w:].astype(BF16)


def _mla_kv(ckv, kpe, lw, *, bm, transposed=False):
    n = ckv.shape[0]
    if transposed:
        vb_shape, vb_spec = (BV_W, n), pl.BlockSpec((BV_W, bm), lambda i: (0, i))
    else:
        vb_shape, vb_spec = (n, BV_W), pl.BlockSpec((bm, BV_W), lambda i: (i, 0))

    def row(i):
        return (i, 0)

    def const(i):
        return (0, 0)

    return pl.pallas_call(
        functools.partial(_mla_kv_kernel, transposed=transposed),
        grid=(n // bm,),
        in_specs=[pl.BlockSpec((bm, B_KV_LORA), row), pl.BlockSpec((bm, B_ROPE), row),
                  pl.BlockSpec(lw['w_kvb'].shape, const), pl.BlockSpec((1, LANE), const),
                  pl.BlockSpec((1, LANE), const)],
        out_specs=[pl.BlockSpec((bm, B_HEADS * B_PAD), row), vb_spec],
        out_shape=[jax.ShapeDtypeStruct((n, B_HEADS * B_PAD), BF16), jax.ShapeDtypeStruct(vb_shape, BF16)],
        compiler_params=_cparams(("arbitrary",)),
        name="mla_kv",
    )(ckv, kpe, lw['w_kvb'], lw['gbkn'], lw['gbkr'])


def _split_components(q):
    lane_lo = lax.broadcasted_iota(jnp.int32, (1, LANE), 1) < 64
    zero = jnp.zeros_like(q)
    return jnp.concatenate([jnp.where(lane_lo, q, zero), jnp.where(lane_lo, zero, q)], axis=0)


def _diff_out(o, rows, lam_ref, sg_ref):
    o = o[:rows] - lam_ref[...] * o[rows:]
    return _rms(o, A_V_DIM) * sg_ref[...]


def _flash_kernel(*refs, tq, tk, diff):
    if diff:
        qt_ref, k_ref, vt_ref, lam_ref, sg_ref, o_ref, m_sc, l_sc, acc_sc = refs
    else:
        qt_ref, k_ref, vt_ref, o_ref, m_sc, l_sc, acc_sc = refs
    i = pl.program_id(2)
    qt = qt_ref[...]
    if diff:
        row_lo = lax.broadcasted_iota(jnp.int32, (qt.shape[0], 1), 0) < A_QK_DIM
        zero = jnp.zeros_like(qt)
        qt = jnp.concatenate([jnp.where(row_lo, qt, zero), jnp.where(row_lo, zero, qt)], axis=1)
    mq = qt.shape[1]
    per_q = tq // tk
    m_sc[...] = jnp.full(m_sc.shape, NEG, F32)
    l_sc[...] = jnp.zeros(l_sc.shape, F32)
    acc_sc[...] = jnp.zeros(acc_sc.shape, F32)

    def step(j, state, diag):
        m_prev, l_prev, acc = state
        start = pl.multiple_of(j * tk, tk)
        k = k_ref[pl.ds(start, tk), :]
        vt = vt_ref[:, pl.ds(start, tk)]
        st = _dot(k, qt)
        if diag is not None:
            kpos = diag * tk + lax.broadcasted_iota(jnp.int32, (tk, mq), 0)
            qpos = lax.broadcasted_iota(jnp.int32, (tk, mq), 1)
            if diff:
                qpos = jnp.where(qpos >= tq, qpos - tq, qpos)
            st = jnp.where((kpos // CHUNK) <= (qpos // CHUNK), st, NEG)
        m_new = jnp.maximum(m_prev, jnp.max(st, axis=0, keepdims=True))
        alpha = jnp.exp(m_prev - m_new)
        pt = jnp.exp(st - m_new)
        l_new = alpha * l_prev + jnp.sum(pt, axis=0, keepdims=True)
        acc = alpha * acc + _dot(vt, pt.astype(BF16))
        return m_new, l_new, acc

    def load():
        return m_sc[...], l_sc[...], acc_sc[...]

    def store(state):
        m_sc[...], l_sc[...], acc_sc[...] = state

    def body(jb, carry):
        state = load()
        for jj in range(per_q):
            state = step(jb * per_q + jj, state, None)
        store(state)
        return carry

    lax.fori_loop(0, i, body, 0)
    state = load()
    for jj in range(per_q):
        state = step(i * per_q + jj, state, jj)
    _, l_fin, acc = state
    ot = acc / l_fin
    if diff:
        o = (ot[:, :tq] - lam_ref[:, :1] * ot[:, tq:]).T
        o = _rms(o, A_V_DIM) * sg_ref[...]
    else:
        o = ot.T
    o_ref[...] = o.astype(o_ref.dtype)


def _flash(qt, k, vt, *, batch, heads, dk, tq, tk, diff, lam=None, sg=None):
    n = k.shape[0]
    seq = n // batch
    nq = seq // tq
    mq = 2 * tq if diff else tq
    in_specs = [pl.BlockSpec((dk, tq), lambda b, h, i: (h, b * nq + i)),
                pl.BlockSpec((seq, dk), lambda b, h, i: (b, h)),
                pl.BlockSpec((LANE, seq), lambda b, h, i: (h, b))]
    args = [qt, k, vt]
    if diff:
        in_specs += [pl.BlockSpec((1, LANE), lambda b, h, i: (0, 0))] * 2
        args += [lam, sg]
    return pl.pallas_call(
        functools.partial(_flash_kernel, tq=tq, tk=tk, diff=diff),
        grid=(batch, heads, nq),
        in_specs=in_specs,
        out_specs=pl.BlockSpec((tq, LANE), lambda b, h, i: (b * nq + i, h)),
        out_shape=jax.ShapeDtypeStruct((n, heads * LANE), BF16),
        scratch_shapes=[pltpu.VMEM((1, mq), F32), pltpu.VMEM((1, mq), F32), pltpu.VMEM((LANE, mq), F32)],
        compiler_params=_cparams(("parallel", "parallel", "arbitrary")),
        name="flash_diff" if diff else "flash_mla",
    )(*args)


def _band_kernel(q_ref, k_ref, v_ref, bias_ref, o_ref, m_sc, l_sc, acc_sc, *, tq, nrel):
    i = pl.program_id(2)
    q = q_ref[...]
    m_sc[...] = jnp.full(m_sc.shape, NEG, F32)
    l_sc[...] = jnp.zeros(l_sc.shape, F32)
    acc_sc[...] = jnp.zeros(acc_sc.shape, F32)
    r = lax.broadcasted_iota(jnp.int32, (tq, tq), 0)
    c = lax.broadcasted_iota(jnp.int32, (tq, tq), 1)

    def step(rel):
        start = pl.multiple_of((i - rel) * tq, tq)
        k = k_ref[pl.ds(start, tq), :]
        v = v_ref[pl.ds(start, tq), :]
        s = _dot_nt(q, k) + bias_ref[0, rel]
        q_ch = (rel * tq + r) // CHUNK
        k_ch = c // CHUNK
        s = jnp.where((k_ch <= q_ch) & (k_ch >= q_ch - C_PAST_CHUNKS), s, NEG)
        m_prev = m_sc[...]
        m_new = jnp.maximum(m_prev, jnp.max(s, axis=-1, keepdims=True))
        alpha = jnp.exp(m_prev - m_new)
        p = jnp.exp(s - m_new)
        l_sc[...] = alpha * l_sc[...] + jnp.sum(p, axis=-1, keepdims=True)
        acc_sc[...] = alpha * acc_sc[...] + _dot(p.astype(BF16), v)
        m_sc[...] = m_new

    step(0)
    for rel in range(1, nrel):
        pl.when(i >= rel)(functools.partial(step, rel))
    o_ref[...] = (acc_sc[...] / l_sc[...]).astype(o_ref.dtype)


def _band_bias(table, tq, nrel):
    rel = jnp.arange(nrel)[:, None, None] * tq
    d = rel + jnp.arange(tq)[None, :, None] - jnp.arange(tq)[None, None, :]
    return table[:, jnp.clip(d, -REL_CLIP, REL_CLIP) + REL_CLIP]


def _band(q, k, v, bias, *, batch, tq):
    n = q.shape[0]
    seq = n // batch
    nq = seq // tq
    nrel = bias.shape[1]
    return pl.pallas_call(
        functools.partial(_band_kernel, tq=tq, nrel=nrel),
        grid=(batch, C_HEADS, nq),
        in_specs=[pl.BlockSpec((tq, LANE), lambda b, h, i: (b * nq + i, h)),
                  pl.BlockSpec((seq, LANE), lambda b, h, i: (b, h)),
                  pl.BlockSpec((seq, LANE), lambda b, h, i: (b, h)),
                  pl.BlockSpec((1, nrel, tq, tq), lambda b, h, i: (h, 0, 0, 0))],
        out_specs=pl.BlockSpec((tq, LANE), lambda b, h, i: (b * nq + i, h)),
        out_shape=jax.ShapeDtypeStruct((n, C_W), BF16),
        scratch_shapes=[pltpu.VMEM((tq, 1), F32), pltpu.VMEM((tq, 1), F32), pltpu.VMEM((tq, LANE), F32)],
        compiler_params=_cparams(("parallel", "parallel", "arbitrary")),
        name="band",
    )(q, k, v, bias)


def _decode_kernel(*refs, diff, biased):
    refs = list(refs)
    q_ref, kc_ref, vc_ref, kn_ref, vn_ref = refs[:5]
    rest = refs[5:]
    if biased:
        bc_ref, bn_ref, mc_ref, mn_ref = rest[:4]
        rest = rest[4:]
    if diff:
        lam_ref, sg_ref = rest[:2]
        rest = rest[2:]
    o_ref = rest[0]
    q = q_ref[...]
    rows = q.shape[0]
    if diff:
        q = _split_components(q)
    s_c = _dot_nt(q, kc_ref[0].astype(BF16))
    s_n = _dot_nt(q, kn_ref[...])
    if biased:
        s_c = jnp.where(mc_ref[...] > 0, s_c + bc_ref[0], NEG)
        s_n = jnp.where(mn_ref[...] > 0, s_n + bn_ref[0], NEG)
    m = jnp.maximum(jnp.max(s_c, axis=-1, keepdims=True), jnp.max(s_n, axis=-1, keepdims=True))
    p_c = jnp.exp(s_c - m)
    p_n = jnp.exp(s_n - m)
    l = jnp.sum(p_c, axis=-1, keepdims=True) + jnp.sum(p_n, axis=-1, keepdims=True)
    o = (_dot(p_c.astype(BF16), vc_ref[0].astype(BF16)) + _dot(p_n.astype(BF16), vn_ref[...])) / l
    if diff:
        o = _diff_out(o, rows, lam_ref, sg_ref)
    o_ref[...] = o.astype(o_ref.dtype)


def _decode(q, kc, vc, kn, vn, *, heads, dk, rows, diff=False, lam=None, sg=None, bias=None):
    n = q.shape[0]
    streams, past = kc.shape[0], kc.shape[1]
    in_specs = [pl.BlockSpec((rows, dk), lambda b, h: (b, h)),
                pl.BlockSpec((1, past, dk), lambda b, h: (b, 0, h)),
                pl.BlockSpec((1, past, LANE), lambda b, h: (b, 0, h)),
                pl.BlockSpec((rows, dk), lambda b, h: (b, h)),
                pl.BlockSpec((rows, LANE), lambda b, h: (b, h))]
    args = [q, kc, vc, kn, vn]
    if bias is not None:
        bias_c, bias_n, mask_c, mask_n = bias
        in_specs += [pl.BlockSpec((1, rows, past), lambda b, h: (h, 0, 0)),
                     pl.BlockSpec((1, rows, rows), lambda b, h: (h, 0, 0)),
                     pl.BlockSpec((rows, past), lambda b, h: (0, 0)),
                     pl.BlockSpec((rows, rows), lambda b, h: (0, 0))]
        args += [bias_c, bias_n, mask_c, mask_n]
    if diff:
        in_specs += [pl.BlockSpec((1, LANE), lambda b, h: (0, 0))] * 2
        args += [lam, sg]
    return pl.pallas_call(
        functools.partial(_decode_kernel, diff=diff, biased=bias is not None),
        grid=(streams, heads),
        in_specs=in_specs,
        out_specs=pl.BlockSpec((rows, LANE), lambda b, h: (b, h)),
        out_shape=jax.ShapeDtypeStruct((n, heads * LANE), BF16),
        compiler_params=_cparams(("parallel", "parallel")),
        name="decode",
    )(*args)


def _merge_kernel(x_ref, g_ref, oa_ref, ob_ref, oc_ref, wga_ref, wgb_ref, wgc_ref, bga_ref, bgb_ref, bgc_ref,
                  wba_ref, wbb_ref, wbc_ref, wo_ref, out_ref, h_sc):
    j = pl.program_id(1)

    @pl.when(j == 0)
    def _():
        x = x_ref[...]
        h_sc[...] = (_rms(x, x.shape[1]) * g_ref[...]).astype(BF16)
        out_ref[...] = x

    h = h_sc[...]

    def branch(o_ref, wg_ref, bg_ref, wb_ref):
        gate = jax.nn.sigmoid(_dot(h, wg_ref[...]) + bg_ref[...])
        return gate * _dot(o_ref[...], wb_ref[...])

    merged = (branch(oa_ref, wga_ref, bga_ref, wba_ref) + branch(ob_ref, wgb_ref, bgb_ref, wbb_ref)
              + branch(oc_ref, wgc_ref, bgc_ref, wbc_ref))
    out_ref[...] += _dot(merged.astype(BF16), wo_ref[...])


def _merge(x, oa, ob, oc, lw, *, bm, bn):
    n, d_model = x.shape
    nbn = d_model // bn

    def row(i, j):
        return (i, 0)

    def const(i, j):
        return (0, 0)

    def col(k):
        return lambda i, j: (0, k * nbn + j)

    in_specs = [pl.BlockSpec((bm, d_model), row), pl.BlockSpec((1, d_model), const),
                pl.BlockSpec((bm, AV_W), row), pl.BlockSpec((bm, BV_W), row), pl.BlockSpec((bm, C_W), row),
                pl.BlockSpec((d_model, bn), col(0)), pl.BlockSpec((d_model, bn), col(1)),
                pl.BlockSpec((d_model, bn), col(2)),
                pl.BlockSpec((1, bn), col(0)), pl.BlockSpec((1, bn), col(1)), pl.BlockSpec((1, bn), col(2)),
                pl.BlockSpec((AV_W, bn), col(0)), pl.BlockSpec((BV_W, bn), col(0)), pl.BlockSpec((C_W, bn), col(0)),
                pl.BlockSpec((bn, d_model), lambda i, j: (j, 0))]
    return pl.pallas_call(
        _merge_kernel,
        grid=(n // bm, nbn),
        in_specs=in_specs,
        out_specs=pl.BlockSpec((bm, d_model), row),
        out_shape=jax.ShapeDtypeStruct((n, d_model), F32),
        scratch_shapes=[pltpu.VMEM((bm, d_model), BF16)],
        compiler_params=_cparams(("parallel", "arbitrary")),
        name="merge",
    )(x, lw['g_attn'], oa, ob, oc, lw['w_g'], lw['w_g'], lw['w_g'], lw['b_g'], lw['b_g'], lw['b_g'],
      lw['w_br_a'], lw['w_br_b'], lw['w_br_c'], lw['w_o'])


def _mlp_kernel(x_ref, g_ref, wu_ref, wd_ref, out_ref, h_sc):
    j = pl.program_id(1)

    @pl.when(j == 0)
    def _():
        x = x_ref[...]
        h_sc[...] = (_rms(x, x.shape[1]) * g_ref[...]).astype(BF16)
        out_ref[...] = x

    u = jnp.maximum(_dot(h_sc[...], wu_ref[...]), 0.0)
    out_ref[...] += _dot((u * u).astype(BF16), wd_ref[...])


def _mlp(x, lw, *, bm, bf):
    n, d_model = x.shape
    d_ff = lw['w_up'].shape[1]
    return pl.pallas_call(
        _mlp_kernel,
        grid=(n // bm, d_ff // bf),
        in_specs=[pl.BlockSpec((bm, d_model), lambda i, j: (i, 0)), pl.BlockSpec((1, d_model), lambda i, j: (0, 0)),
                  pl.BlockSpec((d_model, bf), lambda i, j: (0, j)), pl.BlockSpec((bf, d_model), lambda i, j: (j, 0))],
        out_specs=pl.BlockSpec((bm, d_model), lambda i, j: (i, 0)),
        out_shape=jax.ShapeDtypeStruct((n, d_model), F32),
        scratch_shapes=[pltpu.VMEM((bm, d_model), BF16)],
        compiler_params=_cparams(("parallel", "arbitrary")),
        name="mlp",
    )(x, lw['g_mlp'], lw['w_up'], lw['w_down'])


def _rope_tables(pos, rot):
    half = rot // 2
    inv = jnp.power(ROPE_THETA, -jnp.arange(half, dtype=F32) / half)
    ang = pos.astype(F32)[:, None] * inv[None, :]
    cos, sin = jnp.cos(ang), jnp.sin(ang)
    npos = pos.shape[0]
    one = jnp.ones((npos, 64 - rot), F32)
    c64 = jnp.concatenate([cos, cos, one], axis=1)
    s1 = jnp.concatenate([-sin, jnp.zeros((npos, 64 - half), F32)], axis=1)
    s2 = jnp.concatenate([jnp.zeros((npos, half), F32), sin, jnp.zeros((npos, 64 - rot), F32)], axis=1)
    return [jnp.tile(t, (1, 2)) for t in (c64, s1, s2)]


def _layer_weights(l, p):
    d_model = p['w_in'].shape[1]
    w_in = p['w_in'][l]
    o = [0]
    for width in (A_W, A_W, AV_W, B_Q_LORA, B_KV_LORA + B_ROPE, C_W, C_W, C_W):
        o.append(o[-1] + width)
    cols = lambda a, b: w_in[:, a:b]
    w_a = jnp.concatenate([
        cols(o[0], o[1]), cols(o[1], o[2]), cols(o[2], o[3]), cols(o[3], o[4]),
        cols(o[4], o[4] + B_KV_LORA), cols(o[5], o[6]), cols(o[6], o[7]), cols(o[7], o[8]),
        cols(o[4] + B_KV_LORA, o[5]), jnp.zeros((d_model, LANE - B_ROPE), F32)], axis=1).astype(BF16)
    w_qb = p['w_qb'][l].reshape(B_Q_LORA, B_HEADS, B_QK)
    w_qb = jnp.concatenate([w_qb[:, :, :B_NOPE].reshape(B_Q_LORA, -1),
                            w_qb[:, :, B_NOPE:].reshape(B_Q_LORA, -1)], axis=1).astype(BF16)
    w_kvb = p['w_kvb'][l].reshape(B_KV_LORA, B_HEADS, B_NOPE + B_V_DIM)
    w_kvb = jnp.concatenate([w_kvb[:, :, :B_NOPE].reshape(B_KV_LORA, -1),
                             w_kvb[:, :, B_NOPE:].reshape(B_KV_LORA, -1)], axis=1).astype(BF16)
    lam_init = 0.8 - 0.6 * math.exp(-0.3 * l)
    lf = p['a_lambda'][l].astype(F32)
    lam = jnp.exp(jnp.sum(lf[0] * lf[1])) - jnp.exp(jnp.sum(lf[2] * lf[3])) + lam_init

    def tiled(g, reps, scale=1.0):
        return jnp.tile(g * scale, reps)[None, :]

    bqn = p['b_q_norm'][l]
    bkn = p['b_k_norm'][l]
    return {
        'g_attn': p['attn_norm'][l][None, :],
        'g_mlp': p['mlp_norm'][l][None, :],
        'w_a': w_a,
        'w_g': w_in[:, o[8]:].astype(BF16),
        'b_g': p['b_gate'][l][None, :],
        'w_qb': w_qb,
        'w_kvb': w_kvb,
        'gaq': tiled(p['a_q_norm'][l], 2 * A_HEADS, A_QK_DIM ** -0.5),
        'gak': tiled(p['a_k_norm'][l], 2 * A_HEADS),
        'gbqa': p['b_qa_norm'][l][None, :],
        'gbkv': p['b_kv_norm'][l][None, :],
        'gbqn': tiled(bqn[:B_NOPE], 1, B_QK ** -0.5),
        'gbqr': tiled(bqn[B_NOPE:], 2, B_QK ** -0.5),
        'gbkn': tiled(bkn[:B_NOPE], 1),
        'gbkr': tiled(bkn[B_NOPE:], 2),
        'gcq': tiled(p['c_q_norm'][l], C_HEADS, C_DIM ** -0.5),
        'gck': tiled(p['c_k_norm'][l], C_HEADS),
        'lam': jnp.broadcast_to(lam, (1, LANE)).astype(F32),
        'sg': (p['a_sub_norm'][l] * (1.0 - lam_init))[None, :],
        'w_br_a': p['w_br_a'][l].astype(BF16),
        'w_br_b': p['w_br_b'][l].astype(BF16),
        'w_br_c': p['w_br_c'][l].astype(BF16),
        'w_o': p['w_o'][l].astype(BF16),
        'w_up': p['w_up'][l].astype(BF16),
        'w_down': p['w_down'][l].astype(BF16),
    }


def _pick(n, prefs):
    for b in prefs:
        if n % b == 0:
            return b
    raise ValueError(f"no block size in {prefs} divides {n}")


def kernel(x_prompt, x_sample, cache_a_k, cache_a_v, cache_b_ckv, cache_b_kpe, cache_c_k, cache_c_v,
           attn_norm, w_in, b_gate, a_q_norm, a_k_norm, a_lambda, a_sub_norm,
           b_qa_norm, b_kv_norm, w_qb, w_kvb, b_q_norm, b_k_norm,
           c_q_norm, c_k_norm, c_rel_bias, w_br_a, w_br_b, w_br_c, w_o,
           mlp_norm, w_up, w_down):
    p = dict(attn_norm=attn_norm, w_in=w_in, b_gate=b_gate, a_q_norm=a_q_norm, a_k_norm=a_k_norm,
             a_lambda=a_lambda, a_sub_norm=a_sub_norm, b_qa_norm=b_qa_norm, b_kv_norm=b_kv_norm,
             w_qb=w_qb, w_kvb=w_kvb, b_q_norm=b_q_norm, b_k_norm=b_k_norm, c_q_norm=c_q_norm,
             c_k_norm=c_k_norm, w_br_a=w_br_a, w_br_b=w_br_b, w_br_c=w_br_c, w_o=w_o, mlp_norm=mlp_norm,
             w_up=w_up, w_down=w_down)
    depth = w_in.shape[0]
    batch, seq, d_model = x_prompt.shape
    streams, dec, _ = x_sample.shape
    past = cache_a_k.shape[2]
    c_past = cache_c_k.shape[2]
    c_keep = min(C_WINDOW, seq)
    n_p, n_s = batch * seq, streams * dec
    assert seq % CHUNK == 0 and c_keep % CHUNK == 0

    pos_p = jnp.arange(seq)
    pos_s = past + jnp.arange(dec)
    tabs_p = _rope_tables(pos_p, A_ROT) + _rope_tables(pos_p, B_ROPE)
    tabs_s = [jnp.tile(t, (streams, 1)) for t in _rope_tables(pos_s, A_ROT) + _rope_tables(pos_s, B_ROPE)]

    kc_pos = jnp.concatenate([past - c_past + jnp.arange(c_past), pos_s])
    q_ch, k_ch = pos_s // CHUNK, kc_pos // CHUNK
    c_mask = ((k_ch[None, :] <= q_ch[:, None]) & (k_ch[None, :] >= q_ch[:, None] - C_PAST_CHUNKS)).astype(F32)
    rel_s = jnp.clip(pos_s[:, None] - kc_pos[None, :], -REL_CLIP, REL_CLIP) + REL_CLIP

    bm_in = _pick(seq, (256, 128, 64))
    bm_tok = _pick(seq, (512, 256, 128, 64))
    t_a = _pick(seq, (512, 256, 128))
    t_b = _pick(seq, (1024, 512, 256, 128))
    tk_ab = _pick(seq, (256, 128))
    t_c = _pick(seq, (256, 128, 64))
    nrel_c = C_WINDOW // t_c + 1
    bn_merge = 512
    bf_mlp = 512
    bm_cache = _pick(streams * past, (512, 256, 128, 64, 16))

    xp = x_prompt.reshape(n_p, d_model)
    xs = x_sample.reshape(n_s, d_model)
    outs = [[] for _ in range(12)]
    for l in range(depth):
        lw = _layer_weights(l, p)

        (qa, kaf, kab, vaf, vab, qb, ckv, kpe, qc, kcb, vcb, kcf, vcf) = _in_proj(
            xp, lw, tabs_p, bm=bm_in, seg_rows=seq, keep=c_keep, transposed=True)
        kb, vb = _mla_kv(ckv, kpe, lw, bm=bm_tok, transposed=True)
        oa = _flash(qa, kab, vab, batch=batch, heads=A_HEADS, dk=LANE, tq=t_a, tk=tk_ab, diff=True,
                    lam=lw['lam'], sg=lw['sg'])
        ob = _flash(qb, kb, vb, batch=batch, heads=B_HEADS, dk=B_PAD, tq=t_b, tk=tk_ab, diff=False)
        oc = _band(qc, kcb, vcb, _band_bias(c_rel_bias[l], t_c, nrel_c), batch=batch, tq=t_c)
        xp = _merge(xp, oa, ob, oc, lw, bm=bm_tok, bn=bn_merge)
        xp = _mlp(xp, lw, bm=bm_tok, bf=bf_mlp)
        for dst, val in zip(outs[:6], (kaf, vaf, ckv, kpe, kcf, vcf)):
            dst.append(val)

        (qa, kaf, kab, vaf, vab, qb, ckv, kpe, qc, kcb, vcb, kcf, vcf) = _in_proj(
            xs, lw, tabs_s, bm=n_s, seg_rows=n_s, keep=n_s, transposed=False)
        kb, vb = _mla_kv(ckv, kpe, lw, bm=n_s)
        kb_c, vb_c = _mla_kv(cache_b_ckv[l].reshape(streams * past, B_KV_LORA),
                             cache_b_kpe[l].reshape(streams * past, B_ROPE), lw, bm=bm_cache)
        oa = _decode(qa, cache_a_k[l].reshape(streams, past, A_W), cache_a_v[l].reshape(streams, past, AV_W),
                     kab, vab, heads=A_HEADS, dk=LANE, rows=dec, diff=True, lam=lw['lam'], sg=lw['sg'])
        ob = _decode(qb, kb_c.reshape(streams, past, B_HEADS * B_PAD), vb_c.reshape(streams, past, BV_W),
                     kb, vb, heads=B_HEADS, dk=B_PAD, rows=dec)
        bias_s = c_rel_bias[l][:, rel_s]
        oc = _decode(qc, cache_c_k[l].reshape(streams, c_past, C_W), cache_c_v[l].reshape(streams, c_past, C_W),
                     kcb, vcb, heads=C_HEADS, dk=LANE, rows=dec,
                     bias=(bias_s[:, :, :c_past], bias_s[:, :, c_past:], c_mask[:, :c_past], c_mask[:, c_past:]))
        xs = _merge(xs, oa, ob, oc, lw, bm=n_s, bn=bn_merge)
        xs = _mlp(xs, lw, bm=n_s, bf=bf_mlp)
        for dst, val in zip(outs[6:], (kaf, vaf, ckv, kpe, kcf, vcf)):
            dst.append(val)

    st = [jnp.stack(o) for o in outs]
    return (xp.reshape(batch, seq, d_model), xs.reshape(streams, dec, d_model),
            st[0].reshape(depth, batch, seq, A_HEADS, 2 * A_QK_DIM),
            st[1].reshape(depth, batch, seq, A_HEADS, A_V_DIM),
            st[2].reshape(depth, batch, seq, B_KV_LORA),
            st[3].reshape(depth, batch, seq, B_ROPE),
            st[4].reshape(depth, batch, c_keep, C_HEADS, C_DIM),
            st[5].reshape(depth, batch, c_keep, C_HEADS, C_DIM),
            st[6].reshape(depth, streams, dec, A_HEADS, 2 * A_QK_DIM),
            st[7].reshape(depth, streams, dec, A_HEADS, A_V_DIM),
            st[8].reshape(depth, streams, dec, B_KV_LORA),
            st[9].reshape(depth, streams, dec, B_ROPE),
            st[10].reshape(depth, streams, dec, C_HEADS, C_DIM),
            st[11].reshape(depth, streams, dec, C_HEADS, C_DIM))
```

```python
import functools
import math

import jax
import jax.numpy as jnp
from jax import lax
from jax.experimental import pallas as pl
from jax.experimental.pallas import tpu as pltpu

F32 = jnp.float32
BF16 = jnp.bfloat16

CHUNK = 64
ROPE_THETA = 500000.0
EPS = 1e-6
NEG = -1e30
LOG2E = math.log2(math.e)
LANE = 128

A_HEADS = 4
A_QK_DIM = 64
A_V_DIM = 128
A_ROT = A_QK_DIM // 4
B_HEADS = 8
B_Q_LORA = 512
B_KV_LORA = 256
B_NOPE = 128
B_ROPE = 64
B_V_DIM = 128
B_QK = B_NOPE + B_ROPE
B_PAD = 2 * LANE
BF16_SUBLANES = 16
VT_ROWS = LANE + BF16_SUBLANES
C_HEADS = 4
C_DIM = 128
C_PAST_CHUNKS = 8
C_WINDOW = C_PAST_CHUNKS * CHUNK
REL_CLIP = 128

A_W = A_HEADS * 2 * A_QK_DIM
AV_W = A_HEADS * A_V_DIM
BV_W = B_HEADS * B_V_DIM
C_W = C_HEADS * C_DIM

OFF_AQ = 0
OFF_AK = OFF_AQ + A_W
OFF_AV = OFF_AK + A_W
OFF_BQ = OFF_AV + AV_W
OFF_CKV = OFF_BQ + B_Q_LORA
OFF_CQ = OFF_CKV + B_KV_LORA
OFF_CK = OFF_CQ + C_W
OFF_CV = OFF_CK + C_W
OFF_KPE = OFF_CV + C_W
WA_COLS = OFF_KPE + LANE

VMEM_LIMIT = 56 * 1024 * 1024


def _cparams(sem):
    return pltpu.CompilerParams(dimension_semantics=sem, vmem_limit_bytes=VMEM_LIMIT)


def _rms(x, width):
    ms = jnp.sum(x * x, axis=-1, keepdims=True) * (1.0 / width)
    return x * lax.rsqrt(ms + EPS)


def _rms_half(zc, lane_lo):
    sq = zc * zc
    lo = jnp.sum(jnp.where(lane_lo, sq, 0.0), axis=-1, keepdims=True)
    hi = jnp.sum(jnp.where(lane_lo, 0.0, sq), axis=-1, keepdims=True)
    ms = jnp.where(lane_lo, lo, hi) * (1.0 / 64.0)
    return zc * lax.rsqrt(ms + EPS)


def _rope(y, c, s1, s2, half):
    return y * c + pltpu.roll(y, LANE - half, 1) * s1 + pltpu.roll(y, half, 1) * s2


def _put_values_t(ref, head, val):
    base = head * VT_ROWS
    ref[base:base + LANE, :] = val.T.astype(BF16)
    ref[base + LANE:base + VT_ROWS, :] = jnp.ones((VT_ROWS - LANE, val.shape[0]), BF16)


def _dot(a, b):
    return jnp.dot(a, b, preferred_element_type=F32)


def _dot_nt(a, b):
    return lax.dot_general(a, b, (((1,), (1,)), ((), ())), preferred_element_type=F32)


def _in_proj_kernel(x_ref, g_ref, wa_ref, wqb_ref, gaq_ref, gak_ref, gbqa_ref, gbkv_ref, gbqn_ref, gbqr_ref,
                    gcq_ref, gck_ref, ca_ref, s1a_ref, s2a_ref, cb_ref, s1b_ref, s2b_ref,
                    qa_ref, kaf_ref, kab_ref, vaf_ref, vab_ref, qb_ref, ckv_ref, kpe_ref,
                    qc_ref, kcb_ref, vcb_ref, kcf_ref, vcf_ref, *, transposed):
    d_model = x_ref.shape[1]

    def put(ref, c, val):
        if transposed:
            ref[c * LANE:(c + 1) * LANE, :] = val.T.astype(BF16)
        else:
            ref[:, c * LANE:(c + 1) * LANE] = val.astype(BF16)

    h = (_rms(x_ref[...], d_model) * g_ref[...]).astype(BF16)
    lane_lo = lax.broadcasted_iota(jnp.int32, (1, LANE), 1) < 64
    ca, s1a, s2a = ca_ref[...], s1a_ref[...], s2a_ref[...]
    cb, s1b, s2b = cb_ref[...], s1b_ref[...], s2b_ref[...]

    def proj(off, width):
        return _dot(h, wa_ref[:, off:off + width])

    def chunk(c):
        return slice(c * LANE, (c + 1) * LANE)

    z = proj(OFF_AQ, A_W)
    for c in range(A_W // LANE):
        y = _rms_half(z[:, chunk(c)], lane_lo) * gaq_ref[:, chunk(c)]
        put(qa_ref, c, _rope(y, ca, s1a, s2a, A_ROT // 2))
    z = proj(OFF_AK, A_W)
    for c in range(A_W // LANE):
        y = _rms_half(z[:, chunk(c)], lane_lo) * gak_ref[:, chunk(c)]
        y = _rope(y, ca, s1a, s2a, A_ROT // 2)
        kaf_ref[:, chunk(c)] = y
        kab_ref[:, chunk(c)] = y.astype(BF16)
    z = proj(OFF_AV, AV_W)
    vaf_ref[...] = z
    for c in range(A_HEADS):
        if transposed:
            _put_values_t(vab_ref, c, z[:, chunk(c)])
        else:
            put(vab_ref, c, z[:, chunk(c)])

    z = proj(OFF_BQ, B_Q_LORA)
    cq = (_rms(z, B_Q_LORA) * gbqa_ref[...]).astype(BF16)
    qraw = _dot(cq, wqb_ref[...])
    nope_w = B_HEADS * B_NOPE
    roped = [_rope(qraw[:, nope_w + c * LANE:nope_w + (c + 1) * LANE], cb, s1b, s2b, B_ROPE // 2)
             for c in range(B_HEADS * B_ROPE // LANE)]
    for hh in range(B_HEADS):
        nope = qraw[:, hh * B_NOPE:(hh + 1) * B_NOPE]
        in_half = lane_lo if hh % 2 == 0 else jnp.logical_not(lane_lo)
        rsel = jnp.where(in_half, roped[hh // 2], 0.0)
        ss = jnp.sum(nope * nope, axis=-1, keepdims=True) + jnp.sum(rsel * rsel, axis=-1, keepdims=True)
        inv = lax.rsqrt(ss * (1.0 / B_QK) + EPS)
        put(qb_ref, 2 * hh, nope * inv * gbqn_ref[...])
        put(qb_ref, 2 * hh + 1, rsel * inv * gbqr_ref[...])
    z = proj(OFF_CKV, B_KV_LORA)
    ckv_ref[...] = _rms(z, B_KV_LORA) * gbkv_ref[...]
    z = proj(OFF_KPE, LANE)
    kpe_ref[...] = _rope(z, cb, s1b, s2b, B_ROPE // 2)[:, :B_ROPE]

    z = proj(OFF_CQ, C_W)
    for c in range(C_HEADS):
        qc_ref[:, chunk(c)] = (_rms(z[:, chunk(c)], C_DIM) * gcq_ref[:, chunk(c)]).astype(BF16)
    z = proj(OFF_CK, C_W)
    for c in range(C_HEADS):
        y = _rms(z[:, chunk(c)], C_DIM) * gck_ref[:, chunk(c)]
        kcf_ref[:, chunk(c)] = y
        kcb_ref[:, chunk(c)] = y.astype(BF16)
    z = proj(OFF_CV, C_W)
    vcf_ref[...] = z
    vcb_ref[...] = z.astype(BF16)


def _in_proj(x, lw, tabs, *, bm, seg_rows, keep, transposed):
    n, d_model = x.shape
    nb = n // bm
    nbs = seg_rows // bm
    nbk = keep // bm
    ntab = tabs[0].shape[0] // bm

    def row(i):
        return (i, 0)

    def const(i):
        return (0, 0)

    def tab(i):
        return (i % ntab, 0)

    def kept(i):
        return ((i // nbs) * nbk + jnp.maximum(i % nbs - (nbs - nbk), 0), 0)

    def rows(width):
        return pl.BlockSpec((bm, width), row)

    def full(a):
        return pl.BlockSpec(a.shape, const, pipeline_mode=pl.Buffered(1))

    fl = '_flash' if transposed else ''
    gains = [lw['gaq' + fl], lw['gak'], lw['gbqa'], lw['gbkv'], lw['gbqn' + fl], lw['gbqr' + fl],
             lw['gcq'], lw['gck']]
    in_specs = ([rows(d_model), full(lw['g_attn']), full(lw['w_a']), full(lw['w_qb'])]
                + [full(g) for g in gains]
                + [pl.BlockSpec((bm, LANE), tab) for _ in tabs])
    out_shape = [
        jax.ShapeDtypeStruct((n, A_W), BF16),
        jax.ShapeDtypeStruct((n, A_W), F32),
        jax.ShapeDtypeStruct((n, A_W), BF16),
        jax.ShapeDtypeStruct((n, AV_W), F32),
        jax.ShapeDtypeStruct((n, AV_W), BF16),
        jax.ShapeDtypeStruct((n, B_HEADS * B_PAD), BF16),
        jax.ShapeDtypeStruct((n, B_KV_LORA), F32),
        jax.ShapeDtypeStruct((n, B_ROPE), F32),
        jax.ShapeDtypeStruct((n, C_W), BF16),
        jax.ShapeDtypeStruct((n, C_W), BF16),
        jax.ShapeDtypeStruct((n, C_W), BF16),
        jax.ShapeDtypeStruct((n // seg_rows * keep, C_W), F32),
        jax.ShapeDtypeStruct((n // seg_rows * keep, C_W), F32),
    ]
    out_specs = [rows(s.shape[1]) for s in out_shape[:11]] + [pl.BlockSpec((bm, C_W), kept)] * 2
    if transposed:
        for k, width in ((0, A_W), (4, A_HEADS * VT_ROWS), (5, B_HEADS * B_PAD)):
            out_shape[k] = jax.ShapeDtypeStruct((width, n), BF16)
            out_specs[k] = pl.BlockSpec((width, bm), lambda i: (0, i))
    return pl.pallas_call(
        functools.partial(_in_proj_kernel, transposed=transposed),
        grid=(nb,),
        in_specs=in_specs,
        out_specs=out_specs,
        out_shape=out_shape,
        compiler_params=_cparams(("arbitrary",)),
        name="in_proj",
    )(x, lw['g_attn'], lw['w_a'], lw['w_qb'], *gains, *tabs)


def _mla_kv_kernel(ckv_ref, kpe_ref, w_ref, gn_ref, gr_ref, kb_ref, vb_ref, *, transposed):
    ckv = ckv_ref[...]
    kv = _dot(ckv.astype(BF16), w_ref[...])
    kpe = kpe_ref[...]
    kpe2 = jnp.concatenate([kpe, kpe], axis=1)
    ss_pe = jnp.sum(kpe * kpe, axis=-1, keepdims=True)
    lane_lo = lax.broadcasted_iota(jnp.int32, (1, LANE), 1) < 64
    nope_w = B_HEADS * B_NOPE
    for hh in range(B_HEADS):
        nope = kv[:, hh * B_NOPE:(hh + 1) * B_NOPE]
        ss = jnp.sum(nope * nope, axis=-1, keepdims=True) + ss_pe
        inv = lax.rsqrt(ss * (1.0 / B_QK) + EPS)
        in_half = lane_lo if hh % 2 == 0 else jnp.logical_not(lane_lo)
        kb_ref[:, hh * B_PAD:hh * B_PAD + LANE] = (nope * inv * gn_ref[...]).astype(BF16)
        kb_ref[:, hh * B_PAD + LANE:(hh + 1) * B_PAD] = jnp.where(in_half, kpe2 * inv * gr_ref[...], 0.0).astype(BF16)
    if transposed:
        for c in range(B_HEADS):
            _put_values_t(vb_ref, c, kv[:, nope_w + c * LANE:nope_w + (c + 1) * LANE])
    else:
        vb_ref[...] = kv[:, nope_w:].astype(BF16)


def _mla_kv(ckv, kpe, lw, *, bm, n=None, row0=0, transposed=False):
    n = ckv.shape[0] if n is None else n
    blk0 = row0 // bm
    assert row0 % bm == 0
    if transposed:
        vb_shape, vb_spec = (B_HEADS * VT_ROWS, n), pl.BlockSpec((B_HEADS * VT_ROWS, bm), lambda i: (0, i))
    else:
        vb_shape, vb_spec = (n, BV_W), pl.BlockSpec((bm, BV_W), lambda i: (i, 0))

    def row(i):
        return (i, 0)

    def src_row(i):
        return (blk0 + i, 0)

    def const(i):
        return (0, 0)

    return pl.pallas_call(
        functools.partial(_mla_kv_kernel, transposed=transposed),
        grid=(n // bm,),
        in_specs=[pl.BlockSpec((bm, B_KV_LORA), src_row), pl.BlockSpec((bm, B_ROPE), src_row),
                  pl.BlockSpec(lw['w_kvb'].shape, const), pl.BlockSpec((1, LANE), const),
                  pl.BlockSpec((1, LANE), const)],
        out_specs=[pl.BlockSpec((bm, B_HEADS * B_PAD), row), vb_spec],
        out_shape=[jax.ShapeDtypeStruct((n, B_HEADS * B_PAD), BF16), jax.ShapeDtypeStruct(vb_shape, BF16)],
        compiler_params=_cparams(("arbitrary",)),
        name="mla_kv",
    )(ckv, kpe, lw['w_kvb'], lw['gbkn'], lw['gbkr'])


def _split_components(q):
    lane_lo = lax.broadcasted_iota(jnp.int32, (1, LANE), 1) < 64
    zero = jnp.zeros_like(q)
    return jnp.concatenate([jnp.where(lane_lo, q, zero), jnp.where(lane_lo, zero, q)], axis=0)


def _diff_out(o, rows, lam_ref, sg_ref):
    o = o[:rows] - lam_ref[...] * o[rows:]
    return _rms(o, A_V_DIM) * sg_ref[...]


def _flash_kernel(*refs, tq, tk, diff):
    if diff:
        qt_ref, k_ref, vt_ref, lam_ref, sg_ref, o_ref, m_sc, a_sc, acc_sc, s_sc, p_sc = refs
    else:
        qt_ref, k_ref, vt_ref, o_ref, m_sc, a_sc, acc_sc, s_sc, p_sc = refs
    i = pl.program_id(2)
    qt = qt_ref[...]
    if diff:
        row_lo = lax.broadcasted_iota(jnp.int32, (qt.shape[0], 1), 0) < A_QK_DIM
        zero = jnp.zeros_like(qt)
        qt = jnp.concatenate([jnp.where(row_lo, qt, zero), jnp.where(row_lo, zero, qt)], axis=1)
    mq = qt.shape[1]
    per_q = tq // tk

    def scores(j):
        start = pl.multiple_of(j * tk, tk)
        return _dot(k_ref[pl.ds(start, tk), :], qt)

    def softmax(st, m_prev, diag):
        if diag is not None:
            kpos = diag * tk + lax.broadcasted_iota(jnp.int32, (tk, mq), 0)
            qpos = lax.broadcasted_iota(jnp.int32, (tk, mq), 1)
            if diff:
                qpos = jnp.where(qpos >= tq, qpos - tq, qpos)
            st = jnp.where((kpos // CHUNK) <= (qpos // CHUNK), st, NEG)
        m_new = jnp.maximum(m_prev, jnp.max(st, axis=0, keepdims=True))
        return m_new, jnp.exp2(m_prev - m_new), jnp.exp2(st - m_new).astype(BF16)

    def weighted_values(j, acc, alpha, pt):
        start = pl.multiple_of(j * tk, tk)
        return alpha * acc + _dot(vt_ref[:, pl.ds(start, tk)], pt)

    m_sc[...] = jnp.full(m_sc.shape, NEG, F32)
    a_sc[...] = jnp.ones(a_sc.shape, F32)
    acc_sc[...] = jnp.zeros(acc_sc.shape, F32)
    p_sc[...] = jnp.zeros(p_sc.shape, BF16)
    s_sc[...] = scores(0)

    def body(jb, carry):
        m, alpha_prev, acc, p_prev, st = m_sc[...], a_sc[...], acc_sc[...], p_sc[...], s_sc[...]
        for jj in range(per_q):
            j = jb * per_q + jj
            st_next = scores(j + 1)
            m, alpha, pt = softmax(st, m, None)
            acc = weighted_values(jnp.maximum(j - 1, 0), acc, alpha_prev, p_prev)
            alpha_prev, p_prev, st = alpha, pt, st_next
        m_sc[...], a_sc[...], acc_sc[...], p_sc[...], s_sc[...] = m, alpha_prev, acc, p_prev, st
        return carry

    lax.fori_loop(0, i, body, 0)
    m, alpha_prev, acc, p_prev, st = m_sc[...], a_sc[...], acc_sc[...], p_sc[...], s_sc[...]
    for jj in range(per_q):
        j = i * per_q + jj
        st_next = scores(j + 1) if jj + 1 < per_q else None
        m, alpha, pt = softmax(st, m, jj)
        acc = weighted_values(jnp.maximum(j - 1, 0), acc, alpha_prev, p_prev)
        alpha_prev, p_prev, st = alpha, pt, st_next
    acc = weighted_values(i * per_q + per_q - 1, acc, alpha_prev, p_prev)
    ot = acc[:A_V_DIM] / acc[A_V_DIM:A_V_DIM + 1]
    if diff:
        o = (ot[:, :tq] - lam_ref[:, :1] * ot[:, tq:]).T
        o = _rms(o, A_V_DIM) * sg_ref[...]
    else:
        o = ot.T
    o_ref[...] = o.astype(o_ref.dtype)


def _flash(qt, k, vt, *, batch, heads, dk, tq, tk, diff, lam=None, sg=None):
    n = k.shape[0]
    seq = n // batch
    nq = seq // tq
    mq = 2 * tq if diff else tq
    in_specs = [pl.BlockSpec((dk, tq), lambda b, h, i: (h, b * nq + i)),
                pl.BlockSpec((seq, dk), lambda b, h, i: (b, h)),
                pl.BlockSpec((VT_ROWS, seq), lambda b, h, i: (h, b))]
    args = [qt, k, vt]
    if diff:
        in_specs += [pl.BlockSpec((1, LANE), lambda b, h, i: (0, 0))] * 2
        args += [lam, sg]
    return pl.pallas_call(
        functools.partial(_flash_kernel, tq=tq, tk=tk, diff=diff),
        grid=(batch, heads, nq),
        in_specs=in_specs,
        out_specs=pl.BlockSpec((tq, LANE), lambda b, h, i: (b * nq + i, h)),
        out_shape=jax.ShapeDtypeStruct((n, heads * LANE), BF16),
        scratch_shapes=[pltpu.VMEM((1, mq), F32), pltpu.VMEM((1, mq), F32), pltpu.VMEM((VT_ROWS, mq), F32),
                        pltpu.VMEM((tk, mq), F32), pltpu.VMEM((tk, mq), BF16)],
        compiler_params=_cparams(("parallel", "parallel", "arbitrary")),
        name="flash_diff" if diff else "flash_mla",
    )(*args)


def _band_kernel(q_ref, k_ref, v_ref, bias_ref, o_ref, m_sc, l_sc, acc_sc, *, tq, nrel):
    i = pl.program_id(2)
    q = q_ref[...]
    m_sc[...] = jnp.full(m_sc.shape, NEG, F32)
    l_sc[...] = jnp.zeros(l_sc.shape, F32)
    acc_sc[...] = jnp.zeros(acc_sc.shape, F32)
    r = lax.broadcasted_iota(jnp.int32, (tq, tq), 0)
    c = lax.broadcasted_iota(jnp.int32, (tq, tq), 1)

    def step(rel):
        start = pl.multiple_of((i - rel) * tq, tq)
        k = k_ref[pl.ds(start, tq), :]
        v = v_ref[pl.ds(start, tq), :]
        s = _dot_nt(q, k) + bias_ref[0, rel]
        q_ch = (rel * tq + r) // CHUNK
        k_ch = c // CHUNK
        s = jnp.where((k_ch <= q_ch) & (k_ch >= q_ch - C_PAST_CHUNKS), s, NEG)
        m_prev = m_sc[...]
        m_new = jnp.maximum(m_prev, jnp.max(s, axis=-1, keepdims=True))
        alpha = jnp.exp(m_prev - m_new)
        p = jnp.exp(s - m_new)
        l_sc[...] = alpha * l_sc[...] + jnp.sum(p, axis=-1, keepdims=True)
        acc_sc[...] = alpha * acc_sc[...] + _dot(p.astype(BF16), v)
        m_sc[...] = m_new

    step(0)
    for rel in range(1, nrel):
        pl.when(i >= rel)(functools.partial(step, rel))
    o_ref[...] = (acc_sc[...] / l_sc[...]).astype(o_ref.dtype)


def _band_bias(table, tq, nrel):
    heads = table.shape[0]
    c_all = nrel * tq
    n = tq + c_all - 1
    n_lo = tq - 1 - REL_CLIP
    n_hi = n - n_lo - (2 * REL_CLIP + 1)
    assert n_hi >= 0
    u = jnp.concatenate([jnp.broadcast_to(table[:, :1], (heads, max(n_lo, 0))), table[:, max(-n_lo, 0):],
                         jnp.broadcast_to(table[:, -1:], (heads, n_hi))], axis=1)
    w = jnp.concatenate([u[:, ::-1], jnp.zeros((heads, 1), table.dtype)], axis=1)
    m = jnp.tile(w, (1, tq))[:, :tq * n].reshape(heads, tq, n)
    t = m[:, :, tq - 1:tq - 1 + c_all]
    return t.reshape(heads, tq, nrel, tq).transpose(0, 2, 1, 3)[:, ::-1]


def _band(q, k, v, bias, *, batch, tq):
    n = q.shape[0]
    seq = n // batch
    nq = seq // tq
    nrel = bias.shape[1]
    return pl.pallas_call(
        functools.partial(_band_kernel, tq=tq, nrel=nrel),
        grid=(batch, C_HEADS, nq),
        in_specs=[pl.BlockSpec((tq, LANE), lambda b, h, i: (b * nq + i, h)),
                  pl.BlockSpec((seq, LANE), lambda b, h, i: (b, h)),
                  pl.BlockSpec((seq, LANE), lambda b, h, i: (b, h)),
                  pl.BlockSpec((1, nrel, tq, tq), lambda b, h, i: (h, 0, 0, 0))],
        out_specs=pl.BlockSpec((tq, LANE), lambda b, h, i: (b * nq + i, h)),
        out_shape=jax.ShapeDtypeStruct((n, C_W), BF16),
        scratch_shapes=[pltpu.VMEM((tq, 1), F32), pltpu.VMEM((tq, 1), F32), pltpu.VMEM((tq, LANE), F32)],
        compiler_params=_cparams(("parallel", "parallel", "arbitrary")),
        name="band",
    )(q, k, v, bias)


def _decode_kernel(*refs, diff, biased):
    refs = list(refs)
    q_ref, kc_ref, vc_ref, kn_ref, vn_ref = refs[:5]
    rest = refs[5:]
    if biased:
        bc_ref, bn_ref, mc_ref, mn_ref = rest[:4]
        rest = rest[4:]
    if diff:
        lam_ref, sg_ref = rest[:2]
        rest = rest[2:]
    o_ref = rest[0]
    q = q_ref[...]
    rows = q.shape[0]
    if diff:
        q = _split_components(q)
    s_c = _dot_nt(q, kc_ref[0].astype(BF16))
    s_n = _dot_nt(q, kn_ref[...])
    if biased:
        s_c = jnp.where(mc_ref[...] > 0, s_c + bc_ref[0], NEG)
        s_n = jnp.where(mn_ref[...] > 0, s_n + bn_ref[0], NEG)
    m = jnp.maximum(jnp.max(s_c, axis=-1, keepdims=True), jnp.max(s_n, axis=-1, keepdims=True))
    p_c = jnp.exp(s_c - m)
    p_n = jnp.exp(s_n - m)
    l = jnp.sum(p_c, axis=-1, keepdims=True) + jnp.sum(p_n, axis=-1, keepdims=True)
    o = (_dot(p_c.astype(BF16), vc_ref[0].astype(BF16)) + _dot(p_n.astype(BF16), vn_ref[...])) / l
    if diff:
        o = _diff_out(o, rows, lam_ref, sg_ref)
    o_ref[...] = o.astype(o_ref.dtype)


def _decode(q, kc, vc, kn, vn, *, heads, dk, rows, stream0=0, diff=False, lam=None, sg=None, bias=None):
    n = q.shape[0]
    streams, past = n // rows, kc.shape[1]
    in_specs = [pl.BlockSpec((rows, dk), lambda b, h: (b, h)),
                pl.BlockSpec((1, past, dk), lambda b, h: (stream0 + b, 0, h)),
                pl.BlockSpec((1, past, LANE), lambda b, h: (stream0 + b, 0, h)),
                pl.BlockSpec((rows, dk), lambda b, h: (b, h)),
                pl.BlockSpec((rows, LANE), lambda b, h: (b, h))]
    args = [q, kc, vc, kn, vn]
    if bias is not None:
        bias_c, bias_n, mask_c, mask_n = bias
        in_specs += [pl.BlockSpec((1, rows, past), lambda b, h: (h, 0, 0)),
                     pl.BlockSpec((1, rows, rows), lambda b, h: (h, 0, 0)),
                     pl.BlockSpec((rows, past), lambda b, h: (0, 0)),
                     pl.BlockSpec((rows, rows), lambda b, h: (0, 0))]
        args += [bias_c, bias_n, mask_c, mask_n]
    if diff:
        in_specs += [pl.BlockSpec((1, LANE), lambda b, h: (0, 0))] * 2
        args += [lam, sg]
    return pl.pallas_call(
        functools.partial(_decode_kernel, diff=diff, biased=bias is not None),
        grid=(streams, heads),
        in_specs=in_specs,
        out_specs=pl.BlockSpec((rows, LANE), lambda b, h: (b, h)),
        out_shape=jax.ShapeDtypeStruct((n, heads * LANE), BF16),
        compiler_params=_cparams(("parallel", "parallel")),
        name="decode",
    )(*args)


def _merge_kernel(x_ref, g_ref, oa_ref, ob_ref, oc_ref, wga_ref, wgb_ref, wgc_ref, bga_ref, bgb_ref, bgc_ref,
                  wba_ref, wbb_ref, wbc_ref, wo_ref, out_ref, h_sc):
    j = pl.program_id(1)

    @pl.when(j == 0)
    def _():
        x = x_ref[...]
        h_sc[...] = (_rms(x, x.shape[1]) * g_ref[...]).astype(BF16)
        out_ref[...] = x

    h = h_sc[...]

    def branch(o_ref, wg_ref, bg_ref, wb_ref):
        gate = jax.nn.sigmoid(_dot(h, wg_ref[...]) + bg_ref[...])
        return gate * _dot(o_ref[...], wb_ref[...])

    merged = (branch(oa_ref, wga_ref, bga_ref, wba_ref) + branch(ob_ref, wgb_ref, bgb_ref, wbb_ref)
              + branch(oc_ref, wgc_ref, bgc_ref, wbc_ref))
    out_ref[...] += _dot(merged.astype(BF16), wo_ref[...])


def _merge(x, oa, ob, oc, lw, *, bm, bn):
    n, d_model = x.shape
    nbn = d_model // bn

    def row(i, j):
        return (i, 0)

    def const(i, j):
        return (0, 0)

    def col(k):
        return lambda i, j: (0, k * nbn + j)

    in_specs = [pl.BlockSpec((bm, d_model), row), pl.BlockSpec((1, d_model), const),
                pl.BlockSpec((bm, AV_W), row), pl.BlockSpec((bm, BV_W), row), pl.BlockSpec((bm, C_W), row),
                pl.BlockSpec((d_model, bn), col(0)), pl.BlockSpec((d_model, bn), col(1)),
                pl.BlockSpec((d_model, bn), col(2)),
                pl.BlockSpec((1, bn), col(0)), pl.BlockSpec((1, bn), col(1)), pl.BlockSpec((1, bn), col(2)),
                pl.BlockSpec((AV_W, bn), col(0)), pl.BlockSpec((BV_W, bn), col(0)), pl.BlockSpec((C_W, bn), col(0)),
                pl.BlockSpec((bn, d_model), lambda i, j: (j, 0))]
    return pl.pallas_call(
        _merge_kernel,
        grid=(n // bm, nbn),
        in_specs=in_specs,
        out_specs=pl.BlockSpec((bm, d_model), row),
        out_shape=jax.ShapeDtypeStruct((n, d_model), F32),
        scratch_shapes=[pltpu.VMEM((bm, d_model), BF16)],
        compiler_params=_cparams(("parallel", "arbitrary")),
        name="merge",
    )(x, lw['g_attn'], oa, ob, oc, lw['w_g'], lw['w_g'], lw['w_g'], lw['b_g'], lw['b_g'], lw['b_g'],
      lw['w_br_a'], lw['w_br_b'], lw['w_br_c'], lw['w_o'])


def _mlp_kernel(x_ref, g_ref, wu_ref, wd_ref, out_ref, h_sc):
    j = pl.program_id(1)

    @pl.when(j == 0)
    def _():
        x = x_ref[...]
        h_sc[...] = (_rms(x, x.shape[1]) * g_ref[...]).astype(BF16)
        out_ref[...] = x

    u = jnp.maximum(_dot(h_sc[...], wu_ref[...]), 0.0)
    out_ref[...] += _dot((u * u).astype(BF16), wd_ref[...])


def _mlp(x, lw, *, bm, bf):
    n, d_model = x.shape
    d_ff = lw['w_up'].shape[1]
    return pl.pallas_call(
        _mlp_kernel,
        grid=(n // bm, d_ff // bf),
        in_specs=[pl.BlockSpec((bm, d_model), lambda i, j: (i, 0)), pl.BlockSpec((1, d_model), lambda i, j: (0, 0)),
                  pl.BlockSpec((d_model, bf), lambda i, j: (0, j)), pl.BlockSpec((bf, d_model), lambda i, j: (j, 0))],
        out_specs=pl.BlockSpec((bm, d_model), lambda i, j: (i, 0)),
        out_shape=jax.ShapeDtypeStruct((n, d_model), F32),
        scratch_shapes=[pltpu.VMEM((bm, d_model), BF16)],
        compiler_params=_cparams(("parallel", "arbitrary")),
        name="mlp",
    )(x, lw['g_mlp'], lw['w_up'], lw['w_down'])


def _rope_tables(pos, rot):
    half = rot // 2
    inv = jnp.power(ROPE_THETA, -jnp.arange(half, dtype=F32) / half)
    ang = pos.astype(F32)[:, None] * inv[None, :]
    cos, sin = jnp.cos(ang), jnp.sin(ang)
    npos = pos.shape[0]
    one = jnp.ones((npos, 64 - rot), F32)
    c64 = jnp.concatenate([cos, cos, one], axis=1)
    s1 = jnp.concatenate([-sin, jnp.zeros((npos, 64 - half), F32)], axis=1)
    s2 = jnp.concatenate([jnp.zeros((npos, half), F32), sin, jnp.zeros((npos, 64 - rot), F32)], axis=1)
    return [jnp.tile(t, (1, 2)) for t in (c64, s1, s2)]


def _layer_weights(l, p):
    d_model = p['w_in'].shape[1]
    w_in = p['w_in'][l]
    o = [0]
    for width in (A_W, A_W, AV_W, B_Q_LORA, B_KV_LORA + B_ROPE, C_W, C_W, C_W):
        o.append(o[-1] + width)
    cols = lambda a, b: w_in[:, a:b]
    w_a = jnp.concatenate([
        cols(o[0], o[1]), cols(o[1], o[2]), cols(o[2], o[3]), cols(o[3], o[4]),
        cols(o[4], o[4] + B_KV_LORA), cols(o[5], o[6]), cols(o[6], o[7]), cols(o[7], o[8]),
        cols(o[4] + B_KV_LORA, o[5]), jnp.zeros((d_model, LANE - B_ROPE), F32)], axis=1).astype(BF16)
    w_qb = p['w_qb'][l].reshape(B_Q_LORA, B_HEADS, B_QK)
    w_qb = jnp.concatenate([w_qb[:, :, :B_NOPE].reshape(B_Q_LORA, -1),
                            w_qb[:, :, B_NOPE:].reshape(B_Q_LORA, -1)], axis=1).astype(BF16)
    w_kvb = p['w_kvb'][l].reshape(B_KV_LORA, B_HEADS, B_NOPE + B_V_DIM)
    w_kvb = jnp.concatenate([w_kvb[:, :, :B_NOPE].reshape(B_KV_LORA, -1),
                             w_kvb[:, :, B_NOPE:].reshape(B_KV_LORA, -1)], axis=1).astype(BF16)
    lam_init = 0.8 - 0.6 * math.exp(-0.3 * l)
    lf = p['a_lambda'][l].astype(F32)
    lam = jnp.exp(jnp.sum(lf[0] * lf[1])) - jnp.exp(jnp.sum(lf[2] * lf[3])) + lam_init

    def tiled(g, reps, scale=1.0):
        return jnp.tile(g * scale, reps)[None, :]

    bqn = p['b_q_norm'][l]
    bkn = p['b_k_norm'][l]
    return {
        'g_attn': p['attn_norm'][l][None, :],
        'g_mlp': p['mlp_norm'][l][None, :],
        'w_a': w_a,
        'w_g': w_in[:, o[8]:].astype(BF16),
        'b_g': p['b_gate'][l][None, :],
        'w_qb': w_qb,
        'w_kvb': w_kvb,
        'gaq': tiled(p['a_q_norm'][l], 2 * A_HEADS, A_QK_DIM ** -0.5),
        'gak': tiled(p['a_k_norm'][l], 2 * A_HEADS),
        'gbqa': p['b_qa_norm'][l][None, :],
        'gbkv': p['b_kv_norm'][l][None, :],
        'gbqn': tiled(bqn[:B_NOPE], 1, B_QK ** -0.5),
        'gbqr': tiled(bqn[B_NOPE:], 2, B_QK ** -0.5),
        'gaq_flash': tiled(p['a_q_norm'][l], 2 * A_HEADS, A_QK_DIM ** -0.5 * LOG2E),
        'gbqn_flash': tiled(bqn[:B_NOPE], 1, B_QK ** -0.5 * LOG2E),
        'gbqr_flash': tiled(bqn[B_NOPE:], 2, B_QK ** -0.5 * LOG2E),
        'gbkn': tiled(bkn[:B_NOPE], 1),
        'gbkr': tiled(bkn[B_NOPE:], 2),
        'gcq': tiled(p['c_q_norm'][l], C_HEADS, C_DIM ** -0.5),
        'gck': tiled(p['c_k_norm'][l], C_HEADS),
        'lam': jnp.broadcast_to(lam, (1, LANE)).astype(F32),
        'sg': (p['a_sub_norm'][l] * (1.0 - lam_init))[None, :],
        'w_br_a': p['w_br_a'][l].astype(BF16),
        'w_br_b': p['w_br_b'][l].astype(BF16),
        'w_br_c': p['w_br_c'][l].astype(BF16),
        'w_o': p['w_o'][l].astype(BF16),
        'w_up': p['w_up'][l].astype(BF16),
        'w_down': p['w_down'][l].astype(BF16),
    }


def _pick(n, prefs):
    for b in prefs:
        if n % b == 0:
            return b
    raise ValueError(f"no block size in {prefs} divides {n}")


def kernel(x_prompt, x_sample, cache_a_k, cache_a_v, cache_b_ckv, cache_b_kpe, cache_c_k, cache_c_v,
           attn_norm, w_in, b_gate, a_q_norm, a_k_norm, a_lambda, a_sub_norm,
           b_qa_norm, b_kv_norm, w_qb, w_kvb, b_q_norm, b_k_norm,
           c_q_norm, c_k_norm, c_rel_bias, w_br_a, w_br_b, w_br_c, w_o,
           mlp_norm, w_up, w_down):
    p = dict(attn_norm=attn_norm, w_in=w_in, b_gate=b_gate, a_q_norm=a_q_norm, a_k_norm=a_k_norm,
             a_lambda=a_lambda, a_sub_norm=a_sub_norm, b_qa_norm=b_qa_norm, b_kv_norm=b_kv_norm,
             w_qb=w_qb, w_kvb=w_kvb, b_q_norm=b_q_norm, b_k_norm=b_k_norm, c_q_norm=c_q_norm,
             c_k_norm=c_k_norm, w_br_a=w_br_a, w_br_b=w_br_b, w_br_c=w_br_c, w_o=w_o, mlp_norm=mlp_norm,
             w_up=w_up, w_down=w_down)
    depth = w_in.shape[0]
    batch, seq, d_model = x_prompt.shape
    streams, dec, _ = x_sample.shape
    past = cache_a_k.shape[2]
    c_past = cache_c_k.shape[2]
    c_keep = min(C_WINDOW, seq)
    n_p, n_s = batch * seq, streams * dec
    assert seq % CHUNK == 0 and c_keep % CHUNK == 0

    pos_p = jnp.arange(seq)
    pos_s = past + jnp.arange(dec)
    tabs_p = _rope_tables(pos_p, A_ROT) + _rope_tables(pos_p, B_ROPE)
    tabs_s = [jnp.tile(t, (streams, 1)) for t in _rope_tables(pos_s, A_ROT) + _rope_tables(pos_s, B_ROPE)]

    kc_pos = jnp.concatenate([past - c_past + jnp.arange(c_past), pos_s])
    q_ch, k_ch = pos_s // CHUNK, kc_pos // CHUNK
    c_mask = ((k_ch[None, :] <= q_ch[:, None]) & (k_ch[None, :] >= q_ch[:, None] - C_PAST_CHUNKS)).astype(F32)
    rel_s = jnp.clip(pos_s[:, None] - kc_pos[None, :], -REL_CLIP, REL_CLIP) + REL_CLIP

    bm_in = _pick(seq, (256, 128, 64))
    bm_tok = _pick(seq, (512, 256, 128, 64))
    t_a = _pick(seq, (512, 256, 128))
    t_b = _pick(seq, (1024, 512, 256, 128))
    tk_ab = _pick(seq, (512, 256, 128))
    t_c = _pick(seq, (256, 128, 64))
    nrel_c = C_WINDOW // t_c + 1
    bn_merge = 512
    bf_mlp = 512
    bm_cache = _pick(streams * past, (512, 256, 128, 64, 16))

    ca_k = cache_a_k.reshape(depth * streams, past, A_W)
    ca_v = cache_a_v.reshape(depth * streams, past, AV_W)
    cb_ckv = cache_b_ckv.reshape(depth * streams * past, B_KV_LORA)
    cb_kpe = cache_b_kpe.reshape(depth * streams * past, B_ROPE)
    cc_k = cache_c_k.reshape(depth * streams, c_past, C_W)
    cc_v = cache_c_v.reshape(depth * streams, c_past, C_W)

    xp = x_prompt.reshape(n_p, d_model)
    xs = x_sample.reshape(n_s, d_model)
    outs = [[] for _ in range(12)]
    for l in range(depth):
        lw = _layer_weights(l, p)

        (qa, kaf, kab, vaf, vab, qb, ckv, kpe, qc, kcb, vcb, kcf, vcf) = _in_proj(
            xp, lw, tabs_p, bm=bm_in, seg_rows=seq, keep=c_keep, transposed=True)
        kb, vb = _mla_kv(ckv, kpe, lw, bm=bm_tok, transposed=True)
        oa = _flash(qa, kab, vab, batch=batch, heads=A_HEADS, dk=LANE, tq=t_a, tk=tk_ab, diff=True,
                    lam=lw['lam'], sg=lw['sg'])
        ob = _flash(qb, kb, vb, batch=batch, heads=B_HEADS, dk=B_PAD, tq=t_b, tk=tk_ab, diff=False)
        oc = _band(qc, kcb, vcb, _band_bias(c_rel_bias[l], t_c, nrel_c), batch=batch, tq=t_c)
        xp = _merge(xp, oa, ob, oc, lw, bm=bm_tok, bn=bn_merge)
        xp = _mlp(xp, lw, bm=bm_tok, bf=bf_mlp)
        for dst, val in zip(outs[:6], (kaf, vaf, ckv, kpe, kcf, vcf)):
            dst.append(val)

        (qa, kaf, kab, vaf, vab, qb, ckv, kpe, qc, kcb, vcb, kcf, vcf) = _in_proj(
            xs, lw, tabs_s, bm=n_s, seg_rows=n_s, keep=n_s, transposed=False)
        kb, vb = _mla_kv(ckv, kpe, lw, bm=n_s)
        kb_c, vb_c = _mla_kv(cb_ckv, cb_kpe, lw, bm=bm_cache, n=streams * past, row0=l * streams * past)
        oa = _decode(qa, ca_k, ca_v, kab, vab, heads=A_HEADS, dk=LANE, rows=dec, stream0=l * streams,
                     diff=True, lam=lw['lam'], sg=lw['sg'])
        ob = _decode(qb, kb_c.reshape(streams, past, B_HEADS * B_PAD), vb_c.reshape(streams, past, BV_W),
                     kb, vb, heads=B_HEADS, dk=B_PAD, rows=dec)
        bias_s = c_rel_bias[l][:, rel_s]
        oc = _decode(qc, cc_k, cc_v, kcb, vcb, heads=C_HEADS, dk=LANE, rows=dec, stream0=l * streams,
                     bias=(bias_s[:, :, :c_past], bias_s[:, :, c_past:], c_mask[:, :c_past], c_mask[:, c_past:]))
        xs = _merge(xs, oa, ob, oc, lw, bm=n_s, bn=bn_merge)
        xs = _mlp(xs, lw, bm=n_s, bf=bf_mlp)
        for dst, val in zip(outs[6:], (kaf, vaf, ckv, kpe, kcf, vcf)):
            dst.append(val)

    st = [jnp.stack(o) for o in outs]
    return (xp.reshape(batch, seq, d_model), xs.reshape(streams, dec, d_model),
            st[0].reshape(depth, batch, seq, A_HEADS, 2 * A_QK_DIM),
            st[1].reshape(depth, batch, seq, A_HEADS, A_V_DIM),
            st[2].reshape(depth, batch, seq, B_KV_LORA),
            st[3].reshape(depth, batch, seq, B_ROPE),
            st[4].reshape(depth, batch, c_keep, C_HEADS, C_DIM),
            st[5].reshape(depth, batch, c_keep, C_HEADS, C_DIM),
            st[6].reshape(depth, streams, dec, A_HEADS, 2 * A_QK_DIM),
            st[7].reshape(depth, streams, dec, A_HEADS, A_V_DIM),
            st[8].reshape(depth, streams, dec, B_KV_LORA),
            st[9].reshape(depth, streams, dec, B_ROPE),
            st[10].reshape(depth, streams, dec, C_HEADS, C_DIM),
            st[11].reshape(depth, streams, dec, C_HEADS, C_DIM))
```

```python
import functools
import math

import jax
import jax.numpy as jnp
from jax import lax
from jax.experimental import pallas as pl
from jax.experimental.pallas import tpu as pltpu

F32 = jnp.float32
BF16 = jnp.bfloat16

CHUNK = 64
ROPE_THETA = 500000.0
EPS = 1e-6
NEG = -1e30
LOG2E = math.log2(math.e)
FLASH_PLAIN_MAX_LOG2 = 50.0
LANE = 128

A_HEADS = 4
A_QK_DIM = 64
A_V_DIM = 128
A_ROT = A_QK_DIM // 4
B_HEADS = 8
B_Q_LORA = 512
B_KV_LORA = 256
B_NOPE = 128
B_ROPE = 64
B_V_DIM = 128
B_QK = B_NOPE + B_ROPE
B_PAD = 2 * LANE
BF16_SUBLANES = 16
VT_ROWS = LANE + BF16_SUBLANES
C_HEADS = 4
C_DIM = 128
C_PAST_CHUNKS = 8
C_WINDOW = C_PAST_CHUNKS * CHUNK
REL_CLIP = 128

A_W = A_HEADS * 2 * A_QK_DIM
AV_W = A_HEADS * A_V_DIM
BV_W = B_HEADS * B_V_DIM
C_W = C_HEADS * C_DIM

OFF_AQ = 0
OFF_AK = OFF_AQ + A_W
OFF_AV = OFF_AK + A_W
OFF_BQ = OFF_AV + AV_W
OFF_CKV = OFF_BQ + B_Q_LORA
OFF_CQ = OFF_CKV + B_KV_LORA
OFF_CK = OFF_CQ + C_W
OFF_CV = OFF_CK + C_W
OFF_KPE = OFF_CV + C_W
WA_COLS = OFF_KPE + LANE

VMEM_LIMIT = 56 * 1024 * 1024


def _cparams(sem):
    return pltpu.CompilerParams(dimension_semantics=sem, vmem_limit_bytes=VMEM_LIMIT)


def _rms(x, width):
    ms = jnp.sum(x * x, axis=-1, keepdims=True) * (1.0 / width)
    return x * lax.rsqrt(ms + EPS)


def _rms_half(zc, lane_lo):
    sq = zc * zc
    lo = jnp.sum(jnp.where(lane_lo, sq, 0.0), axis=-1, keepdims=True)
    hi = jnp.sum(jnp.where(lane_lo, 0.0, sq), axis=-1, keepdims=True)
    ms = jnp.where(lane_lo, lo, hi) * (1.0 / 64.0)
    return zc * lax.rsqrt(ms + EPS)


def _rope(y, c, s1, s2, half):
    return y * c + pltpu.roll(y, LANE - half, 1) * s1 + pltpu.roll(y, half, 1) * s2


def _put_values_t(ref, head, val):
    base = head * VT_ROWS
    ref[base:base + LANE, :] = val.T.astype(BF16)
    ref[base + LANE:base + VT_ROWS, :] = jnp.ones((VT_ROWS - LANE, val.shape[0]), BF16)


def _dot(a, b):
    return jnp.dot(a, b, preferred_element_type=F32)


def _dot_nt(a, b):
    return lax.dot_general(a, b, (((1,), (1,)), ((), ())), preferred_element_type=F32)


def _in_proj_kernel(x_ref, g_ref, wa_ref, wqb_ref, gaq_ref, gak_ref, gbqa_ref, gbkv_ref, gbqn_ref, gbqr_ref,
                    gcq_ref, gck_ref, ca_ref, s1a_ref, s2a_ref, cb_ref, s1b_ref, s2b_ref,
                    qa_ref, kaf_ref, kab_ref, vaf_ref, vab_ref, qb_ref, ckv_ref, kpe_ref,
                    qc_ref, kcb_ref, vcb_ref, kcf_ref, vcf_ref, *, transposed):
    d_model = x_ref.shape[1]

    def put(ref, c, val):
        if transposed:
            ref[c * LANE:(c + 1) * LANE, :] = val.T.astype(BF16)
        else:
            ref[:, c * LANE:(c + 1) * LANE] = val.astype(BF16)

    h = (_rms(x_ref[...], d_model) * g_ref[...]).astype(BF16)
    lane_lo = lax.broadcasted_iota(jnp.int32, (1, LANE), 1) < 64
    ca, s1a, s2a = ca_ref[...], s1a_ref[...], s2a_ref[...]
    cb, s1b, s2b = cb_ref[...], s1b_ref[...], s2b_ref[...]

    def proj(off, width):
        return _dot(h, wa_ref[:, off:off + width])

    def chunk(c):
        return slice(c * LANE, (c + 1) * LANE)

    z = proj(OFF_AQ, A_W)
    for c in range(A_W // LANE):
        y = _rms_half(z[:, chunk(c)], lane_lo) * gaq_ref[:, chunk(c)]
        put(qa_ref, c, _rope(y, ca, s1a, s2a, A_ROT // 2))
    z = proj(OFF_AK, A_W)
    for c in range(A_W // LANE):
        y = _rms_half(z[:, chunk(c)], lane_lo) * gak_ref[:, chunk(c)]
        y = _rope(y, ca, s1a, s2a, A_ROT // 2)
        kaf_ref[:, chunk(c)] = y
        kab_ref[:, chunk(c)] = y.astype(BF16)
    z = proj(OFF_AV, AV_W)
    vaf_ref[...] = z
    for c in range(A_HEADS):
        if transposed:
            _put_values_t(vab_ref, c, z[:, chunk(c)])
        else:
            put(vab_ref, c, z[:, chunk(c)])

    z = proj(OFF_BQ, B_Q_LORA)
    cq = (_rms(z, B_Q_LORA) * gbqa_ref[...]).astype(BF16)
    qraw = _dot(cq, wqb_ref[...])
    nope_w = B_HEADS * B_NOPE
    roped = [_rope(qraw[:, nope_w + c * LANE:nope_w + (c + 1) * LANE], cb, s1b, s2b, B_ROPE // 2)
             for c in range(B_HEADS * B_ROPE // LANE)]
    for hh in range(B_HEADS):
        nope = qraw[:, hh * B_NOPE:(hh + 1) * B_NOPE]
        in_half = lane_lo if hh % 2 == 0 else jnp.logical_not(lane_lo)
        rsel = jnp.where(in_half, roped[hh // 2], 0.0)
        ss = jnp.sum(nope * nope, axis=-1, keepdims=True) + jnp.sum(rsel * rsel, axis=-1, keepdims=True)
        inv = lax.rsqrt(ss * (1.0 / B_QK) + EPS)
        put(qb_ref, 2 * hh, nope * inv * gbqn_ref[...])
        put(qb_ref, 2 * hh + 1, rsel * inv * gbqr_ref[...])
    z = proj(OFF_CKV, B_KV_LORA)
    ckv_ref[...] = _rms(z, B_KV_LORA) * gbkv_ref[...]
    z = proj(OFF_KPE, LANE)
    kpe_ref[...] = _rope(z, cb, s1b, s2b, B_ROPE // 2)[:, :B_ROPE]

    z = proj(OFF_CQ, C_W)
    for c in range(C_HEADS):
        put(qc_ref, c, _rms(z[:, chunk(c)], C_DIM) * gcq_ref[:, chunk(c)])
    z = proj(OFF_CK, C_W)
    for c in range(C_HEADS):
        y = _rms(z[:, chunk(c)], C_DIM) * gck_ref[:, chunk(c)]
        kcf_ref[:, chunk(c)] = y
        kcb_ref[:, chunk(c)] = y.astype(BF16)
    z = proj(OFF_CV, C_W)
    vcf_ref[...] = z
    for c in range(C_HEADS):
        if transposed:
            _put_values_t(vcb_ref, c, z[:, chunk(c)])
        else:
            put(vcb_ref, c, z[:, chunk(c)])


def _in_proj(x, lw, tabs, *, bm, seg_rows, keep, transposed):
    n, d_model = x.shape
    nb = n // bm
    nbs = seg_rows // bm
    nbk = keep // bm
    ntab = tabs[0].shape[0] // bm

    def row(i):
        return (i, 0)

    def const(i):
        return (0, 0)

    def tab(i):
        return (i % ntab, 0)

    def kept(i):
        return ((i // nbs) * nbk + jnp.maximum(i % nbs - (nbs - nbk), 0), 0)

    def rows(width):
        return pl.BlockSpec((bm, width), row)

    def full(a):
        return pl.BlockSpec(a.shape, const, pipeline_mode=pl.Buffered(1))

    fl = '_flash' if transposed else ''
    gains = [lw['gaq' + fl], lw['gak'], lw['gbqa'], lw['gbkv'], lw['gbqn' + fl], lw['gbqr' + fl],
             lw['gcq' + fl], lw['gck']]
    in_specs = ([rows(d_model), full(lw['g_attn']), full(lw['w_a']), full(lw['w_qb'])]
                + [full(g) for g in gains]
                + [pl.BlockSpec((bm, LANE), tab) for _ in tabs])
    out_shape = [
        jax.ShapeDtypeStruct((n, A_W), BF16),
        jax.ShapeDtypeStruct((n, A_W), F32),
        jax.ShapeDtypeStruct((n, A_W), BF16),
        jax.ShapeDtypeStruct((n, AV_W), F32),
        jax.ShapeDtypeStruct((n, AV_W), BF16),
        jax.ShapeDtypeStruct((n, B_HEADS * B_PAD), BF16),
        jax.ShapeDtypeStruct((n, B_KV_LORA), F32),
        jax.ShapeDtypeStruct((n, B_ROPE), F32),
        jax.ShapeDtypeStruct((n, C_W), BF16),
        jax.ShapeDtypeStruct((n, C_W), BF16),
        jax.ShapeDtypeStruct((n, C_W), BF16),
        jax.ShapeDtypeStruct((n // seg_rows * keep, C_W), F32),
        jax.ShapeDtypeStruct((n // seg_rows * keep, C_W), F32),
    ]
    out_specs = [rows(s.shape[1]) for s in out_shape[:11]] + [pl.BlockSpec((bm, C_W), kept)] * 2
    if transposed:
        for k, width in ((0, A_W), (4, A_HEADS * VT_ROWS), (5, B_HEADS * B_PAD), (8, C_W),
                         (10, C_HEADS * VT_ROWS)):
            out_shape[k] = jax.ShapeDtypeStruct((width, n), BF16)
            out_specs[k] = pl.BlockSpec((width, bm), lambda i: (0, i))
    return pl.pallas_call(
        functools.partial(_in_proj_kernel, transposed=transposed),
        grid=(nb,),
        in_specs=in_specs,
        out_specs=out_specs,
        out_shape=out_shape,
        compiler_params=_cparams(("arbitrary",)),
        name="in_proj",
    )(x, lw['g_attn'], lw['w_a'], lw['w_qb'], *gains, *tabs)


def _mla_kv_kernel(ckv_ref, kpe_ref, w_ref, gn_ref, gr_ref, kb_ref, vb_ref, *, transposed):
    ckv = ckv_ref[...]
    kv = _dot(ckv.astype(BF16), w_ref[...])
    kpe = kpe_ref[...]
    kpe2 = jnp.concatenate([kpe, kpe], axis=1)
    ss_pe = jnp.sum(kpe * kpe, axis=-1, keepdims=True)
    lane_lo = lax.broadcasted_iota(jnp.int32, (1, LANE), 1) < 64
    nope_w = B_HEADS * B_NOPE
    for hh in range(B_HEADS):
        nope = kv[:, hh * B_NOPE:(hh + 1) * B_NOPE]
        ss = jnp.sum(nope * nope, axis=-1, keepdims=True) + ss_pe
        inv = lax.rsqrt(ss * (1.0 / B_QK) + EPS)
        in_half = lane_lo if hh % 2 == 0 else jnp.logical_not(lane_lo)
        kb_ref[:, hh * B_PAD:hh * B_PAD + LANE] = (nope * inv * gn_ref[...]).astype(BF16)
        kb_ref[:, hh * B_PAD + LANE:(hh + 1) * B_PAD] = jnp.where(in_half, kpe2 * inv * gr_ref[...], 0.0).astype(BF16)
    if transposed:
        for c in range(B_HEADS):
            _put_values_t(vb_ref, c, kv[:, nope_w + c * LANE:nope_w + (c + 1) * LANE])
    else:
        vb_ref[...] = kv[:, nope_w:].astype(BF16)


def _mla_kv(ckv, kpe, lw, *, bm, n=None, row0=0, transposed=False):
    n = ckv.shape[0] if n is None else n
    blk0 = row0 // bm
    assert row0 % bm == 0
    if transposed:
        vb_shape, vb_spec = (B_HEADS * VT_ROWS, n), pl.BlockSpec((B_HEADS * VT_ROWS, bm), lambda i: (0, i))
    else:
        vb_shape, vb_spec = (n, BV_W), pl.BlockSpec((bm, BV_W), lambda i: (i, 0))

    def row(i):
        return (i, 0)

    def src_row(i):
        return (blk0 + i, 0)

    def const(i):
        return (0, 0)

    return pl.pallas_call(
        functools.partial(_mla_kv_kernel, transposed=transposed),
        grid=(n // bm,),
        in_specs=[pl.BlockSpec((bm, B_KV_LORA), src_row), pl.BlockSpec((bm, B_ROPE), src_row),
                  pl.BlockSpec(lw['w_kvb'].shape, const), pl.BlockSpec((1, LANE), const),
                  pl.BlockSpec((1, LANE), const)],
        out_specs=[pl.BlockSpec((bm, B_HEADS * B_PAD), row), vb_spec],
        out_shape=[jax.ShapeDtypeStruct((n, B_HEADS * B_PAD), BF16), jax.ShapeDtypeStruct(vb_shape, BF16)],
        compiler_params=_cparams(("arbitrary",)),
        name="mla_kv",
    )(ckv, kpe, lw['w_kvb'], lw['gbkn'], lw['gbkr'])


def _split_components(q):
    lane_lo = lax.broadcasted_iota(jnp.int32, (1, LANE), 1) < 64
    zero = jnp.zeros_like(q)
    return jnp.concatenate([jnp.where(lane_lo, q, zero), jnp.where(lane_lo, zero, q)], axis=0)


def _diff_out(o, rows, lam_ref, sg_ref):
    o = o[:rows] - lam_ref[...] * o[rows:]
    return _rms(o, A_V_DIM) * sg_ref[...]


def _flash_queries(qt_ref, diff):
    qt = qt_ref[...]
    if diff:
        row_lo = lax.broadcasted_iota(jnp.int32, (qt.shape[0], 1), 0) < A_QK_DIM
        zero = jnp.zeros_like(qt)
        qt = jnp.concatenate([jnp.where(row_lo, qt, zero), jnp.where(row_lo, zero, qt)], axis=1)
    return qt


def _flash_finish(acc, o_ref, lam_ref, sg_ref, tq, diff):
    ot = acc[:A_V_DIM] / acc[A_V_DIM:A_V_DIM + 1]
    if diff:
        o = (ot[:, :tq] - lam_ref[:, :1] * ot[:, tq:]).T
        o = _rms(o, A_V_DIM) * sg_ref[...]
    else:
        o = ot.T
    o_ref[...] = o.astype(o_ref.dtype)


def _flash_plain_kernel(*refs, tq, tk, diff, group):
    if diff:
        qt_ref, k_ref, vt_ref, mask_ref, lam_ref, sg_ref, o_ref, acc_sc = refs
    else:
        qt_ref, k_ref, vt_ref, mask_ref, o_ref, acc_sc = refs
        lam_ref = sg_ref = None
    i = pl.program_id(2)
    qt = _flash_queries(qt_ref, diff)
    per_q = tq // tk

    def block(j, diag):
        start = pl.multiple_of(j * tk, tk)
        pt = jnp.exp2(_dot(k_ref[pl.ds(start, tk), :], qt))
        if diag is not None:
            pt = jnp.where(mask_ref[diag] > 0, pt, 0.0)
        return _dot(vt_ref[:, pl.ds(start, tk)], pt.astype(BF16))

    acc_sc[...] = jnp.zeros(acc_sc.shape, F32)

    def add_blocks(j0, nblocks):
        acc = acc_sc[...]
        for jj in range(nblocks):
            acc = acc + block(j0 + jj, None)
        acc_sc[...] = acc

    def trips(jb, carry):
        add_blocks(jb * (group * per_q), group * per_q)
        return carry

    def leftover(r, carry):
        add_blocks((i - 1) * per_q, per_q)
        return carry

    lax.fori_loop(0, i // group, trips, 0)
    if group == 2:
        lax.fori_loop(0, i % group, leftover, 0)
    else:
        assert group == 1
    acc = acc_sc[...]
    for jj in range(per_q):
        acc = acc + block(i * per_q + jj, jj)
    _flash_finish(acc, o_ref, lam_ref, sg_ref, tq, diff)


def _flash_kernel(*refs, tq, tk, diff):
    if diff:
        qt_ref, k_ref, vt_ref, mask_ref, lam_ref, sg_ref, o_ref, m_sc, a_sc, acc_sc, s_sc, p_sc = refs
    else:
        qt_ref, k_ref, vt_ref, mask_ref, o_ref, m_sc, a_sc, acc_sc, s_sc, p_sc = refs
        lam_ref = sg_ref = None
    i = pl.program_id(2)
    qt = _flash_queries(qt_ref, diff)
    per_q = tq // tk

    def scores(j):
        start = pl.multiple_of(j * tk, tk)
        return _dot(k_ref[pl.ds(start, tk), :], qt)

    def softmax(st, m_prev, diag):
        if diag is not None:
            st = jnp.where(mask_ref[diag] > 0, st, NEG)
        m_new = jnp.maximum(m_prev, jnp.max(st, axis=0, keepdims=True))
        return m_new, jnp.exp2(m_prev - m_new), jnp.exp2(st - m_new).astype(BF16)

    def weighted_values(j, acc, alpha, pt):
        start = pl.multiple_of(j * tk, tk)
        return alpha * acc + _dot(vt_ref[:, pl.ds(start, tk)], pt)

    m_sc[...] = jnp.full(m_sc.shape, NEG, F32)
    a_sc[...] = jnp.ones(a_sc.shape, F32)
    acc_sc[...] = jnp.zeros(acc_sc.shape, F32)
    p_sc[...] = jnp.zeros(p_sc.shape, BF16)
    s_sc[...] = scores(0)

    def body(jb, carry):
        m, alpha_prev, acc, p_prev, st = m_sc[...], a_sc[...], acc_sc[...], p_sc[...], s_sc[...]
        for jj in range(per_q):
            j = jb * per_q + jj
            st_next = scores(j + 1)
            m, alpha, pt = softmax(st, m, None)
            acc = weighted_values(jnp.maximum(j - 1, 0), acc, alpha_prev, p_prev)
            alpha_prev, p_prev, st = alpha, pt, st_next
        m_sc[...], a_sc[...], acc_sc[...], p_sc[...], s_sc[...] = m, alpha_prev, acc, p_prev, st
        return carry

    lax.fori_loop(0, i, body, 0)
    m, alpha_prev, acc, p_prev, st = m_sc[...], a_sc[...], acc_sc[...], p_sc[...], s_sc[...]
    for jj in range(per_q):
        j = i * per_q + jj
        st_next = scores(j + 1) if jj + 1 < per_q else None
        m, alpha, pt = softmax(st, m, jj)
        acc = weighted_values(jnp.maximum(j - 1, 0), acc, alpha_prev, p_prev)
        alpha_prev, p_prev, st = alpha, pt, st_next
    acc = weighted_values(i * per_q + per_q - 1, acc, alpha_prev, p_prev)
    _flash_finish(acc, o_ref, lam_ref, sg_ref, tq, diff)


def _flash(qt, k, vt, *, batch, heads, dk, tq, tk, diff, plain, group=1, lam=None, sg=None):
    n = k.shape[0]
    seq = n // batch
    nq = seq // tq
    mq = 2 * tq if diff else tq
    per_q = tq // tk
    kpos = jnp.arange(tq, dtype=jnp.int32).reshape(per_q, tk, 1)
    qpos = jnp.arange(mq, dtype=jnp.int32).reshape(1, 1, mq) % tq
    mask = ((kpos // CHUNK) <= (qpos // CHUNK)).astype(F32)
    in_specs = [pl.BlockSpec((dk, tq), lambda b, h, i: (h, b * nq + i)),
                pl.BlockSpec((seq, dk), lambda b, h, i: (b, h)),
                pl.BlockSpec((VT_ROWS, seq), lambda b, h, i: (h, b)),
                pl.BlockSpec((per_q, tk, mq), lambda b, h, i: (0, 0, 0), pipeline_mode=pl.Buffered(1))]
    args = [qt, k, vt, mask]
    if diff:
        in_specs += [pl.BlockSpec((1, LANE), lambda b, h, i: (0, 0))] * 2
        args += [lam, sg]
    scratch = [pltpu.VMEM((VT_ROWS, mq), F32)]
    if not plain:
        scratch = [pltpu.VMEM((1, mq), F32), pltpu.VMEM((1, mq), F32)] + scratch + [
            pltpu.VMEM((tk, mq), F32), pltpu.VMEM((tk, mq), BF16)]
    return pl.pallas_call(
        (functools.partial(_flash_plain_kernel, tq=tq, tk=tk, diff=diff, group=group) if plain
         else functools.partial(_flash_kernel, tq=tq, tk=tk, diff=diff)),
        grid=(batch, heads, nq),
        in_specs=in_specs,
        out_specs=pl.BlockSpec((tq, LANE), lambda b, h, i: (b * nq + i, h)),
        out_shape=jax.ShapeDtypeStruct((n, heads * LANE), BF16),
        scratch_shapes=scratch,
        compiler_params=_cparams(("parallel", "parallel", "arbitrary")),
        name=("flash_diff" if diff else "flash_mla") + ("_plain" if plain else ""),
    )(*args)


def _band_kernel(qt_ref, k_ref, vt_ref, bias_ref, o_ref, *, tq, nrel):
    i = pl.program_id(2)
    qt = qt_ref[...]

    def attend(rels):
        def keys(rel):
            return pl.ds(pl.multiple_of((i - rel) * tq, tq), tq)

        st = [_dot(k_ref[keys(rel), :], qt) + bias_ref[0, rel] for rel in rels]
        m = functools.reduce(jnp.maximum, [jnp.max(s, axis=0, keepdims=True) for s in st])
        acc = sum(_dot(vt_ref[:, keys(rel)], jnp.exp2(s - m).astype(BF16)) for rel, s in zip(rels, st))
        o_ref[...] = (acc[:C_DIM] / acc[C_DIM:C_DIM + 1]).T.astype(o_ref.dtype)

    for avail in range(nrel):
        cond = (i == avail) if avail < nrel - 1 else (i >= avail)
        pl.when(cond)(functools.partial(attend, tuple(range(avail + 1))))


def _band_bias_t(table, tq, nrel):
    bias = jnp.swapaxes(_band_bias(table, tq, nrel), 2, 3) * LOG2E
    k_ch = jnp.arange(tq).reshape(1, tq, 1) // CHUNK
    q_ch = (jnp.arange(nrel).reshape(nrel, 1, 1) * tq + jnp.arange(tq).reshape(1, 1, tq)) // CHUNK
    visible = (k_ch <= q_ch) & (k_ch >= q_ch - C_PAST_CHUNKS)
    return jnp.where(visible[None], bias, NEG)


def _band_bias(table, tq, nrel):
    heads = table.shape[0]
    c_all = nrel * tq
    n = tq + c_all - 1
    n_lo = tq - 1 - REL_CLIP
    n_hi = n - n_lo - (2 * REL_CLIP + 1)
    assert n_hi >= 0
    u = jnp.concatenate([jnp.broadcast_to(table[:, :1], (heads, max(n_lo, 0))), table[:, max(-n_lo, 0):],
                         jnp.broadcast_to(table[:, -1:], (heads, n_hi))], axis=1)
    w = jnp.concatenate([u[:, ::-1], jnp.zeros((heads, 1), table.dtype)], axis=1)
    m = jnp.tile(w, (1, tq))[:, :tq * n].reshape(heads, tq, n)
    t = m[:, :, tq - 1:tq - 1 + c_all]
    return t.reshape(heads, tq, nrel, tq).transpose(0, 2, 1, 3)[:, ::-1]


def _band(qt, k, vt, bias, *, batch, tq):
    n = k.shape[0]
    seq = n // batch
    nq = seq // tq
    nrel = bias.shape[1]
    return pl.pallas_call(
        functools.partial(_band_kernel, tq=tq, nrel=nrel),
        grid=(batch, C_HEADS, nq),
        in_specs=[pl.BlockSpec((LANE, tq), lambda b, h, i: (h, b * nq + i)),
                  pl.BlockSpec((seq, LANE), lambda b, h, i: (b, h)),
                  pl.BlockSpec((VT_ROWS, seq), lambda b, h, i: (h, b)),
                  pl.BlockSpec((1, nrel, tq, tq), lambda b, h, i: (h, 0, 0, 0))],
        out_specs=pl.BlockSpec((tq, LANE), lambda b, h, i: (b * nq + i, h)),
        out_shape=jax.ShapeDtypeStruct((n, C_W), BF16),
        compiler_params=_cparams(("parallel", "parallel", "arbitrary")),
        name="band",
    )(qt, k, vt, bias)


def _decode_kernel(*refs, diff, biased):
    refs = list(refs)
    q_ref, kc_ref, vc_ref, kn_ref, vn_ref = refs[:5]
    rest = refs[5:]
    if biased:
        bc_ref, bn_ref, mc_ref, mn_ref = rest[:4]
        rest = rest[4:]
    if diff:
        lam_ref, sg_ref = rest[:2]
        rest = rest[2:]
    o_ref = rest[0]
    q = q_ref[...]
    rows = q.shape[0]
    if diff:
        q = _split_components(q)
    s_c = _dot_nt(q, kc_ref[0].astype(BF16))
    s_n = _dot_nt(q, kn_ref[...])
    if biased:
        s_c = jnp.where(mc_ref[...] > 0, s_c + bc_ref[0], NEG)
        s_n = jnp.where(mn_ref[...] > 0, s_n + bn_ref[0], NEG)
    m = jnp.maximum(jnp.max(s_c, axis=-1, keepdims=True), jnp.max(s_n, axis=-1, keepdims=True))
    p_c = jnp.exp(s_c - m)
    p_n = jnp.exp(s_n - m)
    l = jnp.sum(p_c, axis=-1, keepdims=True) + jnp.sum(p_n, axis=-1, keepdims=True)
    o = (_dot(p_c.astype(BF16), vc_ref[0].astype(BF16)) + _dot(p_n.astype(BF16), vn_ref[...])) / l
    if diff:
        o = _diff_out(o, rows, lam_ref, sg_ref)
    o_ref[...] = o.astype(o_ref.dtype)


def _decode(q, kc, vc, kn, vn, *, heads, dk, rows, stream0=0, diff=False, lam=None, sg=None, bias=None):
    n = q.shape[0]
    streams, past = n // rows, kc.shape[1]
    in_specs = [pl.BlockSpec((rows, dk), lambda b, h: (b, h)),
                pl.BlockSpec((1, past, dk), lambda b, h: (stream0 + b, 0, h)),
                pl.BlockSpec((1, past, LANE), lambda b, h: (stream0 + b, 0, h)),
                pl.BlockSpec((rows, dk), lambda b, h: (b, h)),
                pl.BlockSpec((rows, LANE), lambda b, h: (b, h))]
    args = [q, kc, vc, kn, vn]
    if bias is not None:
        bias_c, bias_n, mask_c, mask_n = bias
        in_specs += [pl.BlockSpec((1, rows, past), lambda b, h: (h, 0, 0)),
                     pl.BlockSpec((1, rows, rows), lambda b, h: (h, 0, 0)),
                     pl.BlockSpec((rows, past), lambda b, h: (0, 0)),
                     pl.BlockSpec((rows, rows), lambda b, h: (0, 0))]
        args += [bias_c, bias_n, mask_c, mask_n]
    if diff:
        in_specs += [pl.BlockSpec((1, LANE), lambda b, h: (0, 0))] * 2
        args += [lam, sg]
    return pl.pallas_call(
        functools.partial(_decode_kernel, diff=diff, biased=bias is not None),
        grid=(streams, heads),
        in_specs=in_specs,
        out_specs=pl.BlockSpec((rows, LANE), lambda b, h: (b, h)),
        out_shape=jax.ShapeDtypeStruct((n, heads * LANE), BF16),
        compiler_params=_cparams(("parallel", "parallel")),
        name="decode",
    )(*args)


def _merge_kernel(x_ref, g_ref, oa_ref, ob_ref, oc_ref, wga_ref, wgb_ref, wgc_ref, bga_ref, bgb_ref, bgc_ref,
                  wba_ref, wbb_ref, wbc_ref, wo_ref, out_ref, h_sc):
    j = pl.program_id(1)

    @pl.when(j == 0)
    def _():
        x = x_ref[...]
        h_sc[...] = (_rms(x, x.shape[1]) * g_ref[...]).astype(BF16)
        out_ref[...] = x

    h = h_sc[...]

    def branch(o_ref, wg_ref, bg_ref, wb_ref):
        gate = jax.nn.sigmoid(_dot(h, wg_ref[...]) + bg_ref[...])
        return gate * _dot(o_ref[...], wb_ref[...])

    merged = (branch(oa_ref, wga_ref, bga_ref, wba_ref) + branch(ob_ref, wgb_ref, bgb_ref, wbb_ref)
              + branch(oc_ref, wgc_ref, bgc_ref, wbc_ref))
    out_ref[...] += _dot(merged.astype(BF16), wo_ref[...])


def _merge(x, oa, ob, oc, lw, *, bm, bn):
    n, d_model = x.shape
    nbn = d_model // bn

    def row(i, j):
        return (i, 0)

    def const(i, j):
        return (0, 0)

    def col(k):
        return lambda i, j: (0, k * nbn + j)

    in_specs = [pl.BlockSpec((bm, d_model), row), pl.BlockSpec((1, d_model), const),
                pl.BlockSpec((bm, AV_W), row), pl.BlockSpec((bm, BV_W), row), pl.BlockSpec((bm, C_W), row),
                pl.BlockSpec((d_model, bn), col(0)), pl.BlockSpec((d_model, bn), col(1)),
                pl.BlockSpec((d_model, bn), col(2)),
                pl.BlockSpec((1, bn), col(0)), pl.BlockSpec((1, bn), col(1)), pl.BlockSpec((1, bn), col(2)),
                pl.BlockSpec((AV_W, bn), col(0)), pl.BlockSpec((BV_W, bn), col(0)), pl.BlockSpec((C_W, bn), col(0)),
                pl.BlockSpec((bn, d_model), lambda i, j: (j, 0))]
    return pl.pallas_call(
        _merge_kernel,
        grid=(n // bm, nbn),
        in_specs=in_specs,
        out_specs=pl.BlockSpec((bm, d_model), row),
        out_shape=jax.ShapeDtypeStruct((n, d_model), F32),
        scratch_shapes=[pltpu.VMEM((bm, d_model), BF16)],
        compiler_params=_cparams(("parallel", "arbitrary")),
        name="merge",
    )(x, lw['g_attn'], oa, ob, oc, lw['w_g'], lw['w_g'], lw['w_g'], lw['b_g'], lw['b_g'], lw['b_g'],
      lw['w_br_a'], lw['w_br_b'], lw['w_br_c'], lw['w_o'])


def _mlp_kernel(x_ref, g_ref, wu_ref, wd_ref, out_ref, h_sc):
    j = pl.program_id(1)

    @pl.when(j == 0)
    def _():
        x = x_ref[...]
        h_sc[...] = (_rms(x, x.shape[1]) * g_ref[...]).astype(BF16)
        out_ref[...] = x

    u = jnp.maximum(_dot(h_sc[...], wu_ref[...]), 0.0)
    out_ref[...] += _dot((u * u).astype(BF16), wd_ref[...])


def _mlp(x, lw, *, bm, bf):
    n, d_model = x.shape
    d_ff = lw['w_up'].shape[1]
    return pl.pallas_call(
        _mlp_kernel,
        grid=(n // bm, d_ff // bf),
        in_specs=[pl.BlockSpec((bm, d_model), lambda i, j: (i, 0)), pl.BlockSpec((1, d_model), lambda i, j: (0, 0)),
                  pl.BlockSpec((d_model, bf), lambda i, j: (0, j)), pl.BlockSpec((bf, d_model), lambda i, j: (j, 0))],
        out_specs=pl.BlockSpec((bm, d_model), lambda i, j: (i, 0)),
        out_shape=jax.ShapeDtypeStruct((n, d_model), F32),
        scratch_shapes=[pltpu.VMEM((bm, d_model), BF16)],
        compiler_params=_cparams(("parallel", "arbitrary")),
        name="mlp",
    )(x, lw['g_mlp'], lw['w_up'], lw['w_down'])


def _rope_tables(pos, rot):
    half = rot // 2
    inv = jnp.power(ROPE_THETA, -jnp.arange(half, dtype=F32) / half)
    ang = pos.astype(F32)[:, None] * inv[None, :]
    cos, sin = jnp.cos(ang), jnp.sin(ang)
    npos = pos.shape[0]
    one = jnp.ones((npos, 64 - rot), F32)
    c64 = jnp.concatenate([cos, cos, one], axis=1)
    s1 = jnp.concatenate([-sin, jnp.zeros((npos, 64 - half), F32)], axis=1)
    s2 = jnp.concatenate([jnp.zeros((npos, half), F32), sin, jnp.zeros((npos, 64 - rot), F32)], axis=1)
    return [jnp.tile(t, (1, 2)) for t in (c64, s1, s2)]


def _score_bound(gq, gk, d):
    return math.sqrt(d) * LOG2E * jnp.max(jnp.abs(gq)) * jnp.max(jnp.abs(gk))


def _layer_weights(l, p):
    d_model = p['w_in'].shape[1]
    w_in = p['w_in'][l]
    o = [0]
    for width in (A_W, A_W, AV_W, B_Q_LORA, B_KV_LORA + B_ROPE, C_W, C_W, C_W):
        o.append(o[-1] + width)
    cols = lambda a, b: w_in[:, a:b]
    w_a = jnp.concatenate([
        cols(o[0], o[1]), cols(o[1], o[2]), cols(o[2], o[3]), cols(o[3], o[4]),
        cols(o[4], o[4] + B_KV_LORA), cols(o[5], o[6]), cols(o[6], o[7]), cols(o[7], o[8]),
        cols(o[4] + B_KV_LORA, o[5]), jnp.zeros((d_model, LANE - B_ROPE), F32)], axis=1).astype(BF16)
    w_qb = p['w_qb'][l].reshape(B_Q_LORA, B_HEADS, B_QK)
    w_qb = jnp.concatenate([w_qb[:, :, :B_NOPE].reshape(B_Q_LORA, -1),
                            w_qb[:, :, B_NOPE:].reshape(B_Q_LORA, -1)], axis=1).astype(BF16)
    w_kvb = p['w_kvb'][l].reshape(B_KV_LORA, B_HEADS, B_NOPE + B_V_DIM)
    w_kvb = jnp.concatenate([w_kvb[:, :, :B_NOPE].reshape(B_KV_LORA, -1),
                             w_kvb[:, :, B_NOPE:].reshape(B_KV_LORA, -1)], axis=1).astype(BF16)
    lam_init = 0.8 - 0.6 * math.exp(-0.3 * l)
    lf = p['a_lambda'][l].astype(F32)
    lam = jnp.exp(jnp.sum(lf[0] * lf[1])) - jnp.exp(jnp.sum(lf[2] * lf[3])) + lam_init

    def tiled(g, reps, scale=1.0):
        return jnp.tile(g * scale, reps)[None, :]

    bqn = p['b_q_norm'][l]
    bkn = p['b_k_norm'][l]
    return {
        'g_attn': p['attn_norm'][l][None, :],
        'g_mlp': p['mlp_norm'][l][None, :],
        'w_a': w_a,
        'w_g': w_in[:, o[8]:].astype(BF16),
        'b_g': p['b_gate'][l][None, :],
        'w_qb': w_qb,
        'w_kvb': w_kvb,
        'gaq': tiled(p['a_q_norm'][l], 2 * A_HEADS, A_QK_DIM ** -0.5),
        'gak': tiled(p['a_k_norm'][l], 2 * A_HEADS),
        'gbqa': p['b_qa_norm'][l][None, :],
        'gbkv': p['b_kv_norm'][l][None, :],
        'gbqn': tiled(bqn[:B_NOPE], 1, B_QK ** -0.5),
        'gbqr': tiled(bqn[B_NOPE:], 2, B_QK ** -0.5),
        'gaq_flash': tiled(p['a_q_norm'][l], 2 * A_HEADS, A_QK_DIM ** -0.5 * LOG2E),
        'gbqn_flash': tiled(bqn[:B_NOPE], 1, B_QK ** -0.5 * LOG2E),
        'gbqr_flash': tiled(bqn[B_NOPE:], 2, B_QK ** -0.5 * LOG2E),
        'gcq_flash': tiled(p['c_q_norm'][l], C_HEADS, C_DIM ** -0.5 * LOG2E),
        'gbkn': tiled(bkn[:B_NOPE], 1),
        'gbkr': tiled(bkn[B_NOPE:], 2),
        'gcq': tiled(p['c_q_norm'][l], C_HEADS, C_DIM ** -0.5),
        'gck': tiled(p['c_k_norm'][l], C_HEADS),
        'lam': jnp.broadcast_to(lam, (1, LANE)).astype(F32),
        'sg': (p['a_sub_norm'][l] * (1.0 - lam_init))[None, :],
        'bound_a': _score_bound(p['a_q_norm'][l], p['a_k_norm'][l], A_QK_DIM),
        'bound_b': _score_bound(bqn, bkn, B_QK),
        'w_br_a': p['w_br_a'][l].astype(BF16),
        'w_br_b': p['w_br_b'][l].astype(BF16),
        'w_br_c': p['w_br_c'][l].astype(BF16),
        'w_o': p['w_o'][l].astype(BF16),
        'w_up': p['w_up'][l].astype(BF16),
        'w_down': p['w_down'][l].astype(BF16),
    }


def _pick(n, prefs):
    for b in prefs:
        if n % b == 0:
            return b
    raise ValueError(f"no block size in {prefs} divides {n}")


def kernel(x_prompt, x_sample, cache_a_k, cache_a_v, cache_b_ckv, cache_b_kpe, cache_c_k, cache_c_v,
           attn_norm, w_in, b_gate, a_q_norm, a_k_norm, a_lambda, a_sub_norm,
           b_qa_norm, b_kv_norm, w_qb, w_kvb, b_q_norm, b_k_norm,
           c_q_norm, c_k_norm, c_rel_bias, w_br_a, w_br_b, w_br_c, w_o,
           mlp_norm, w_up, w_down):
    p = dict(attn_norm=attn_norm, w_in=w_in, b_gate=b_gate, a_q_norm=a_q_norm, a_k_norm=a_k_norm,
             a_lambda=a_lambda, a_sub_norm=a_sub_norm, b_qa_norm=b_qa_norm, b_kv_norm=b_kv_norm,
             w_qb=w_qb, w_kvb=w_kvb, b_q_norm=b_q_norm, b_k_norm=b_k_norm, c_q_norm=c_q_norm,
             c_k_norm=c_k_norm, w_br_a=w_br_a, w_br_b=w_br_b, w_br_c=w_br_c, w_o=w_o, mlp_norm=mlp_norm,
             w_up=w_up, w_down=w_down)
    depth = w_in.shape[0]
    batch, seq, d_model = x_prompt.shape
    streams, dec, _ = x_sample.shape
    past = cache_a_k.shape[2]
    c_past = cache_c_k.shape[2]
    c_keep = min(C_WINDOW, seq)
    n_p, n_s = batch * seq, streams * dec
    assert seq % CHUNK == 0 and c_keep % CHUNK == 0

    pos_p = jnp.arange(seq)
    pos_s = past + jnp.arange(dec)
    tabs_p = _rope_tables(pos_p, A_ROT) + _rope_tables(pos_p, B_ROPE)
    tabs_s = [jnp.tile(t, (streams, 1)) for t in _rope_tables(pos_s, A_ROT) + _rope_tables(pos_s, B_ROPE)]

    kc_pos = jnp.concatenate([past - c_past + jnp.arange(c_past), pos_s])
    q_ch, k_ch = pos_s // CHUNK, kc_pos // CHUNK
    c_mask = ((k_ch[None, :] <= q_ch[:, None]) & (k_ch[None, :] >= q_ch[:, None] - C_PAST_CHUNKS)).astype(F32)
    rel_s = jnp.clip(pos_s[:, None] - kc_pos[None, :], -REL_CLIP, REL_CLIP) + REL_CLIP

    bm_in = _pick(seq, (256, 128, 64))
    bm_tok = _pick(seq, (512, 256, 128, 64))
    t_a = _pick(seq, (512, 256, 128))
    t_b = _pick(seq, (1024, 512, 256, 128))
    tk_a, tk_b = t_a, t_b // 2
    tp_a = (t_b, t_b // 2)
    tp_b = (t_b, t_b // 2)
    t_c = _pick(seq, (512, 256, 128))
    nrel_c = C_WINDOW // t_c + 1
    bn_merge = 512
    bf_mlp = 512
    bm_cache = _pick(streams * past, (512, 256, 128, 64, 16))

    ca_k = cache_a_k.reshape(depth * streams, past, A_W)
    ca_v = cache_a_v.reshape(depth * streams, past, AV_W)
    cb_ckv = cache_b_ckv.reshape(depth * streams * past, B_KV_LORA)
    cb_kpe = cache_b_kpe.reshape(depth * streams * past, B_ROPE)
    cc_k = cache_c_k.reshape(depth * streams, c_past, C_W)
    cc_v = cache_c_v.reshape(depth * streams, c_past, C_W)

    xp = x_prompt.reshape(n_p, d_model)
    xs = x_sample.reshape(n_s, d_model)
    outs = [[] for _ in range(12)]
    for l in range(depth):
        lw = _layer_weights(l, p)

        (qa, kaf, kab, vaf, vab, qb, ckv, kpe, qc, kcb, vcb, kcf, vcf) = _in_proj(
            xp, lw, tabs_p, bm=bm_in, seg_rows=seq, keep=c_keep, transposed=True)
        kb, vb = _mla_kv(ckv, kpe, lw, bm=bm_tok, transposed=True)
        flash_a = functools.partial(_flash, qa, kab, vab, batch=batch, heads=A_HEADS, dk=LANE, diff=True,
                                    lam=lw['lam'], sg=lw['sg'])
        flash_b = functools.partial(_flash, qb, kb, vb, batch=batch, heads=B_HEADS, dk=B_PAD, diff=False)
        oa = lax.cond(lw['bound_a'] <= FLASH_PLAIN_MAX_LOG2,
                      functools.partial(flash_a, tq=tp_a[0], tk=tp_a[1], plain=True),
                      functools.partial(flash_a, tq=t_a, tk=tk_a, plain=False))
        ob = lax.cond(lw['bound_b'] <= FLASH_PLAIN_MAX_LOG2,
                      functools.partial(flash_b, tq=tp_b[0], tk=tp_b[1], plain=True, group=2),
                      functools.partial(flash_b, tq=t_b, tk=tk_b, plain=False))
        oc = _band(qc, kcb, vcb, _band_bias_t(c_rel_bias[l], t_c, nrel_c), batch=batch, tq=t_c)
        xp = _merge(xp, oa, ob, oc, lw, bm=bm_tok, bn=bn_merge)
        xp = _mlp(xp, lw, bm=bm_tok, bf=bf_mlp)
        for dst, val in zip(outs[:6], (kaf, vaf, ckv, kpe, kcf, vcf)):
            dst.append(val)

        (qa, kaf, kab, vaf, vab, qb, ckv, kpe, qc, kcb, vcb, kcf, vcf) = _in_proj(
            xs, lw, tabs_s, bm=n_s, seg_rows=n_s, keep=n_s, transposed=False)
        kb, vb = _mla_kv(ckv, kpe, lw, bm=n_s)
        kb_c, vb_c = _mla_kv(cb_ckv, cb_kpe, lw, bm=bm_cache, n=streams * past, row0=l * streams * past)
        oa = _decode(qa, ca_k, ca_v, kab, vab, heads=A_HEADS, dk=LANE, rows=dec, stream0=l * streams,
                     diff=True, lam=lw['lam'], sg=lw['sg'])
        ob = _decode(qb, kb_c.reshape(streams, past, B_HEADS * B_PAD), vb_c.reshape(streams, past, BV_W),
                     kb, vb, heads=B_HEADS, dk=B_PAD, rows=dec)
        bias_s = c_rel_bias[l][:, rel_s]
        oc = _decode(qc, cc_k, cc_v, kcb, vcb, heads=C_HEADS, dk=LANE, rows=dec, stream0=l * streams,
                     bias=(bias_s[:, :, :c_past], bias_s[:, :, c_past:], c_mask[:, :c_past], c_mask[:, c_past:]))
        xs = _merge(xs, oa, ob, oc, lw, bm=n_s, bn=bn_merge)
        xs = _mlp(xs, lw, bm=n_s, bf=bf_mlp)
        for dst, val in zip(outs[6:], (kaf, vaf, ckv, kpe, kcf, vcf)):
            dst.append(val)

    st = [jnp.stack(o) for o in outs]
    return (xp.reshape(batch, seq, d_model), xs.reshape(streams, dec, d_model),
            st[0].reshape(depth, batch, seq, A_HEADS, 2 * A_QK_DIM),
            st[1].reshape(depth, batch, seq, A_HEADS, A_V_DIM),
            st[2].reshape(depth, batch, seq, B_KV_LORA),
            st[3].reshape(depth, batch, seq, B_ROPE),
            st[4].reshape(depth, batch, c_keep, C_HEADS, C_DIM),
            st[5].reshape(depth, batch, c_keep, C_HEADS, C_DIM),
            st[6].reshape(depth, streams, dec, A_HEADS, 2 * A_QK_DIM),
            st[7].reshape(depth, streams, dec, A_HEADS, A_V_DIM),
            st[8].reshape(depth, streams, dec, B_KV_LORA),
            st[9].reshape(depth, streams, dec, B_ROPE),
            st[10].reshape(depth, streams, dec, C_HEADS, C_DIM),
            st[11].reshape(depth, streams, dec, C_HEADS, C_DIM))
```

```python
import functools
import math

import jax
import jax.numpy as jnp
from jax import lax
from jax.experimental import pallas as pl
from jax.experimental.pallas import tpu as pltpu

F32 = jnp.float32
BF16 = jnp.bfloat16

CHUNK = 64
ROPE_THETA = 500000.0
EPS = 1e-6
NEG = -1e30
LOG2E = math.log2(math.e)
FLASH_PLAIN_MAX_LOG2 = 50.0
LANE = 128

A_HEADS = 4
A_QK_DIM = 64
A_V_DIM = 128
A_ROT = A_QK_DIM // 4
B_HEADS = 8
B_Q_LORA = 512
B_KV_LORA = 256
B_NOPE = 128
B_ROPE = 64
B_V_DIM = 128
B_QK = B_NOPE + B_ROPE
B_PAD = 2 * LANE
BF16_SUBLANES = 16
VT_ROWS = LANE + BF16_SUBLANES
C_HEADS = 4
C_DIM = 128
C_PAST_CHUNKS = 8
C_WINDOW = C_PAST_CHUNKS * CHUNK
REL_CLIP = 128

A_W = A_HEADS * 2 * A_QK_DIM
AV_W = A_HEADS * A_V_DIM
BV_W = B_HEADS * B_V_DIM
C_W = C_HEADS * C_DIM

OFF_AQ = 0
OFF_AK = OFF_AQ + A_W
OFF_AV = OFF_AK + A_W
OFF_BQ = OFF_AV + AV_W
OFF_CKV = OFF_BQ + B_Q_LORA
OFF_CQ = OFF_CKV + B_KV_LORA
OFF_CK = OFF_CQ + C_W
OFF_CV = OFF_CK + C_W
OFF_KPE = OFF_CV + C_W
WA_COLS = OFF_KPE + LANE

VMEM_LIMIT = 56 * 1024 * 1024


def _cparams(sem):
    return pltpu.CompilerParams(dimension_semantics=sem, vmem_limit_bytes=VMEM_LIMIT)


def _rms(x, width):
    ms = jnp.sum(x * x, axis=-1, keepdims=True) * (1.0 / width)
    return x * lax.rsqrt(ms + EPS)


def _rms_half(zc, lane_lo):
    sq = zc * zc
    lo = jnp.sum(jnp.where(lane_lo, sq, 0.0), axis=-1, keepdims=True)
    hi = jnp.sum(jnp.where(lane_lo, 0.0, sq), axis=-1, keepdims=True)
    ms = jnp.where(lane_lo, lo, hi) * (1.0 / 64.0)
    return zc * lax.rsqrt(ms + EPS)


def _rope(y, c, s1, s2, half):
    return y * c + pltpu.roll(y, LANE - half, 1) * s1 + pltpu.roll(y, half, 1) * s2


def _put_values_t(ref, head, val):
    base = head * VT_ROWS
    ref[base:base + LANE, :] = val.T.astype(BF16)
    ref[base + LANE:base + VT_ROWS, :] = jnp.ones((VT_ROWS - LANE, val.shape[0]), BF16)


def _dot(a, b):
    return jnp.dot(a, b, preferred_element_type=F32)


def _dot_nt(a, b):
    return lax.dot_general(a, b, (((1,), (1,)), ((), ())), preferred_element_type=F32)


def _in_proj_kernel(*refs, transposed, n_alias):
    (x_ref, g_ref, wa_ref, wqb_ref, gaq_ref, gak_ref, gbqa_ref, gbkv_ref, gbqn_ref, gbqr_ref,
     gcq_ref, gck_ref, ca_ref, s1a_ref, s2a_ref, cb_ref, s1b_ref, s2b_ref) = refs[:18]
    (qa_ref, kaf_ref, kab_ref, vaf_ref, vab_ref, qb_ref, ckv_ref, kpe_ref,
     qc_ref, kcb_ref, vcb_ref, kcf_ref, vcf_ref) = refs[18 + n_alias:]
    d_model = x_ref.shape[1]

    def put(ref, c, val):
        if transposed:
            ref[c * LANE:(c + 1) * LANE, :] = val.T.astype(BF16)
        else:
            ref[:, c * LANE:(c + 1) * LANE] = val.astype(BF16)

    h = (_rms(x_ref[...], d_model) * g_ref[...]).astype(BF16)
    lane_lo = lax.broadcasted_iota(jnp.int32, (1, LANE), 1) < 64
    ca, s1a, s2a = ca_ref[...], s1a_ref[...], s2a_ref[...]
    cb, s1b, s2b = cb_ref[...], s1b_ref[...], s2b_ref[...]

    def proj(off, width):
        return _dot(h, wa_ref[:, off:off + width])

    def chunk(c):
        return slice(c * LANE, (c + 1) * LANE)

    z = proj(OFF_AQ, A_W)
    for c in range(A_W // LANE):
        y = _rms_half(z[:, chunk(c)], lane_lo) * gaq_ref[:, chunk(c)]
        put(qa_ref, c, _rope(y, ca, s1a, s2a, A_ROT // 2))
    z = proj(OFF_AK, A_W)
    for c in range(A_W // LANE):
        y = _rms_half(z[:, chunk(c)], lane_lo) * gak_ref[:, chunk(c)]
        y = _rope(y, ca, s1a, s2a, A_ROT // 2)
        kaf_ref[:, chunk(c)] = y
        kab_ref[:, chunk(c)] = y.astype(BF16)
    z = proj(OFF_AV, AV_W)
    vaf_ref[...] = z
    for c in range(A_HEADS):
        if transposed:
            _put_values_t(vab_ref, c, z[:, chunk(c)])
        else:
            put(vab_ref, c, z[:, chunk(c)])

    z = proj(OFF_BQ, B_Q_LORA)
    cq = (_rms(z, B_Q_LORA) * gbqa_ref[...]).astype(BF16)
    qraw = _dot(cq, wqb_ref[...])
    nope_w = B_HEADS * B_NOPE
    roped = [_rope(qraw[:, nope_w + c * LANE:nope_w + (c + 1) * LANE], cb, s1b, s2b, B_ROPE // 2)
             for c in range(B_HEADS * B_ROPE // LANE)]
    for hh in range(B_HEADS):
        nope = qraw[:, hh * B_NOPE:(hh + 1) * B_NOPE]
        in_half = lane_lo if hh % 2 == 0 else jnp.logical_not(lane_lo)
        rsel = jnp.where(in_half, roped[hh // 2], 0.0)
        ss = jnp.sum(nope * nope, axis=-1, keepdims=True) + jnp.sum(rsel * rsel, axis=-1, keepdims=True)
        inv = lax.rsqrt(ss * (1.0 / B_QK) + EPS)
        put(qb_ref, 2 * hh, nope * inv * gbqn_ref[...])
        put(qb_ref, 2 * hh + 1, rsel * inv * gbqr_ref[...])
    z = proj(OFF_CKV, B_KV_LORA)
    ckv_ref[...] = _rms(z, B_KV_LORA) * gbkv_ref[...]
    z = proj(OFF_KPE, LANE)
    kpe_ref[...] = _rope(z, cb, s1b, s2b, B_ROPE // 2)[:, :B_ROPE]

    z = proj(OFF_CQ, C_W)
    for c in range(C_HEADS):
        put(qc_ref, c, _rms(z[:, chunk(c)], C_DIM) * gcq_ref[:, chunk(c)])
    z = proj(OFF_CK, C_W)
    for c in range(C_HEADS):
        y = _rms(z[:, chunk(c)], C_DIM) * gck_ref[:, chunk(c)]
        kcf_ref[:, chunk(c)] = y
        kcb_ref[:, chunk(c)] = y.astype(BF16)
    z = proj(OFF_CV, C_W)
    vcf_ref[...] = z
    for c in range(C_HEADS):
        if transposed:
            _put_values_t(vcb_ref, c, z[:, chunk(c)])
        else:
            put(vcb_ref, c, z[:, chunk(c)])


STACKED_OUTS = (1, 3, 6, 7)


def _in_proj(x, lw, tabs, *, bm, seg_rows, keep, transposed, stack=None):
    n, d_model = x.shape
    nb = n // bm
    nbs = seg_rows // bm
    nbk = keep // bm
    ntab = tabs[0].shape[0] // bm

    def row(i):
        return (i, 0)

    def const(i):
        return (0, 0)

    def tab(i):
        return (i % ntab, 0)

    def kept(i):
        return ((i // nbs) * nbk + jnp.maximum(i % nbs - (nbs - nbk), 0), 0)

    def rows(width):
        return pl.BlockSpec((bm, width), row)

    def full(a):
        return pl.BlockSpec(a.shape, const, pipeline_mode=pl.Buffered(1))

    fl = '_flash' if transposed else ''
    gains = [lw['gaq' + fl], lw['gak'], lw['gbqa'], lw['gbkv'], lw['gbqn' + fl], lw['gbqr' + fl],
             lw['gcq' + fl], lw['gck']]
    in_specs = ([rows(d_model), full(lw['g_attn']), full(lw['w_a']), full(lw['w_qb'])]
                + [full(g) for g in gains]
                + [pl.BlockSpec((bm, LANE), tab) for _ in tabs])
    out_shape = [
        jax.ShapeDtypeStruct((n, A_W), BF16),
        jax.ShapeDtypeStruct((n, A_W), F32),
        jax.ShapeDtypeStruct((n, A_W), BF16),
        jax.ShapeDtypeStruct((n, AV_W), F32),
        jax.ShapeDtypeStruct((n, AV_W), BF16),
        jax.ShapeDtypeStruct((n, B_HEADS * B_PAD), BF16),
        jax.ShapeDtypeStruct((n, B_KV_LORA), F32),
        jax.ShapeDtypeStruct((n, B_ROPE), F32),
        jax.ShapeDtypeStruct((n, C_W), BF16),
        jax.ShapeDtypeStruct((n, C_W), BF16),
        jax.ShapeDtypeStruct((n, C_W), BF16),
        jax.ShapeDtypeStruct((n // seg_rows * keep, C_W), F32),
        jax.ShapeDtypeStruct((n // seg_rows * keep, C_W), F32),
    ]
    out_specs = [rows(s.shape[1]) for s in out_shape[:11]] + [pl.BlockSpec((bm, C_W), kept)] * 2
    if transposed:
        for k, width in ((0, A_W), (4, A_HEADS * VT_ROWS), (5, B_HEADS * B_PAD), (8, C_W),
                         (10, C_HEADS * VT_ROWS)):
            out_shape[k] = jax.ShapeDtypeStruct((width, n), BF16)
            out_specs[k] = pl.BlockSpec((width, bm), lambda i: (0, i))
    args = [x, lw['g_attn'], lw['w_a'], lw['w_qb'], *gains, *tabs]
    aliases = {}
    if stack is not None:
        layer, depth, buffers = stack
        for k in STACKED_OUTS:
            width = out_shape[k].shape[1]
            out_shape[k] = jax.ShapeDtypeStruct((depth * n, width), F32)
            out_specs[k] = pl.BlockSpec((bm, width), lambda i: (layer * nb + i, 0))
        if buffers is not None:
            for pos, k in enumerate(STACKED_OUTS):
                aliases[len(args)] = k
                args.append(buffers[pos])
                in_specs.append(pl.BlockSpec(memory_space=pl.ANY))
    return pl.pallas_call(
        functools.partial(_in_proj_kernel, transposed=transposed, n_alias=len(aliases)),
        grid=(nb,),
        in_specs=in_specs,
        out_specs=out_specs,
        out_shape=out_shape,
        input_output_aliases=aliases,
        compiler_params=_cparams(("arbitrary",)),
        name="in_proj",
    )(*args)


def _mla_kv_kernel(ckv_ref, kpe_ref, w_ref, gn_ref, gr_ref, kb_ref, vb_ref, *, transposed):
    ckv = ckv_ref[...]
    kv = _dot(ckv.astype(BF16), w_ref[...])
    kpe = kpe_ref[...]
    kpe2 = jnp.concatenate([kpe, kpe], axis=1)
    ss_pe = jnp.sum(kpe * kpe, axis=-1, keepdims=True)
    lane_lo = lax.broadcasted_iota(jnp.int32, (1, LANE), 1) < 64
    nope_w = B_HEADS * B_NOPE
    for hh in range(B_HEADS):
        nope = kv[:, hh * B_NOPE:(hh + 1) * B_NOPE]
        ss = jnp.sum(nope * nope, axis=-1, keepdims=True) + ss_pe
        inv = lax.rsqrt(ss * (1.0 / B_QK) + EPS)
        in_half = lane_lo if hh % 2 == 0 else jnp.logical_not(lane_lo)
        kb_ref[:, hh * B_PAD:hh * B_PAD + LANE] = (nope * inv * gn_ref[...]).astype(BF16)
        kb_ref[:, hh * B_PAD + LANE:(hh + 1) * B_PAD] = jnp.where(in_half, kpe2 * inv * gr_ref[...], 0.0).astype(BF16)
    if transposed:
        for c in range(B_HEADS):
            _put_values_t(vb_ref, c, kv[:, nope_w + c * LANE:nope_w + (c + 1) * LANE])
    else:
        vb_ref[...] = kv[:, nope_w:].astype(BF16)


def _mla_kv(ckv, kpe, lw, *, bm, n=None, row0=0, transposed=False):
    n = ckv.shape[0] if n is None else n
    blk0 = row0 // bm
    assert row0 % bm == 0
    if transposed:
        vb_shape, vb_spec = (B_HEADS * VT_ROWS, n), pl.BlockSpec((B_HEADS * VT_ROWS, bm), lambda i: (0, i))
    else:
        vb_shape, vb_spec = (n, BV_W), pl.BlockSpec((bm, BV_W), lambda i: (i, 0))

    def row(i):
        return (i, 0)

    def src_row(i):
        return (blk0 + i, 0)

    def const(i):
        return (0, 0)

    return pl.pallas_call(
        functools.partial(_mla_kv_kernel, transposed=transposed),
        grid=(n // bm,),
        in_specs=[pl.BlockSpec((bm, B_KV_LORA), src_row), pl.BlockSpec((bm, B_ROPE), src_row),
                  pl.BlockSpec(lw['w_kvb'].shape, const), pl.BlockSpec((1, LANE), const),
                  pl.BlockSpec((1, LANE), const)],
        out_specs=[pl.BlockSpec((bm, B_HEADS * B_PAD), row), vb_spec],
        out_shape=[jax.ShapeDtypeStruct((n, B_HEADS * B_PAD), BF16), jax.ShapeDtypeStruct(vb_shape, BF16)],
        compiler_params=_cparams(("arbitrary",)),
        name="mla_kv",
    )(ckv, kpe, lw['w_kvb'], lw['gbkn'], lw['gbkr'])


def _split_components(q):
    lane_lo = lax.broadcasted_iota(jnp.int32, (1, LANE), 1) < 64
    zero = jnp.zeros_like(q)
    return jnp.concatenate([jnp.where(lane_lo, q, zero), jnp.where(lane_lo, zero, q)], axis=0)


def _diff_out(o, rows, lam_ref, sg_ref):
    o = o[:rows] - lam_ref[...] * o[rows:]
    return _rms(o, A_V_DIM) * sg_ref[...]


def _flash_queries(qt_ref, diff):
    qt = qt_ref[...]
    if diff:
        row_lo = lax.broadcasted_iota(jnp.int32, (qt.shape[0], 1), 0) < A_QK_DIM
        zero = jnp.zeros_like(qt)
        qt = jnp.concatenate([jnp.where(row_lo, qt, zero), jnp.where(row_lo, zero, qt)], axis=1)
    return qt


def _flash_finish(acc, o_ref, lam_ref, sg_ref, tq, diff):
    ot = acc[:A_V_DIM] / acc[A_V_DIM:A_V_DIM + 1]
    if diff:
        o = (ot[:, :tq] - lam_ref[:, :1] * ot[:, tq:]).T
        o = _rms(o, A_V_DIM) * sg_ref[...]
    else:
        o = ot.T
    o_ref[...] = o.astype(o_ref.dtype)


def _flash_plain_kernel(*refs, tq, tk, diff, group):
    if diff:
        qt_ref, k_ref, vt_ref, mask_ref, lam_ref, sg_ref, o_ref, acc_sc = refs
    else:
        qt_ref, k_ref, vt_ref, mask_ref, o_ref, acc_sc = refs
        lam_ref = sg_ref = None
    i = pl.program_id(2)
    qt = _flash_queries(qt_ref, diff)
    per_q = tq // tk

    def block(j, diag):
        start = pl.multiple_of(j * tk, tk)
        pt = jnp.exp2(_dot(k_ref[pl.ds(start, tk), :], qt))
        if diag is not None:
            pt = jnp.where(mask_ref[diag] > 0, pt, 0.0)
        return _dot(vt_ref[:, pl.ds(start, tk)], pt.astype(BF16))

    acc_sc[...] = jnp.zeros(acc_sc.shape, F32)

    def add_blocks(j0, nblocks):
        acc = acc_sc[...]
        for jj in range(nblocks):
            acc = acc + block(j0 + jj, None)
        acc_sc[...] = acc

    def trips(jb, carry):
        add_blocks(jb * (group * per_q), group * per_q)
        return carry

    def leftover(r, carry):
        add_blocks((i - 1) * per_q, per_q)
        return carry

    lax.fori_loop(0, i // group, trips, 0)
    if group == 2:
        lax.fori_loop(0, i % group, leftover, 0)
    else:
        assert group == 1
    acc = acc_sc[...]
    for jj in range(per_q):
        acc = acc + block(i * per_q + jj, jj)
    _flash_finish(acc, o_ref, lam_ref, sg_ref, tq, diff)


def _flash_kernel(*refs, tq, tk, diff):
    if diff:
        qt_ref, k_ref, vt_ref, mask_ref, lam_ref, sg_ref, o_ref, m_sc, a_sc, acc_sc, s_sc, p_sc = refs
    else:
        qt_ref, k_ref, vt_ref, mask_ref, o_ref, m_sc, a_sc, acc_sc, s_sc, p_sc = refs
        lam_ref = sg_ref = None
    i = pl.program_id(2)
    qt = _flash_queries(qt_ref, diff)
    per_q = tq // tk

    def scores(j):
        start = pl.multiple_of(j * tk, tk)
        return _dot(k_ref[pl.ds(start, tk), :], qt)

    def softmax(st, m_prev, diag):
        if diag is not None:
            st = jnp.where(mask_ref[diag] > 0, st, NEG)
        m_new = jnp.maximum(m_prev, jnp.max(st, axis=0, keepdims=True))
        return m_new, jnp.exp2(m_prev - m_new), jnp.exp2(st - m_new).astype(BF16)

    def weighted_values(j, acc, alpha, pt):
        start = pl.multiple_of(j * tk, tk)
        return alpha * acc + _dot(vt_ref[:, pl.ds(start, tk)], pt)

    m_sc[...] = jnp.full(m_sc.shape, NEG, F32)
    a_sc[...] = jnp.ones(a_sc.shape, F32)
    acc_sc[...] = jnp.zeros(acc_sc.shape, F32)
    p_sc[...] = jnp.zeros(p_sc.shape, BF16)
    s_sc[...] = scores(0)

    def body(jb, carry):
        m, alpha_prev, acc, p_prev, st = m_sc[...], a_sc[...], acc_sc[...], p_sc[...], s_sc[...]
        for jj in range(per_q):
            j = jb * per_q + jj
            st_next = scores(j + 1)
            m, alpha, pt = softmax(st, m, None)
            acc = weighted_values(jnp.maximum(j - 1, 0), acc, alpha_prev, p_prev)
            alpha_prev, p_prev, st = alpha, pt, st_next
        m_sc[...], a_sc[...], acc_sc[...], p_sc[...], s_sc[...] = m, alpha_prev, acc, p_prev, st
        return carry

    lax.fori_loop(0, i, body, 0)
    m, alpha_prev, acc, p_prev, st = m_sc[...], a_sc[...], acc_sc[...], p_sc[...], s_sc[...]
    for jj in range(per_q):
        j = i * per_q + jj
        st_next = scores(j + 1) if jj + 1 < per_q else None
        m, alpha, pt = softmax(st, m, jj)
        acc = weighted_values(jnp.maximum(j - 1, 0), acc, alpha_prev, p_prev)
        alpha_prev, p_prev, st = alpha, pt, st_next
    acc = weighted_values(i * per_q + per_q - 1, acc, alpha_prev, p_prev)
    _flash_finish(acc, o_ref, lam_ref, sg_ref, tq, diff)


def _flash(qt, k, vt, *, batch, heads, dk, tq, tk, diff, plain, group=1, lam=None, sg=None):
    n = k.shape[0]
    seq = n // batch
    nq = seq // tq
    mq = 2 * tq if diff else tq
    per_q = tq // tk
    kpos = jnp.arange(tq, dtype=jnp.int32).reshape(per_q, tk, 1)
    qpos = jnp.arange(mq, dtype=jnp.int32).reshape(1, 1, mq) % tq
    mask = ((kpos // CHUNK) <= (qpos // CHUNK)).astype(F32)
    in_specs = [pl.BlockSpec((dk, tq), lambda b, h, i: (h, b * nq + i)),
                pl.BlockSpec((seq, dk), lambda b, h, i: (b, h)),
                pl.BlockSpec((VT_ROWS, seq), lambda b, h, i: (h, b)),
                pl.BlockSpec((per_q, tk, mq), lambda b, h, i: (0, 0, 0), pipeline_mode=pl.Buffered(1))]
    args = [qt, k, vt, mask]
    if diff:
        in_specs += [pl.BlockSpec((1, LANE), lambda b, h, i: (0, 0))] * 2
        args += [lam, sg]
    scratch = [pltpu.VMEM((VT_ROWS, mq), F32)]
    if not plain:
        scratch = [pltpu.VMEM((1, mq), F32), pltpu.VMEM((1, mq), F32)] + scratch + [
            pltpu.VMEM((tk, mq), F32), pltpu.VMEM((tk, mq), BF16)]
    return pl.pallas_call(
        (functools.partial(_flash_plain_kernel, tq=tq, tk=tk, diff=diff, group=group) if plain
         else functools.partial(_flash_kernel, tq=tq, tk=tk, diff=diff)),
        grid=(batch, heads, nq),
        in_specs=in_specs,
        out_specs=pl.BlockSpec((tq, LANE), lambda b, h, i: (b * nq + i, h)),
        out_shape=jax.ShapeDtypeStruct((n, heads * LANE), BF16),
        scratch_shapes=scratch,
        compiler_params=_cparams(("parallel", "parallel", "arbitrary")),
        name=("flash_diff" if diff else "flash_mla") + ("_plain" if plain else ""),
    )(*args)


def _band_kernel(qt_ref, k_ref, vt_ref, bias_ref, o_ref, *, tq, nrel):
    i = pl.program_id(2)
    qt = qt_ref[...]

    def attend(rels):
        def keys(rel):
            return pl.ds(pl.multiple_of((i - rel) * tq, tq), tq)

        st = [_dot(k_ref[keys(rel), :], qt) + bias_ref[0, rel] for rel in rels]
        m = functools.reduce(jnp.maximum, [jnp.max(s, axis=0, keepdims=True) for s in st])
        acc = sum(_dot(vt_ref[:, keys(rel)], jnp.exp2(s - m).astype(BF16)) for rel, s in zip(rels, st))
        o_ref[...] = (acc[:C_DIM] / acc[C_DIM:C_DIM + 1]).T.astype(o_ref.dtype)

    for avail in range(nrel):
        cond = (i == avail) if avail < nrel - 1 else (i >= avail)
        pl.when(cond)(functools.partial(attend, tuple(range(avail + 1))))


def _band_bias_t(table, tq, nrel):
    bias = jnp.swapaxes(_band_bias(table, tq, nrel), 2, 3) * LOG2E
    k_ch = jnp.arange(tq).reshape(1, tq, 1) // CHUNK
    q_ch = (jnp.arange(nrel).reshape(nrel, 1, 1) * tq + jnp.arange(tq).reshape(1, 1, tq)) // CHUNK
    visible = (k_ch <= q_ch) & (k_ch >= q_ch - C_PAST_CHUNKS)
    return jnp.where(visible[None], bias, NEG)


def _band_bias(table, tq, nrel):
    heads = table.shape[0]
    c_all = nrel * tq
    n = tq + c_all - 1
    n_lo = tq - 1 - REL_CLIP
    n_hi = n - n_lo - (2 * REL_CLIP + 1)
    assert n_hi >= 0
    u = jnp.concatenate([jnp.broadcast_to(table[:, :1], (heads, max(n_lo, 0))), table[:, max(-n_lo, 0):],
                         jnp.broadcast_to(table[:, -1:], (heads, n_hi))], axis=1)
    w = jnp.concatenate([u[:, ::-1], jnp.zeros((heads, 1), table.dtype)], axis=1)
    m = jnp.tile(w, (1, tq))[:, :tq * n].reshape(heads, tq, n)
    t = m[:, :, tq - 1:tq - 1 + c_all]
    return t.reshape(heads, tq, nrel, tq).transpose(0, 2, 1, 3)[:, ::-1]


def _band(qt, k, vt, bias, *, batch, tq):
    n = k.shape[0]
    seq = n // batch
    nq = seq // tq
    nrel = bias.shape[1]
    return pl.pallas_call(
        functools.partial(_band_kernel, tq=tq, nrel=nrel),
        grid=(batch, C_HEADS, nq),
        in_specs=[pl.BlockSpec((LANE, tq), lambda b, h, i: (h, b * nq + i)),
                  pl.BlockSpec((seq, LANE), lambda b, h, i: (b, h)),
                  pl.BlockSpec((VT_ROWS, seq), lambda b, h, i: (h, b)),
                  pl.BlockSpec((1, nrel, tq, tq), lambda b, h, i: (h, 0, 0, 0))],
        out_specs=pl.BlockSpec((tq, LANE), lambda b, h, i: (b * nq + i, h)),
        out_shape=jax.ShapeDtypeStruct((n, C_W), BF16),
        compiler_params=_cparams(("parallel", "parallel", "arbitrary")),
        name="band",
    )(qt, k, vt, bias)


def _decode_kernel(*refs, heads, dk, diff, biased):
    refs = list(refs)
    q_ref, kc_ref, vc_ref, kn_ref, vn_ref = refs[:5]
    rest = refs[5:]
    if biased:
        bc_ref, bn_ref, mc_ref, mn_ref = rest[:4]
        rest = rest[4:]
    if diff:
        lam_ref, sg_ref = rest[:2]
        rest = rest[2:]
    o_ref = rest[0]
    rows = q_ref.shape[0]
    for h in range(heads):
        qk = slice(h * dk, (h + 1) * dk)
        vv = slice(h * LANE, (h + 1) * LANE)
        q = q_ref[:, qk]
        if diff:
            q = _split_components(q)
        s_c = _dot_nt(q, kc_ref[0, :, qk].astype(BF16))
        s_n = _dot_nt(q, kn_ref[:, qk])
        if biased:
            s_c = jnp.where(mc_ref[...] > 0, s_c + bc_ref[h], NEG)
            s_n = jnp.where(mn_ref[...] > 0, s_n + bn_ref[h], NEG)
        m = jnp.maximum(jnp.max(s_c, axis=-1, keepdims=True), jnp.max(s_n, axis=-1, keepdims=True))
        p_c = jnp.exp(s_c - m)
        p_n = jnp.exp(s_n - m)
        l = jnp.sum(p_c, axis=-1, keepdims=True) + jnp.sum(p_n, axis=-1, keepdims=True)
        o = (_dot(p_c.astype(BF16), vc_ref[0, :, vv].astype(BF16)) + _dot(p_n.astype(BF16), vn_ref[:, vv])) / l
        if diff:
            o = _diff_out(o, rows, lam_ref, sg_ref)
        o_ref[:, vv] = o.astype(o_ref.dtype)


def _decode(q, kc, vc, kn, vn, *, heads, dk, rows, stream0=0, diff=False, lam=None, sg=None, bias=None):
    n = q.shape[0]
    streams, past = n // rows, kc.shape[1]
    in_specs = [pl.BlockSpec((rows, heads * dk), lambda b: (b, 0)),
                pl.BlockSpec((1, past, heads * dk), lambda b: (stream0 + b, 0, 0)),
                pl.BlockSpec((1, past, heads * LANE), lambda b: (stream0 + b, 0, 0)),
                pl.BlockSpec((rows, heads * dk), lambda b: (b, 0)),
                pl.BlockSpec((rows, heads * LANE), lambda b: (b, 0))]
    args = [q, kc, vc, kn, vn]
    if bias is not None:
        bias_c, bias_n, mask_c, mask_n = bias
        in_specs += [pl.BlockSpec((heads, rows, past), lambda b: (0, 0, 0)),
                     pl.BlockSpec((heads, rows, rows), lambda b: (0, 0, 0)),
                     pl.BlockSpec((rows, past), lambda b: (0, 0)),
                     pl.BlockSpec((rows, rows), lambda b: (0, 0))]
        args += [bias_c, bias_n, mask_c, mask_n]
    if diff:
        in_specs += [pl.BlockSpec((1, LANE), lambda b: (0, 0))] * 2
        args += [lam, sg]
    return pl.pallas_call(
        functools.partial(_decode_kernel, heads=heads, dk=dk, diff=diff, biased=bias is not None),
        grid=(streams,),
        in_specs=in_specs,
        out_specs=pl.BlockSpec((rows, heads * LANE), lambda b: (b, 0)),
        out_shape=jax.ShapeDtypeStruct((n, heads * LANE), BF16),
        compiler_params=_cparams(("parallel",)),
        name="decode",
    )(*args)


def _merge_kernel(x_ref, g_ref, oa_ref, ob_ref, oc_ref, wga_ref, wgb_ref, wgc_ref, bga_ref, bgb_ref, bgc_ref,
                  wba_ref, wbb_ref, wbc_ref, wo_ref, out_ref, h_sc):
    j = pl.program_id(1)

    @pl.when(j == 0)
    def _():
        x = x_ref[...]
        h_sc[...] = (_rms(x, x.shape[1]) * g_ref[...]).astype(BF16)
        out_ref[...] = x

    h = h_sc[...]

    def branch(o_ref, wg_ref, bg_ref, wb_ref):
        gate = jax.nn.sigmoid(_dot(h, wg_ref[...]) + bg_ref[...])
        return gate * _dot(o_ref[...], wb_ref[...])

    merged = (branch(oa_ref, wga_ref, bga_ref, wba_ref) + branch(ob_ref, wgb_ref, bgb_ref, wbb_ref)
              + branch(oc_ref, wgc_ref, bgc_ref, wbc_ref))
    out_ref[...] += _dot(merged.astype(BF16), wo_ref[...])


def _merge(x, oa, ob, oc, lw, *, bm, bn):
    n, d_model = x.shape
    nbn = d_model // bn

    def row(i, j):
        return (i, 0)

    def const(i, j):
        return (0, 0)

    def col(k):
        return lambda i, j: (0, k * nbn + j)

    in_specs = [pl.BlockSpec((bm, d_model), row), pl.BlockSpec((1, d_model), const),
                pl.BlockSpec((bm, AV_W), row), pl.BlockSpec((bm, BV_W), row), pl.BlockSpec((bm, C_W), row),
                pl.BlockSpec((d_model, bn), col(0)), pl.BlockSpec((d_model, bn), col(1)),
                pl.BlockSpec((d_model, bn), col(2)),
                pl.BlockSpec((1, bn), col(0)), pl.BlockSpec((1, bn), col(1)), pl.BlockSpec((1, bn), col(2)),
                pl.BlockSpec((AV_W, bn), col(0)), pl.BlockSpec((BV_W, bn), col(0)), pl.BlockSpec((C_W, bn), col(0)),
                pl.BlockSpec((bn, d_model), lambda i, j: (j, 0))]
    return pl.pallas_call(
        _merge_kernel,
        grid=(n // bm, nbn),
        in_specs=in_specs,
        out_specs=pl.BlockSpec((bm, d_model), row),
        out_shape=jax.ShapeDtypeStruct((n, d_model), F32),
        scratch_shapes=[pltpu.VMEM((bm, d_model), BF16)],
        compiler_params=_cparams(("parallel", "arbitrary")),
        name="merge",
    )(x, lw['g_attn'], oa, ob, oc, lw['w_g'], lw['w_g'], lw['w_g'], lw['b_g'], lw['b_g'], lw['b_g'],
      lw['w_br_a'], lw['w_br_b'], lw['w_br_c'], lw['w_o'])


def _mlp_kernel(x_ref, g_ref, wu_ref, wd_ref, out_ref, h_sc):
    j = pl.program_id(1)

    @pl.when(j == 0)
    def _():
        x = x_ref[...]
        h_sc[...] = (_rms(x, x.shape[1]) * g_ref[...]).astype(BF16)
        out_ref[...] = x

    u = jnp.maximum(_dot(h_sc[...], wu_ref[...]), 0.0)
    out_ref[...] += _dot((u * u).astype(BF16), wd_ref[...])


def _mlp(x, lw, *, bm, bf):
    n, d_model = x.shape
    d_ff = lw['w_up'].shape[1]
    return pl.pallas_call(
        _mlp_kernel,
        grid=(n // bm, d_ff // bf),
        in_specs=[pl.BlockSpec((bm, d_model), lambda i, j: (i, 0)), pl.BlockSpec((1, d_model), lambda i, j: (0, 0)),
                  pl.BlockSpec((d_model, bf), lambda i, j: (0, j)), pl.BlockSpec((bf, d_model), lambda i, j: (j, 0))],
        out_specs=pl.BlockSpec((bm, d_model), lambda i, j: (i, 0)),
        out_shape=jax.ShapeDtypeStruct((n, d_model), F32),
        scratch_shapes=[pltpu.VMEM((bm, d_model), BF16)],
        compiler_params=_cparams(("parallel", "arbitrary")),
        name="mlp",
    )(x, lw['g_mlp'], lw['w_up'], lw['w_down'])


def _rope_tables(pos, rot):
    half = rot // 2
    inv = jnp.power(ROPE_THETA, -jnp.arange(half, dtype=F32) / half)
    ang = pos.astype(F32)[:, None] * inv[None, :]
    cos, sin = jnp.cos(ang), jnp.sin(ang)
    npos = pos.shape[0]
    one = jnp.ones((npos, 64 - rot), F32)
    c64 = jnp.concatenate([cos, cos, one], axis=1)
    s1 = jnp.concatenate([-sin, jnp.zeros((npos, 64 - half), F32)], axis=1)
    s2 = jnp.concatenate([jnp.zeros((npos, half), F32), sin, jnp.zeros((npos, 64 - rot), F32)], axis=1)
    return [jnp.tile(t, (1, 2)) for t in (c64, s1, s2)]


def _score_bound(gq, gk, d):
    return math.sqrt(d) * LOG2E * jnp.max(jnp.abs(gq)) * jnp.max(jnp.abs(gk))


def _layer_weights(l, p):
    d_model = p['w_in'].shape[1]
    w_in = p['w_in'][l]
    o = [0]
    for width in (A_W, A_W, AV_W, B_Q_LORA, B_KV_LORA + B_ROPE, C_W, C_W, C_W):
        o.append(o[-1] + width)
    cols = lambda a, b: w_in[:, a:b]
    w_a = jnp.concatenate([
        cols(o[0], o[1]), cols(o[1], o[2]), cols(o[2], o[3]), cols(o[3], o[4]),
        cols(o[4], o[4] + B_KV_LORA), cols(o[5], o[6]), cols(o[6], o[7]), cols(o[7], o[8]),
        cols(o[4] + B_KV_LORA, o[5]), jnp.zeros((d_model, LANE - B_ROPE), F32)], axis=1).astype(BF16)
    w_qb = p['w_qb'][l].reshape(B_Q_LORA, B_HEADS, B_QK)
    w_qb = jnp.concatenate([w_qb[:, :, :B_NOPE].reshape(B_Q_LORA, -1),
                            w_qb[:, :, B_NOPE:].reshape(B_Q_LORA, -1)], axis=1).astype(BF16)
    w_kvb = p['w_kvb'][l].reshape(B_KV_LORA, B_HEADS, B_NOPE + B_V_DIM)
    w_kvb = jnp.concatenate([w_kvb[:, :, :B_NOPE].reshape(B_KV_LORA, -1),
                             w_kvb[:, :, B_NOPE:].reshape(B_KV_LORA, -1)], axis=1).astype(BF16)
    lam_init = 0.8 - 0.6 * math.exp(-0.3 * l)
    lf = p['a_lambda'][l].astype(F32)
    lam = jnp.exp(jnp.sum(lf[0] * lf[1])) - jnp.exp(jnp.sum(lf[2] * lf[3])) + lam_init

    def tiled(g, reps, scale=1.0):
        return jnp.tile(g * scale, reps)[None, :]

    bqn = p['b_q_norm'][l]
    bkn = p['b_k_norm'][l]
    return {
        'g_attn': p['attn_norm'][l][None, :],
        'g_mlp': p['mlp_norm'][l][None, :],
        'w_a': w_a,
        'w_g': w_in[:, o[8]:].astype(BF16),
        'b_g': p['b_gate'][l][None, :],
        'w_qb': w_qb,
        'w_kvb': w_kvb,
        'gaq': tiled(p['a_q_norm'][l], 2 * A_HEADS, A_QK_DIM ** -0.5),
        'gak': tiled(p['a_k_norm'][l], 2 * A_HEADS),
        'gbqa': p['b_qa_norm'][l][None, :],
        'gbkv': p['b_kv_norm'][l][None, :],
        'gbqn': tiled(bqn[:B_NOPE], 1, B_QK ** -0.5),
        'gbqr': tiled(bqn[B_NOPE:], 2, B_QK ** -0.5),
        'gaq_flash': tiled(p['a_q_norm'][l], 2 * A_HEADS, A_QK_DIM ** -0.5 * LOG2E),
        'gbqn_flash': tiled(bqn[:B_NOPE], 1, B_QK ** -0.5 * LOG2E),
        'gbqr_flash': tiled(bqn[B_NOPE:], 2, B_QK ** -0.5 * LOG2E),
        'gcq_flash': tiled(p['c_q_norm'][l], C_HEADS, C_DIM ** -0.5 * LOG2E),
        'gbkn': tiled(bkn[:B_NOPE], 1),
        'gbkr': tiled(bkn[B_NOPE:], 2),
        'gcq': tiled(p['c_q_norm'][l], C_HEADS, C_DIM ** -0.5),
        'gck': tiled(p['c_k_norm'][l], C_HEADS),
        'lam': jnp.broadcast_to(lam, (1, LANE)).astype(F32),
        'sg': (p['a_sub_norm'][l] * (1.0 - lam_init))[None, :],
        'bound_a': _score_bound(p['a_q_norm'][l], p['a_k_norm'][l], A_QK_DIM),
        'bound_b': _score_bound(bqn, bkn, B_QK),
        'w_br_a': p['w_br_a'][l].astype(BF16),
        'w_br_b': p['w_br_b'][l].astype(BF16),
        'w_br_c': p['w_br_c'][l].astype(BF16),
        'w_o': p['w_o'][l].astype(BF16),
        'w_up': p['w_up'][l].astype(BF16),
        'w_down': p['w_down'][l].astype(BF16),
    }


def _pick(n, prefs):
    for b in prefs:
        if n % b == 0:
            return b
    raise ValueError(f"no block size in {prefs} divides {n}")


def kernel(x_prompt, x_sample, cache_a_k, cache_a_v, cache_b_ckv, cache_b_kpe, cache_c_k, cache_c_v,
           attn_norm, w_in, b_gate, a_q_norm, a_k_norm, a_lambda, a_sub_norm,
           b_qa_norm, b_kv_norm, w_qb, w_kvb, b_q_norm, b_k_norm,
           c_q_norm, c_k_norm, c_rel_bias, w_br_a, w_br_b, w_br_c, w_o,
           mlp_norm, w_up, w_down):
    p = dict(attn_norm=attn_norm, w_in=w_in, b_gate=b_gate, a_q_norm=a_q_norm, a_k_norm=a_k_norm,
             a_lambda=a_lambda, a_sub_norm=a_sub_norm, b_qa_norm=b_qa_norm, b_kv_norm=b_kv_norm,
             w_qb=w_qb, w_kvb=w_kvb, b_q_norm=b_q_norm, b_k_norm=b_k_norm, c_q_norm=c_q_norm,
             c_k_norm=c_k_norm, w_br_a=w_br_a, w_br_b=w_br_b, w_br_c=w_br_c, w_o=w_o, mlp_norm=mlp_norm,
             w_up=w_up, w_down=w_down)
    depth = w_in.shape[0]
    batch, seq, d_model = x_prompt.shape
    streams, dec, _ = x_sample.shape
    past = cache_a_k.shape[2]
    c_past = cache_c_k.shape[2]
    c_keep = min(C_WINDOW, seq)
    n_p, n_s = batch * seq, streams * dec
    assert seq % CHUNK == 0 and c_keep % CHUNK == 0

    pos_p = jnp.arange(seq)
    pos_s = past + jnp.arange(dec)
    tabs_p = _rope_tables(pos_p, A_ROT) + _rope_tables(pos_p, B_ROPE)
    tabs_s = [jnp.tile(t, (streams, 1)) for t in _rope_tables(pos_s, A_ROT) + _rope_tables(pos_s, B_ROPE)]

    kc_pos = jnp.concatenate([past - c_past + jnp.arange(c_past), pos_s])
    q_ch, k_ch = pos_s // CHUNK, kc_pos // CHUNK
    c_mask = ((k_ch[None, :] <= q_ch[:, None]) & (k_ch[None, :] >= q_ch[:, None] - C_PAST_CHUNKS)).astype(F32)
    rel_s = jnp.clip(pos_s[:, None] - kc_pos[None, :], -REL_CLIP, REL_CLIP) + REL_CLIP

    bm_in = _pick(seq, (256, 128, 64))
    bm_tok = _pick(seq, (512, 256, 128, 64))
    t_a = _pick(seq, (512, 256, 128))
    t_b = _pick(seq, (1024, 512, 256, 128))
    tk_a, tk_b = t_a, t_b // 2
    tp_a = (t_b, t_b // 2)
    tp_b = (t_b, t_b // 2)
    t_c = _pick(seq, (512, 256, 128))
    nrel_c = C_WINDOW // t_c + 1
    bm_wide = _pick(seq, (1024, 512, 256, 128, 64))
    bn_merge = 512
    bf_mlp = 512
    bm_cache = _pick(streams * past, (512, 256, 128, 64, 16))

    ca_k = cache_a_k.reshape(depth * streams, past, A_W)
    ca_v = cache_a_v.reshape(depth * streams, past, AV_W)
    cb_ckv = cache_b_ckv.reshape(depth * streams * past, B_KV_LORA)
    cb_kpe = cache_b_kpe.reshape(depth * streams * past, B_ROPE)
    cc_k = cache_c_k.reshape(depth * streams, c_past, C_W)
    cc_v = cache_c_v.reshape(depth * streams, c_past, C_W)

    xp = x_prompt.reshape(n_p, d_model)
    xs = x_sample.reshape(n_s, d_model)
    outs = [[] for _ in range(12)]
    stacked = None
    for l in range(depth):
        lw = _layer_weights(l, p)

        (qa, kaf, kab, vaf, vab, qb, ckv, kpe, qc, kcb, vcb, kcf, vcf) = _in_proj(
            xp, lw, tabs_p, bm=bm_in, seg_rows=seq, keep=c_keep, transposed=True, stack=(l, depth, stacked))
        stacked = (kaf, vaf, ckv, kpe)
        kb, vb = _mla_kv(ckv, kpe, lw, bm=bm_tok, n=n_p, row0=l * n_p, transposed=True)
        flash_a = functools.partial(_flash, qa, kab, vab, batch=batch, heads=A_HEADS, dk=LANE, diff=True,
                                    lam=lw['lam'], sg=lw['sg'])
        flash_b = functools.partial(_flash, qb, kb, vb, batch=batch, heads=B_HEADS, dk=B_PAD, diff=False)
        oa = lax.cond(lw['bound_a'] <= FLASH_PLAIN_MAX_LOG2,
                      functools.partial(flash_a, tq=tp_a[0], tk=tp_a[1], plain=True),
                      functools.partial(flash_a, tq=t_a, tk=tk_a, plain=False))
        ob = lax.cond(lw['bound_b'] <= FLASH_PLAIN_MAX_LOG2,
                      functools.partial(flash_b, tq=tp_b[0], tk=tp_b[1], plain=True, group=2),
                      functools.partial(flash_b, tq=t_b, tk=tk_b, plain=False))
        oc = _band(qc, kcb, vcb, _band_bias_t(c_rel_bias[l], t_c, nrel_c), batch=batch, tq=t_c)
        xp = _merge(xp, oa, ob, oc, lw, bm=bm_tok, bn=bn_merge)
        xp = _mlp(xp, lw, bm=bm_wide, bf=bf_mlp)
        for dst, val in zip(outs[4:6], (kcf, vcf)):
            dst.append(val)

        (qa, kaf, kab, vaf, vab, qb, ckv, kpe, qc, kcb, vcb, kcf, vcf) = _in_proj(
            xs, lw, tabs_s, bm=n_s, seg_rows=n_s, keep=n_s, transposed=False)
        kb, vb = _mla_kv(ckv, kpe, lw, bm=n_s)
        kb_c, vb_c = _mla_kv(cb_ckv, cb_kpe, lw, bm=bm_cache, n=streams * past, row0=l * streams * past)
        oa = _decode(qa, ca_k, ca_v, kab, vab, heads=A_HEADS, dk=LANE, rows=dec, stream0=l * streams,
                     diff=True, lam=lw['lam'], sg=lw['sg'])
        ob = _decode(qb, kb_c.reshape(streams, past, B_HEADS * B_PAD), vb_c.reshape(streams, past, BV_W),
                     kb, vb, heads=B_HEADS, dk=B_PAD, rows=dec)
        bias_s = c_rel_bias[l][:, rel_s]
        oc = _decode(qc, cc_k, cc_v, kcb, vcb, heads=C_HEADS, dk=LANE, rows=dec, stream0=l * streams,
                     bias=(bias_s[:, :, :c_past], bias_s[:, :, c_past:], c_mask[:, :c_past], c_mask[:, c_past:]))
        xs = _merge(xs, oa, ob, oc, lw, bm=n_s, bn=bn_merge)
        xs = _mlp(xs, lw, bm=n_s, bf=bf_mlp)
        for dst, val in zip(outs[6:], (kaf, vaf, ckv, kpe, kcf, vcf)):
            dst.append(val)

    st = list(stacked) + [jnp.stack(o) for o in outs[4:]]
    return (xp.reshape(batch, seq, d_model), xs.reshape(streams, dec, d_model),
            st[0].reshape(depth, batch, seq, A_HEADS, 2 * A_QK_DIM),
            st[1].reshape(depth, batch, seq, A_HEADS, A_V_DIM),
            st[2].reshape(depth, batch, seq, B_KV_LORA),
            st[3].reshape(depth, batch, seq, B_ROPE),
            st[4].reshape(depth, batch, c_keep, C_HEADS, C_DIM),
            st[5].reshape(depth, batch, c_keep, C_HEADS, C_DIM),
            st[6].reshape(depth, streams, dec, A_HEADS, 2 * A_QK_DIM),
            st[7].reshape(depth, streams, dec, A_HEADS, A_V_DIM),
            st[8].reshape(depth, streams, dec, B_KV_LORA),
            st[9].reshape(depth, streams, dec, B_ROPE),
            st[10].reshape(depth, streams, dec, C_HEADS, C_DIM),
            st[11].reshape(depth, streams, dec, C_HEADS, C_DIM))
```

```python
import functools
import math

import jax
import jax.numpy as jnp
from jax import lax
from jax.experimental import pallas as pl
from jax.experimental.pallas import tpu as pltpu

F32 = jnp.float32
BF16 = jnp.bfloat16

CHUNK = 64
ROPE_THETA = 500000.0
EPS = 1e-6
NEG = -1e30
LOG2E = math.log2(math.e)
FLASH_PLAIN_MAX_LOG2 = 50.0
LANE = 128

A_HEADS = 4
A_QK_DIM = 64
A_V_DIM = 128
A_ROT = A_QK_DIM // 4
B_HEADS = 8
B_Q_LORA = 512
B_KV_LORA = 256
B_NOPE = 128
B_ROPE = 64
B_V_DIM = 128
B_QK = B_NOPE + B_ROPE
B_PAD = 2 * LANE
BF16_SUBLANES = 16
VT_ROWS = LANE + BF16_SUBLANES
C_HEADS = 4
C_DIM = 128
C_PAST_CHUNKS = 8
C_WINDOW = C_PAST_CHUNKS * CHUNK
REL_CLIP = 128

A_W = A_HEADS * 2 * A_QK_DIM
AV_W = A_HEADS * A_V_DIM
BV_W = B_HEADS * B_V_DIM
C_W = C_HEADS * C_DIM

OFF_AQ = 0
OFF_AK = OFF_AQ + A_W
OFF_AV = OFF_AK + A_W
OFF_BQ = OFF_AV + AV_W
OFF_CKV = OFF_BQ + B_Q_LORA
OFF_CQ = OFF_CKV + B_KV_LORA
OFF_CK = OFF_CQ + C_W
OFF_CV = OFF_CK + C_W
OFF_KPE = OFF_CV + C_W
WA_COLS = OFF_KPE + LANE

VMEM_LIMIT = 56 * 1024 * 1024


def _cparams(sem):
    return pltpu.CompilerParams(dimension_semantics=sem, vmem_limit_bytes=VMEM_LIMIT)


def _rms(x, width):
    ms = jnp.sum(x * x, axis=-1, keepdims=True) * (1.0 / width)
    return x * lax.rsqrt(ms + EPS)


def _rms_half(zc, lane_lo):
    sq = zc * zc
    lo = jnp.sum(jnp.where(lane_lo, sq, 0.0), axis=-1, keepdims=True)
    hi = jnp.sum(jnp.where(lane_lo, 0.0, sq), axis=-1, keepdims=True)
    ms = jnp.where(lane_lo, lo, hi) * (1.0 / 64.0)
    return zc * lax.rsqrt(ms + EPS)


def _rope(y, c, s1, s2, half):
    return y * c + pltpu.roll(y, LANE - half, 1) * s1 + pltpu.roll(y, half, 1) * s2


def _put_values_t(ref, head, val):
    base = head * VT_ROWS
    ref[base:base + LANE, :] = val.T.astype(BF16)
    ref[base + LANE:base + VT_ROWS, :] = jnp.ones((VT_ROWS - LANE, val.shape[0]), BF16)


def _dot(a, b):
    return jnp.dot(a, b, preferred_element_type=F32)


def _dot_nt(a, b):
    return lax.dot_general(a, b, (((1,), (1,)), ((), ())), preferred_element_type=F32)


def _in_proj_kernel(*refs, transposed, n_alias, interleaved):
    (x_ref, g_ref, wa_ref, wqb_ref, gaq_ref, gak_ref, gbqa_ref, gbkv_ref, gbqn_ref, gbqr_ref,
     gcq_ref, gck_ref, ca_ref, s1a_ref, s2a_ref, cb_ref, s1b_ref, s2b_ref) = refs[:18]
    (qa_ref, kaf_ref, kab_ref, vaf_ref, vab_ref, qb_ref, ckv_ref, kpe_ref,
     qc_ref, kcb_ref, vcb_ref, kcf_ref, vcf_ref) = refs[18 + n_alias:]
    d_model = x_ref.shape[1]

    def put(ref, c, val):
        if transposed:
            ref[c * LANE:(c + 1) * LANE, :] = val.T.astype(BF16)
        else:
            ref[:, c * LANE:(c + 1) * LANE] = val.astype(BF16)

    def put_cache(ref, head, val):
        if interleaved:
            ref[pl.ds(head, val.shape[0], stride=A_HEADS), :] = val
        else:
            ref[:, head * LANE:(head + 1) * LANE] = val

    h = (_rms(x_ref[...], d_model) * g_ref[...]).astype(BF16)
    lane_lo = lax.broadcasted_iota(jnp.int32, (1, LANE), 1) < 64
    ca, s1a, s2a = ca_ref[...], s1a_ref[...], s2a_ref[...]
    cb, s1b, s2b = cb_ref[...], s1b_ref[...], s2b_ref[...]

    def proj(off, width):
        return _dot(h, wa_ref[:, off:off + width])

    def chunk(c):
        return slice(c * LANE, (c + 1) * LANE)

    z = proj(OFF_AQ, A_W)
    for c in range(A_W // LANE):
        y = _rms_half(z[:, chunk(c)], lane_lo) * gaq_ref[:, chunk(c)]
        put(qa_ref, c, _rope(y, ca, s1a, s2a, A_ROT // 2))
    z = proj(OFF_AK, A_W)
    for c in range(A_W // LANE):
        y = _rms_half(z[:, chunk(c)], lane_lo) * gak_ref[:, chunk(c)]
        y = _rope(y, ca, s1a, s2a, A_ROT // 2)
        put_cache(kaf_ref, c, y)
        kab_ref[:, chunk(c)] = y.astype(BF16)
    z = proj(OFF_AV, AV_W)
    for c in range(A_HEADS):
        put_cache(vaf_ref, c, z[:, chunk(c)])
        if transposed:
            _put_values_t(vab_ref, c, z[:, chunk(c)])
        else:
            put(vab_ref, c, z[:, chunk(c)])

    z = proj(OFF_BQ, B_Q_LORA)
    cq = (_rms(z, B_Q_LORA) * gbqa_ref[...]).astype(BF16)
    qraw = _dot(cq, wqb_ref[...])
    nope_w = B_HEADS * B_NOPE
    roped = [_rope(qraw[:, nope_w + c * LANE:nope_w + (c + 1) * LANE], cb, s1b, s2b, B_ROPE // 2)
             for c in range(B_HEADS * B_ROPE // LANE)]
    for hh in range(B_HEADS):
        nope = qraw[:, hh * B_NOPE:(hh + 1) * B_NOPE]
        in_half = lane_lo if hh % 2 == 0 else jnp.logical_not(lane_lo)
        rsel = jnp.where(in_half, roped[hh // 2], 0.0)
        ss = jnp.sum(nope * nope, axis=-1, keepdims=True) + jnp.sum(rsel * rsel, axis=-1, keepdims=True)
        inv = lax.rsqrt(ss * (1.0 / B_QK) + EPS)
        put(qb_ref, 2 * hh, nope * inv * gbqn_ref[...])
        put(qb_ref, 2 * hh + 1, rsel * inv * gbqr_ref[...])
    z = proj(OFF_CKV, B_KV_LORA)
    ckv_ref[...] = _rms(z, B_KV_LORA) * gbkv_ref[...]
    z = proj(OFF_KPE, LANE)
    kpe_ref[...] = _rope(z, cb, s1b, s2b, B_ROPE // 2)[:, :B_ROPE]

    z = proj(OFF_CQ, C_W)
    for c in range(C_HEADS):
        put(qc_ref, c, _rms(z[:, chunk(c)], C_DIM) * gcq_ref[:, chunk(c)])
    z = proj(OFF_CK, C_W)
    for c in range(C_HEADS):
        y = _rms(z[:, chunk(c)], C_DIM) * gck_ref[:, chunk(c)]
        kcf_ref[:, chunk(c)] = y
        kcb_ref[:, chunk(c)] = y.astype(BF16)
    z = proj(OFF_CV, C_W)
    vcf_ref[...] = z
    for c in range(C_HEADS):
        if transposed:
            _put_values_t(vcb_ref, c, z[:, chunk(c)])
        else:
            put(vcb_ref, c, z[:, chunk(c)])


STACKED_OUTS = (1, 3, 6, 7)
INTERLEAVED_OUTS = (1, 3)


def _in_proj(x, lw, tabs, *, bm, seg_rows, keep, transposed, stack=None):
    n, d_model = x.shape
    nb = n // bm
    nbs = seg_rows // bm
    nbk = keep // bm
    ntab = tabs[0].shape[0] // bm

    def row(i):
        return (i, 0)

    def const(i):
        return (0, 0)

    def tab(i):
        return (i % ntab, 0)

    def kept(i):
        return ((i // nbs) * nbk + jnp.maximum(i % nbs - (nbs - nbk), 0), 0)

    def rows(width):
        return pl.BlockSpec((bm, width), row)

    def full(a):
        return pl.BlockSpec(a.shape, const, pipeline_mode=pl.Buffered(1))

    fl = '_flash' if transposed else ''
    gains = [lw['gaq' + fl], lw['gak'], lw['gbqa'], lw['gbkv'], lw['gbqn' + fl], lw['gbqr' + fl],
             lw['gcq' + fl], lw['gck']]
    in_specs = ([rows(d_model), full(lw['g_attn']), full(lw['w_a']), full(lw['w_qb'])]
                + [full(g) for g in gains]
                + [pl.BlockSpec((bm, LANE), tab) for _ in tabs])
    out_shape = [
        jax.ShapeDtypeStruct((n, A_W), BF16),
        jax.ShapeDtypeStruct((n, A_W), F32),
        jax.ShapeDtypeStruct((n, A_W), BF16),
        jax.ShapeDtypeStruct((n, AV_W), F32),
        jax.ShapeDtypeStruct((n, AV_W), BF16),
        jax.ShapeDtypeStruct((n, B_HEADS * B_PAD), BF16),
        jax.ShapeDtypeStruct((n, B_KV_LORA), F32),
        jax.ShapeDtypeStruct((n, B_ROPE), F32),
        jax.ShapeDtypeStruct((n, C_W), BF16),
        jax.ShapeDtypeStruct((n, C_W), BF16),
        jax.ShapeDtypeStruct((n, C_W), BF16),
        jax.ShapeDtypeStruct((n // seg_rows * keep, C_W), F32),
        jax.ShapeDtypeStruct((n // seg_rows * keep, C_W), F32),
    ]
    out_specs = [rows(s.shape[1]) for s in out_shape[:11]] + [pl.BlockSpec((bm, C_W), kept)] * 2
    if transposed:
        for k, width in ((0, A_W), (4, A_HEADS * VT_ROWS), (5, B_HEADS * B_PAD), (8, C_W),
                         (10, C_HEADS * VT_ROWS)):
            out_shape[k] = jax.ShapeDtypeStruct((width, n), BF16)
            out_specs[k] = pl.BlockSpec((width, bm), lambda i: (0, i))
    args = [x, lw['g_attn'], lw['w_a'], lw['w_qb'], *gains, *tabs]
    aliases = {}
    if stack is not None:
        layer, depth, buffers = stack
        for k in STACKED_OUTS:
            width = out_shape[k].shape[1]
            split = width // LANE if k in INTERLEAVED_OUTS else 1
            out_shape[k] = jax.ShapeDtypeStruct((depth * n * split, width // split), F32)
            out_specs[k] = pl.BlockSpec((bm * split, width // split), lambda i: (layer * nb + i, 0))
        if buffers is not None:
            for pos, k in enumerate(STACKED_OUTS):
                aliases[len(args)] = k
                args.append(buffers[pos])
                in_specs.append(pl.BlockSpec(memory_space=pl.ANY))
    return pl.pallas_call(
        functools.partial(_in_proj_kernel, transposed=transposed, n_alias=len(aliases),
                          interleaved=stack is not None),
        grid=(nb,),
        in_specs=in_specs,
        out_specs=out_specs,
        out_shape=out_shape,
        input_output_aliases=aliases,
        compiler_params=_cparams(("arbitrary",)),
        name="in_proj",
    )(*args)


def _mla_kv_kernel(ckv_ref, kpe_ref, w_ref, gn_ref, gr_ref, kb_ref, vb_ref, *, transposed):
    ckv = ckv_ref[...]
    kv = _dot(ckv.astype(BF16), w_ref[...])
    kpe = kpe_ref[...]
    kpe2 = jnp.concatenate([kpe, kpe], axis=1)
    ss_pe = jnp.sum(kpe * kpe, axis=-1, keepdims=True)
    lane_lo = lax.broadcasted_iota(jnp.int32, (1, LANE), 1) < 64
    nope_w = B_HEADS * B_NOPE
    for hh in range(B_HEADS):
        nope = kv[:, hh * B_NOPE:(hh + 1) * B_NOPE]
        ss = jnp.sum(nope * nope, axis=-1, keepdims=True) + ss_pe
        inv = lax.rsqrt(ss * (1.0 / B_QK) + EPS)
        in_half = lane_lo if hh % 2 == 0 else jnp.logical_not(lane_lo)
        kb_ref[:, hh * B_PAD:hh * B_PAD + LANE] = (nope * inv * gn_ref[...]).astype(BF16)
        kb_ref[:, hh * B_PAD + LANE:(hh + 1) * B_PAD] = jnp.where(in_half, kpe2 * inv * gr_ref[...], 0.0).astype(BF16)
    if transposed:
        for c in range(B_HEADS):
            _put_values_t(vb_ref, c, kv[:, nope_w + c * LANE:nope_w + (c + 1) * LANE])
    else:
        vb_ref[...] = kv[:, nope_w:].astype(BF16)


def _mla_kv(ckv, kpe, lw, *, bm, n=None, row0=0, transposed=False):
    n = ckv.shape[0] if n is None else n
    blk0 = row0 // bm
    assert row0 % bm == 0
    if transposed:
        vb_shape, vb_spec = (B_HEADS * VT_ROWS, n), pl.BlockSpec((B_HEADS * VT_ROWS, bm), lambda i: (0, i))
    else:
        vb_shape, vb_spec = (n, BV_W), pl.BlockSpec((bm, BV_W), lambda i: (i, 0))

    def row(i):
        return (i, 0)

    def src_row(i):
        return (blk0 + i, 0)

    def const(i):
        return (0, 0)

    return pl.pallas_call(
        functools.partial(_mla_kv_kernel, transposed=transposed),
        grid=(n // bm,),
        in_specs=[pl.BlockSpec((bm, B_KV_LORA), src_row), pl.BlockSpec((bm, B_ROPE), src_row),
                  pl.BlockSpec(lw['w_kvb'].shape, const), pl.BlockSpec((1, LANE), const),
                  pl.BlockSpec((1, LANE), const)],
        out_specs=[pl.BlockSpec((bm, B_HEADS * B_PAD), row), vb_spec],
        out_shape=[jax.ShapeDtypeStruct((n, B_HEADS * B_PAD), BF16), jax.ShapeDtypeStruct(vb_shape, BF16)],
        compiler_params=_cparams(("arbitrary",)),
        name="mla_kv",
    )(ckv, kpe, lw['w_kvb'], lw['gbkn'], lw['gbkr'])


def _split_components(q):
    lane_lo = lax.broadcasted_iota(jnp.int32, (1, LANE), 1) < 64
    zero = jnp.zeros_like(q)
    return jnp.concatenate([jnp.where(lane_lo, q, zero), jnp.where(lane_lo, zero, q)], axis=0)


def _diff_out(o, rows, lam_ref, sg_ref):
    o = o[:rows] - lam_ref[...] * o[rows:]
    return _rms(o, A_V_DIM) * sg_ref[...]


def _flash_queries(qt_ref, diff):
    qt = qt_ref[...]
    if diff:
        row_lo = lax.broadcasted_iota(jnp.int32, (qt.shape[0], 1), 0) < A_QK_DIM
        zero = jnp.zeros_like(qt)
        qt = jnp.concatenate([jnp.where(row_lo, qt, zero), jnp.where(row_lo, zero, qt)], axis=1)
    return qt


def _flash_finish(acc, o_ref, lam_ref, sg_ref, tq, diff):
    ot = acc[:A_V_DIM] / acc[A_V_DIM:A_V_DIM + 1]
    if diff:
        o = (ot[:, :tq] - lam_ref[:, :1] * ot[:, tq:]).T
        o = _rms(o, A_V_DIM) * sg_ref[...]
    else:
        o = ot.T
    o_ref[...] = o.astype(o_ref.dtype)


def _flash_plain_kernel(*refs, tq, tk, diff, group):
    if diff:
        qt_ref, k_ref, vt_ref, mask_ref, lam_ref, sg_ref, o_ref, acc_sc = refs
    else:
        qt_ref, k_ref, vt_ref, mask_ref, o_ref, acc_sc = refs
        lam_ref = sg_ref = None
    i = pl.program_id(2)
    qt = _flash_queries(qt_ref, diff)
    per_q = tq // tk

    def block(j, diag):
        start = pl.multiple_of(j * tk, tk)
        pt = jnp.exp2(_dot(k_ref[pl.ds(start, tk), :], qt))
        if diag is not None:
            pt = jnp.where(mask_ref[diag] > 0, pt, 0.0)
        return _dot(vt_ref[:, pl.ds(start, tk)], pt.astype(BF16))

    acc_sc[...] = jnp.zeros(acc_sc.shape, F32)

    def add_blocks(j0, nblocks):
        acc = acc_sc[...]
        for jj in range(nblocks):
            acc = acc + block(j0 + jj, None)
        acc_sc[...] = acc

    def trips(jb, carry):
        add_blocks(jb * (group * per_q), group * per_q)
        return carry

    def leftover(r, carry):
        add_blocks((i - 1) * per_q, per_q)
        return carry

    lax.fori_loop(0, i // group, trips, 0)
    if group == 2:
        lax.fori_loop(0, i % group, leftover, 0)
    else:
        assert group == 1
    acc = acc_sc[...]
    for jj in range(per_q):
        acc = acc + block(i * per_q + jj, jj)
    _flash_finish(acc, o_ref, lam_ref, sg_ref, tq, diff)


def _flash_kernel(*refs, tq, tk, diff):
    if diff:
        qt_ref, k_ref, vt_ref, mask_ref, lam_ref, sg_ref, o_ref, m_sc, a_sc, acc_sc, s_sc, p_sc = refs
    else:
        qt_ref, k_ref, vt_ref, mask_ref, o_ref, m_sc, a_sc, acc_sc, s_sc, p_sc = refs
        lam_ref = sg_ref = None
    i = pl.program_id(2)
    qt = _flash_queries(qt_ref, diff)
    per_q = tq // tk

    def scores(j):
        start = pl.multiple_of(j * tk, tk)
        return _dot(k_ref[pl.ds(start, tk), :], qt)

    def softmax(st, m_prev, diag):
        if diag is not None:
            st = jnp.where(mask_ref[diag] > 0, st, NEG)
        m_new = jnp.maximum(m_prev, jnp.max(st, axis=0, keepdims=True))
        return m_new, jnp.exp2(m_prev - m_new), jnp.exp2(st - m_new).astype(BF16)

    def weighted_values(j, acc, alpha, pt):
        start = pl.multiple_of(j * tk, tk)
        return alpha * acc + _dot(vt_ref[:, pl.ds(start, tk)], pt)

    m_sc[...] = jnp.full(m_sc.shape, NEG, F32)
    a_sc[...] = jnp.ones(a_sc.shape, F32)
    acc_sc[...] = jnp.zeros(acc_sc.shape, F32)
    p_sc[...] = jnp.zeros(p_sc.shape, BF16)
    s_sc[...] = scores(0)

    def body(jb, carry):
        m, alpha_prev, acc, p_prev, st = m_sc[...], a_sc[...], acc_sc[...], p_sc[...], s_sc[...]
        for jj in range(per_q):
            j = jb * per_q + jj
            st_next = scores(j + 1)
            m, alpha, pt = softmax(st, m, None)
            acc = weighted_values(jnp.maximum(j - 1, 0), acc, alpha_prev, p_prev)
            alpha_prev, p_prev, st = alpha, pt, st_next
        m_sc[...], a_sc[...], acc_sc[...], p_sc[...], s_sc[...] = m, alpha_prev, acc, p_prev, st
        return carry

    lax.fori_loop(0, i, body, 0)
    m, alpha_prev, acc, p_prev, st = m_sc[...], a_sc[...], acc_sc[...], p_sc[...], s_sc[...]
    for jj in range(per_q):
        j = i * per_q + jj
        st_next = scores(j + 1) if jj + 1 < per_q else None
        m, alpha, pt = softmax(st, m, jj)
        acc = weighted_values(jnp.maximum(j - 1, 0), acc, alpha_prev, p_prev)
        alpha_prev, p_prev, st = alpha, pt, st_next
    acc = weighted_values(i * per_q + per_q - 1, acc, alpha_prev, p_prev)
    _flash_finish(acc, o_ref, lam_ref, sg_ref, tq, diff)


def _flash(qt, k, vt, *, batch, heads, dk, tq, tk, diff, plain, group=1, lam=None, sg=None):
    n = k.shape[0]
    seq = n // batch
    nq = seq // tq
    mq = 2 * tq if diff else tq
    per_q = tq // tk
    kpos = jnp.arange(tq, dtype=jnp.int32).reshape(per_q, tk, 1)
    qpos = jnp.arange(mq, dtype=jnp.int32).reshape(1, 1, mq) % tq
    mask = ((kpos // CHUNK) <= (qpos // CHUNK)).astype(F32)
    in_specs = [pl.BlockSpec((dk, tq), lambda b, h, i: (h, b * nq + i)),
                pl.BlockSpec((seq, dk), lambda b, h, i: (b, h)),
                pl.BlockSpec((VT_ROWS, seq), lambda b, h, i: (h, b)),
                pl.BlockSpec((per_q, tk, mq), lambda b, h, i: (0, 0, 0), pipeline_mode=pl.Buffered(1))]
    args = [qt, k, vt, mask]
    if diff:
        in_specs += [pl.BlockSpec((1, LANE), lambda b, h, i: (0, 0))] * 2
        args += [lam, sg]
    scratch = [pltpu.VMEM((VT_ROWS, mq), F32)]
    if not plain:
        scratch = [pltpu.VMEM((1, mq), F32), pltpu.VMEM((1, mq), F32)] + scratch + [
            pltpu.VMEM((tk, mq), F32), pltpu.VMEM((tk, mq), BF16)]
    return pl.pallas_call(
        (functools.partial(_flash_plain_kernel, tq=tq, tk=tk, diff=diff, group=group) if plain
         else functools.partial(_flash_kernel, tq=tq, tk=tk, diff=diff)),
        grid=(batch, heads, nq),
        in_specs=in_specs,
        out_specs=pl.BlockSpec((tq, LANE), lambda b, h, i: (b * nq + i, h)),
        out_shape=jax.ShapeDtypeStruct((n, heads * LANE), BF16),
        scratch_shapes=scratch,
        compiler_params=_cparams(("parallel", "parallel", "arbitrary")),
        name=("flash_diff" if diff else "flash_mla") + ("_plain" if plain else ""),
    )(*args)


def _band_kernel(qt_ref, k_ref, vt_ref, bias_ref, o_ref, *, tq, nrel):
    i = pl.program_id(2)
    qt = qt_ref[...]

    def attend(rels):
        def keys(rel):
            return pl.ds(pl.multiple_of((i - rel) * tq, tq), tq)

        st = [_dot(k_ref[keys(rel), :], qt) + bias_ref[0, rel] for rel in rels]
        m = functools.reduce(jnp.maximum, [jnp.max(s, axis=0, keepdims=True) for s in st])
        acc = sum(_dot(vt_ref[:, keys(rel)], jnp.exp2(s - m).astype(BF16)) for rel, s in zip(rels, st))
        o_ref[...] = (acc[:C_DIM] / acc[C_DIM:C_DIM + 1]).T.astype(o_ref.dtype)

    for avail in range(nrel):
        cond = (i == avail) if avail < nrel - 1 else (i >= avail)
        pl.when(cond)(functools.partial(attend, tuple(range(avail + 1))))


def _band_bias_t(table, tq, nrel):
    bias = jnp.swapaxes(_band_bias(table, tq, nrel), 2, 3) * LOG2E
    k_ch = jnp.arange(tq).reshape(1, tq, 1) // CHUNK
    q_ch = (jnp.arange(nrel).reshape(nrel, 1, 1) * tq + jnp.arange(tq).reshape(1, 1, tq)) // CHUNK
    visible = (k_ch <= q_ch) & (k_ch >= q_ch - C_PAST_CHUNKS)
    return jnp.where(visible[None], bias, NEG)


def _band_bias(table, tq, nrel):
    heads = table.shape[0]
    c_all = nrel * tq
    n = tq + c_all - 1
    n_lo = tq - 1 - REL_CLIP
    n_hi = n - n_lo - (2 * REL_CLIP + 1)
    assert n_hi >= 0
    u = jnp.concatenate([jnp.broadcast_to(table[:, :1], (heads, max(n_lo, 0))), table[:, max(-n_lo, 0):],
                         jnp.broadcast_to(table[:, -1:], (heads, n_hi))], axis=1)
    w = jnp.concatenate([u[:, ::-1], jnp.zeros((heads, 1), table.dtype)], axis=1)
    m = jnp.tile(w, (1, tq))[:, :tq * n].reshape(heads, tq, n)
    t = m[:, :, tq - 1:tq - 1 + c_all]
    return t.reshape(heads, tq, nrel, tq).transpose(0, 2, 1, 3)[:, ::-1]


def _band(qt, k, vt, bias, *, batch, tq):
    n = k.shape[0]
    seq = n // batch
    nq = seq // tq
    nrel = bias.shape[1]
    return pl.pallas_call(
        functools.partial(_band_kernel, tq=tq, nrel=nrel),
        grid=(batch, C_HEADS, nq),
        in_specs=[pl.BlockSpec((LANE, tq), lambda b, h, i: (h, b * nq + i)),
                  pl.BlockSpec((seq, LANE), lambda b, h, i: (b, h)),
                  pl.BlockSpec((VT_ROWS, seq), lambda b, h, i: (h, b)),
                  pl.BlockSpec((1, nrel, tq, tq), lambda b, h, i: (h, 0, 0, 0))],
        out_specs=pl.BlockSpec((tq, LANE), lambda b, h, i: (b * nq + i, h)),
        out_shape=jax.ShapeDtypeStruct((n, C_W), BF16),
        compiler_params=_cparams(("parallel", "parallel", "arbitrary")),
        name="band",
    )(qt, k, vt, bias)


def _decode_kernel(*refs, heads, dk, diff, biased, interleaved):
    refs = list(refs)
    q_ref, kc_ref, vc_ref, kn_ref, vn_ref = refs[:5]
    rest = refs[5:]
    if biased:
        bc_ref, bn_ref, mc_ref, mn_ref = rest[:4]
        rest = rest[4:]
    if diff:
        lam_ref, sg_ref = rest[:2]
        rest = rest[2:]
    o_ref = rest[0]
    rows = q_ref.shape[0]
    for h in range(heads):
        qk = slice(h * dk, (h + 1) * dk)
        vv = slice(h * LANE, (h + 1) * LANE)
        q = q_ref[:, qk]
        if diff:
            q = _split_components(q)
        if interleaved:
            past = kc_ref.shape[1] // heads
            k_cache = kc_ref[0, pl.ds(h, past, stride=heads), :]
            v_cache = vc_ref[0, pl.ds(h, past, stride=heads), :]
        else:
            k_cache, v_cache = kc_ref[0, :, qk], vc_ref[0, :, vv]
        s_c = _dot_nt(q, k_cache.astype(BF16))
        s_n = _dot_nt(q, kn_ref[:, qk])
        if biased:
            s_c = jnp.where(mc_ref[...] > 0, s_c + bc_ref[h], NEG)
            s_n = jnp.where(mn_ref[...] > 0, s_n + bn_ref[h], NEG)
        m = jnp.maximum(jnp.max(s_c, axis=-1, keepdims=True), jnp.max(s_n, axis=-1, keepdims=True))
        p_c = jnp.exp(s_c - m)
        p_n = jnp.exp(s_n - m)
        l = jnp.sum(p_c, axis=-1, keepdims=True) + jnp.sum(p_n, axis=-1, keepdims=True)
        o = (_dot(p_c.astype(BF16), v_cache.astype(BF16)) + _dot(p_n.astype(BF16), vn_ref[:, vv])) / l
        if diff:
            o = _diff_out(o, rows, lam_ref, sg_ref)
        o_ref[:, vv] = o.astype(o_ref.dtype)


def _decode(q, kc, vc, kn, vn, *, heads, dk, rows, stream0=0, interleaved=False, diff=False, lam=None, sg=None,
            bias=None):
    n = q.shape[0]
    streams = n // rows
    past = kc.shape[1] // heads if interleaved else kc.shape[1]
    assert not interleaved or dk == LANE
    in_specs = [pl.BlockSpec((rows, heads * dk), lambda b: (b, 0)),
                pl.BlockSpec((1,) + kc.shape[1:], lambda b: (stream0 + b, 0, 0)),
                pl.BlockSpec((1,) + vc.shape[1:], lambda b: (stream0 + b, 0, 0)),
                pl.BlockSpec((rows, heads * dk), lambda b: (b, 0)),
                pl.BlockSpec((rows, heads * LANE), lambda b: (b, 0))]
    args = [q, kc, vc, kn, vn]
    if bias is not None:
        bias_c, bias_n, mask_c, mask_n = bias
        in_specs += [pl.BlockSpec((heads, rows, past), lambda b: (0, 0, 0)),
                     pl.BlockSpec((heads, rows, rows), lambda b: (0, 0, 0)),
                     pl.BlockSpec((rows, past), lambda b: (0, 0)),
                     pl.BlockSpec((rows, rows), lambda b: (0, 0))]
        args += [bias_c, bias_n, mask_c, mask_n]
    if diff:
        in_specs += [pl.BlockSpec((1, LANE), lambda b: (0, 0))] * 2
        args += [lam, sg]
    return pl.pallas_call(
        functools.partial(_decode_kernel, heads=heads, dk=dk, diff=diff, biased=bias is not None,
                          interleaved=interleaved),
        grid=(streams,),
        in_specs=in_specs,
        out_specs=pl.BlockSpec((rows, heads * LANE), lambda b: (b, 0)),
        out_shape=jax.ShapeDtypeStruct((n, heads * LANE), BF16),
        compiler_params=_cparams(("parallel",)),
        name="decode",
    )(*args)


def _merge_kernel(x_ref, g_ref, oa_ref, ob_ref, oc_ref, wga_ref, wgb_ref, wgc_ref, bga_ref, bgb_ref, bgc_ref,
                  wba_ref, wbb_ref, wbc_ref, wo_ref, out_ref, h_sc):
    j = pl.program_id(1)

    @pl.when(j == 0)
    def _():
        x = x_ref[...]
        h_sc[...] = (_rms(x, x.shape[1]) * g_ref[...]).astype(BF16)
        out_ref[...] = x

    h = h_sc[...]

    def branch(o_ref, wg_ref, bg_ref, wb_ref):
        gate = jax.nn.sigmoid(_dot(h, wg_ref[...]) + bg_ref[...])
        return gate * _dot(o_ref[...], wb_ref[...])

    merged = (branch(oa_ref, wga_ref, bga_ref, wba_ref) + branch(ob_ref, wgb_ref, bgb_ref, wbb_ref)
              + branch(oc_ref, wgc_ref, bgc_ref, wbc_ref))
    out_ref[...] += _dot(merged.astype(BF16), wo_ref[...])


def _merge(x, oa, ob, oc, lw, *, bm, bn):
    n, d_model = x.shape
    nbn = d_model // bn

    def row(i, j):
        return (i, 0)

    def const(i, j):
        return (0, 0)

    def col(k):
        return lambda i, j: (0, k * nbn + j)

    layer = lw['layer']

    def wcol(rows, k):
        return pl.BlockSpec((None, rows, bn), lambda i, j: (layer, 0, k * nbn + j))

    in_specs = [pl.BlockSpec((bm, d_model), row), pl.BlockSpec((1, d_model), const),
                pl.BlockSpec((bm, AV_W), row), pl.BlockSpec((bm, BV_W), row), pl.BlockSpec((bm, C_W), row),
                wcol(d_model, 0), wcol(d_model, 1), wcol(d_model, 2),
                pl.BlockSpec((1, bn), col(0)), pl.BlockSpec((1, bn), col(1)), pl.BlockSpec((1, bn), col(2)),
                wcol(AV_W, 0), wcol(BV_W, 0), wcol(C_W, 0),
                pl.BlockSpec((None, bn, d_model), lambda i, j: (layer, j, 0))]
    return pl.pallas_call(
        _merge_kernel,
        grid=(n // bm, nbn),
        in_specs=in_specs,
        out_specs=pl.BlockSpec((bm, d_model), row),
        out_shape=jax.ShapeDtypeStruct((n, d_model), F32),
        scratch_shapes=[pltpu.VMEM((bm, d_model), BF16)],
        compiler_params=_cparams(("parallel", "arbitrary")),
        name="merge",
    )(x, lw['g_attn'], oa, ob, oc, lw['w_g'], lw['w_g'], lw['w_g'], lw['b_g'], lw['b_g'], lw['b_g'],
      lw['w_br_a'], lw['w_br_b'], lw['w_br_c'], lw['w_o'])


def _mlp_kernel(x_ref, g_ref, wu_ref, wd_ref, out_ref, h_sc):
    j = pl.program_id(1)

    @pl.when(j == 0)
    def _():
        x = x_ref[...]
        h_sc[...] = (_rms(x, x.shape[1]) * g_ref[...]).astype(BF16)
        out_ref[...] = x

    u = jnp.maximum(_dot(h_sc[...], wu_ref[...]), 0.0)
    out_ref[...] += _dot((u * u).astype(BF16), wd_ref[...])


def _mlp(x, lw, *, bm, bf):
    n, d_model = x.shape
    d_ff = lw['w_up'].shape[2]
    layer = lw['layer']
    return pl.pallas_call(
        _mlp_kernel,
        grid=(n // bm, d_ff // bf),
        in_specs=[pl.BlockSpec((bm, d_model), lambda i, j: (i, 0)), pl.BlockSpec((1, d_model), lambda i, j: (0, 0)),
                  pl.BlockSpec((None, d_model, bf), lambda i, j: (layer, 0, j)),
                  pl.BlockSpec((None, bf, d_model), lambda i, j: (layer, j, 0))],
        out_specs=pl.BlockSpec((bm, d_model), lambda i, j: (i, 0)),
        out_shape=jax.ShapeDtypeStruct((n, d_model), F32),
        scratch_shapes=[pltpu.VMEM((bm, d_model), BF16)],
        compiler_params=_cparams(("parallel", "arbitrary")),
        name="mlp",
    )(x, lw['g_mlp'], lw['w_up'], lw['w_down'])


def _rope_tables(pos, rot):
    half = rot // 2
    inv = jnp.power(ROPE_THETA, -jnp.arange(half, dtype=F32) / half)
    ang = pos.astype(F32)[:, None] * inv[None, :]
    cos, sin = jnp.cos(ang), jnp.sin(ang)
    npos = pos.shape[0]
    one = jnp.ones((npos, 64 - rot), F32)
    c64 = jnp.concatenate([cos, cos, one], axis=1)
    s1 = jnp.concatenate([-sin, jnp.zeros((npos, 64 - half), F32)], axis=1)
    s2 = jnp.concatenate([jnp.zeros((npos, half), F32), sin, jnp.zeros((npos, 64 - rot), F32)], axis=1)
    return [jnp.tile(t, (1, 2)) for t in (c64, s1, s2)]


def _score_bound(gq, gk, d):
    return math.sqrt(d) * LOG2E * jnp.max(jnp.abs(gq)) * jnp.max(jnp.abs(gk))


def _layer_weights(l, p, stacks):
    d_model = p['w_in'].shape[1]
    w_in = p['w_in'][l]
    o = [0]
    for width in (A_W, A_W, AV_W, B_Q_LORA, B_KV_LORA + B_ROPE, C_W, C_W, C_W):
        o.append(o[-1] + width)
    cols = lambda a, b: w_in[:, a:b]
    w_a = jnp.concatenate([
        cols(o[0], o[1]), cols(o[1], o[2]), cols(o[2], o[3]), cols(o[3], o[4]),
        cols(o[4], o[4] + B_KV_LORA), cols(o[5], o[6]), cols(o[6], o[7]), cols(o[7], o[8]),
        cols(o[4] + B_KV_LORA, o[5]), jnp.zeros((d_model, LANE - B_ROPE), F32)], axis=1).astype(BF16)
    w_qb = p['w_qb'][l].reshape(B_Q_LORA, B_HEADS, B_QK)
    w_qb = jnp.concatenate([w_qb[:, :, :B_NOPE].reshape(B_Q_LORA, -1),
                            w_qb[:, :, B_NOPE:].reshape(B_Q_LORA, -1)], axis=1).astype(BF16)
    w_kvb = p['w_kvb'][l].reshape(B_KV_LORA, B_HEADS, B_NOPE + B_V_DIM)
    w_kvb = jnp.concatenate([w_kvb[:, :, :B_NOPE].reshape(B_KV_LORA, -1),
                             w_kvb[:, :, B_NOPE:].reshape(B_KV_LORA, -1)], axis=1).astype(BF16)
    lam_init = 0.8 - 0.6 * math.exp(-0.3 * l)
    lf = p['a_lambda'][l].astype(F32)
    lam = jnp.exp(jnp.sum(lf[0] * lf[1])) - jnp.exp(jnp.sum(lf[2] * lf[3])) + lam_init

    def tiled(g, reps, scale=1.0):
        return jnp.tile(g * scale, reps)[None, :]

    bqn = p['b_q_norm'][l]
    bkn = p['b_k_norm'][l]
    return {
        'g_attn': p['attn_norm'][l][None, :],
        'g_mlp': p['mlp_norm'][l][None, :],
        'w_a': w_a,
        'b_g': p['b_gate'][l][None, :],
        'w_qb': w_qb,
        'w_kvb': w_kvb,
        'gaq': tiled(p['a_q_norm'][l], 2 * A_HEADS, A_QK_DIM ** -0.5),
        'gak': tiled(p['a_k_norm'][l], 2 * A_HEADS),
        'gbqa': p['b_qa_norm'][l][None, :],
        'gbkv': p['b_kv_norm'][l][None, :],
        'gbqn': tiled(bqn[:B_NOPE], 1, B_QK ** -0.5),
        'gbqr': tiled(bqn[B_NOPE:], 2, B_QK ** -0.5),
        'gaq_flash': tiled(p['a_q_norm'][l], 2 * A_HEADS, A_QK_DIM ** -0.5 * LOG2E),
        'gbqn_flash': tiled(bqn[:B_NOPE], 1, B_QK ** -0.5 * LOG2E),
        'gbqr_flash': tiled(bqn[B_NOPE:], 2, B_QK ** -0.5 * LOG2E),
        'gcq_flash': tiled(p['c_q_norm'][l], C_HEADS, C_DIM ** -0.5 * LOG2E),
        'gbkn': tiled(bkn[:B_NOPE], 1),
        'gbkr': tiled(bkn[B_NOPE:], 2),
        'gcq': tiled(p['c_q_norm'][l], C_HEADS, C_DIM ** -0.5),
        'gck': tiled(p['c_k_norm'][l], C_HEADS),
        'lam': jnp.broadcast_to(lam, (1, LANE)).astype(F32),
        'sg': (p['a_sub_norm'][l] * (1.0 - lam_init))[None, :],
        'bound_a': _score_bound(p['a_q_norm'][l], p['a_k_norm'][l], A_QK_DIM),
        'bound_b': _score_bound(bqn, bkn, B_QK),
        'layer': l,
        **stacks,
    }


def _stacked_weights(p):
    gate_off = p['w_in'].shape[2] - 3 * p['w_in'].shape[1]
    stacks = {name: p[name].astype(BF16) for name in ('w_br_a', 'w_br_b', 'w_br_c', 'w_o', 'w_up', 'w_down')}
    stacks['w_g'] = p['w_in'][:, :, gate_off:].astype(BF16)
    return stacks


def _pick(n, prefs):
    for b in prefs:
        if n % b == 0:
            return b
    raise ValueError(f"no block size in {prefs} divides {n}")


def kernel(x_prompt, x_sample, cache_a_k, cache_a_v, cache_b_ckv, cache_b_kpe, cache_c_k, cache_c_v,
           attn_norm, w_in, b_gate, a_q_norm, a_k_norm, a_lambda, a_sub_norm,
           b_qa_norm, b_kv_norm, w_qb, w_kvb, b_q_norm, b_k_norm,
           c_q_norm, c_k_norm, c_rel_bias, w_br_a, w_br_b, w_br_c, w_o,
           mlp_norm, w_up, w_down):
    p = dict(attn_norm=attn_norm, w_in=w_in, b_gate=b_gate, a_q_norm=a_q_norm, a_k_norm=a_k_norm,
             a_lambda=a_lambda, a_sub_norm=a_sub_norm, b_qa_norm=b_qa_norm, b_kv_norm=b_kv_norm,
             w_qb=w_qb, w_kvb=w_kvb, b_q_norm=b_q_norm, b_k_norm=b_k_norm, c_q_norm=c_q_norm,
             c_k_norm=c_k_norm, w_br_a=w_br_a, w_br_b=w_br_b, w_br_c=w_br_c, w_o=w_o, mlp_norm=mlp_norm,
             w_up=w_up, w_down=w_down)
    depth = w_in.shape[0]
    stacks = _stacked_weights(p)
    batch, seq, d_model = x_prompt.shape
    streams, dec, _ = x_sample.shape
    past = cache_a_k.shape[2]
    c_past = cache_c_k.shape[2]
    c_keep = min(C_WINDOW, seq)
    n_p, n_s = batch * seq, streams * dec
    assert seq % CHUNK == 0 and c_keep % CHUNK == 0

    pos_p = jnp.arange(seq)
    pos_s = past + jnp.arange(dec)
    tabs_p = _rope_tables(pos_p, A_ROT) + _rope_tables(pos_p, B_ROPE)
    tabs_s = [jnp.tile(t, (streams, 1)) for t in _rope_tables(pos_s, A_ROT) + _rope_tables(pos_s, B_ROPE)]

    kc_pos = jnp.concatenate([past - c_past + jnp.arange(c_past), pos_s])
    q_ch, k_ch = pos_s // CHUNK, kc_pos // CHUNK
    c_mask = ((k_ch[None, :] <= q_ch[:, None]) & (k_ch[None, :] >= q_ch[:, None] - C_PAST_CHUNKS)).astype(F32)
    rel_s = jnp.clip(pos_s[:, None] - kc_pos[None, :], -REL_CLIP, REL_CLIP) + REL_CLIP

    bm_in = _pick(seq, (256, 128, 64))
    bm_tok = _pick(seq, (512, 256, 128, 64))
    t_a = _pick(seq, (512, 256, 128))
    t_b = _pick(seq, (1024, 512, 256, 128))
    tk_a, tk_b = t_a, t_b // 2
    tp_a = (t_b, t_b // 2)
    tp_b = (t_b, t_b // 2)
    t_c = _pick(seq, (512, 256, 128))
    nrel_c = C_WINDOW // t_c + 1
    bm_wide = _pick(seq, (1024, 512, 256, 128, 64))
    bn_merge = 512
    bf_mlp = 512
    bm_cache = _pick(streams * past, (512, 256, 128, 64, 16))

    ca_k = cache_a_k.reshape(depth * streams, past * A_HEADS, LANE)
    ca_v = cache_a_v.reshape(depth * streams, past * A_HEADS, LANE)
    cb_ckv = cache_b_ckv.reshape(depth * streams * past, B_KV_LORA)
    cb_kpe = cache_b_kpe.reshape(depth * streams * past, B_ROPE)
    cc_k = cache_c_k.reshape(depth * streams, c_past * C_HEADS, LANE)
    cc_v = cache_c_v.reshape(depth * streams, c_past * C_HEADS, LANE)

    xp = x_prompt.reshape(n_p, d_model)
    xs = x_sample.reshape(n_s, d_model)
    outs = [[] for _ in range(12)]
    stacked = None
    for l in range(depth):
        lw = _layer_weights(l, p, stacks)

        (qa, kaf, kab, vaf, vab, qb, ckv, kpe, qc, kcb, vcb, kcf, vcf) = _in_proj(
            xp, lw, tabs_p, bm=bm_in, seg_rows=seq, keep=c_keep, transposed=True, stack=(l, depth, stacked))
        stacked = (kaf, vaf, ckv, kpe)
        kb, vb = _mla_kv(ckv, kpe, lw, bm=bm_tok, n=n_p, row0=l * n_p, transposed=True)
        flash_a = functools.partial(_flash, qa, kab, vab, batch=batch, heads=A_HEADS, dk=LANE, diff=True,
                                    lam=lw['lam'], sg=lw['sg'])
        flash_b = functools.partial(_flash, qb, kb, vb, batch=batch, heads=B_HEADS, dk=B_PAD, diff=False)
        oa = lax.cond(lw['bound_a'] <= FLASH_PLAIN_MAX_LOG2,
                      functools.partial(flash_a, tq=tp_a[0], tk=tp_a[1], plain=True),
                      functools.partial(flash_a, tq=t_a, tk=tk_a, plain=False))
        ob = lax.cond(lw['bound_b'] <= FLASH_PLAIN_MAX_LOG2,
                      functools.partial(flash_b, tq=tp_b[0], tk=tp_b[1], plain=True, group=2),
                      functools.partial(flash_b, tq=t_b, tk=tk_b, plain=False))
        oc = _band(qc, kcb, vcb, _band_bias_t(c_rel_bias[l], t_c, nrel_c), batch=batch, tq=t_c)
        xp = _merge(xp, oa, ob, oc, lw, bm=bm_tok, bn=bn_merge)
        xp = _mlp(xp, lw, bm=bm_wide, bf=bf_mlp)
        for dst, val in zip(outs[4:6], (kcf, vcf)):
            dst.append(val)

        (qa, kaf, kab, vaf, vab, qb, ckv, kpe, qc, kcb, vcb, kcf, vcf) = _in_proj(
            xs, lw, tabs_s, bm=n_s, seg_rows=n_s, keep=n_s, transposed=False)
        kb, vb = _mla_kv(ckv, kpe, lw, bm=n_s)
        kb_c, vb_c = _mla_kv(cb_ckv, cb_kpe, lw, bm=bm_cache, n=streams * past, row0=l * streams * past)
        oa = _decode(qa, ca_k, ca_v, kab, vab, heads=A_HEADS, dk=LANE, rows=dec, stream0=l * streams,
                     interleaved=True, diff=True, lam=lw['lam'], sg=lw['sg'])
        ob = _decode(qb, kb_c.reshape(streams, past, B_HEADS * B_PAD), vb_c.reshape(streams, past, BV_W),
                     kb, vb, heads=B_HEADS, dk=B_PAD, rows=dec)
        bias_s = c_rel_bias[l][:, rel_s]
        oc = _decode(qc, cc_k, cc_v, kcb, vcb, heads=C_HEADS, dk=LANE, rows=dec, stream0=l * streams,
                     interleaved=True, bias=(bias_s[:, :, :c_past], bias_s[:, :, c_past:], c_mask[:, :c_past], c_mask[:, c_past:]))
        xs = _merge(xs, oa, ob, oc, lw, bm=n_s, bn=bn_merge)
        xs = _mlp(xs, lw, bm=n_s, bf=bf_mlp)
        for dst, val in zip(outs[6:], (kaf, vaf, ckv, kpe, kcf, vcf)):
            dst.append(val)

    st = list(stacked) + [jnp.stack(o) for o in outs[4:]]
    return (xp.reshape(batch, seq, d_model), xs.reshape(streams, dec, d_model),
            st[0].reshape(depth, batch, seq, A_HEADS, 2 * A_QK_DIM),
            st[1].reshape(depth, batch, seq, A_HEADS, A_V_DIM),
            st[2].reshape(depth, batch, seq, B_KV_LORA),
            st[3].reshape(depth, batch, seq, B_ROPE),
            st[4].reshape(depth, batch, c_keep, C_HEADS, C_DIM),
            st[5].reshape(depth, batch, c_keep, C_HEADS, C_DIM),
            st[6].reshape(depth, streams, dec, A_HEADS, 2 * A_QK_DIM),
            st[7].reshape(depth, streams, dec, A_HEADS, A_V_DIM),
            st[8].reshape(depth, streams, dec, B_KV_LORA),
            st[9].reshape(depth, streams, dec, B_ROPE),
            st[10].reshape(depth, streams, dec, C_HEADS, C_DIM),
            st[11].reshape(depth, streams, dec, C_HEADS, C_DIM))
```

```python
import functools
import math

import jax
import jax.numpy as jnp
from jax import lax
from jax.experimental import pallas as pl
from jax.experimental.pallas import tpu as pltpu

F32 = jnp.float32
BF16 = jnp.bfloat16

CHUNK = 64
ROPE_THETA = 500000.0
EPS = 1e-6
NEG = -1e30
LOG2E = math.log2(math.e)
FLASH_PLAIN_MAX_LOG2 = 50.0
LANE = 128

A_HEADS = 4
A_QK_DIM = 64
A_V_DIM = 128
A_ROT = A_QK_DIM // 4
B_HEADS = 8
B_Q_LORA = 512
B_KV_LORA = 256
B_NOPE = 128
B_ROPE = 64
B_V_DIM = 128
B_QK = B_NOPE + B_ROPE
B_PAD = 2 * LANE
BF16_SUBLANES = 16
VT_ROWS = LANE + BF16_SUBLANES
C_HEADS = 4
C_DIM = 128
C_PAST_CHUNKS = 8
C_WINDOW = C_PAST_CHUNKS * CHUNK
REL_CLIP = 128

A_W = A_HEADS * 2 * A_QK_DIM
AV_W = A_HEADS * A_V_DIM
BV_W = B_HEADS * B_V_DIM
C_W = C_HEADS * C_DIM

OFF_AQ = 0
OFF_AK = OFF_AQ + A_W
OFF_AV = OFF_AK + A_W
OFF_BQ = OFF_AV + AV_W
OFF_CKV = OFF_BQ + B_Q_LORA
OFF_CQ = OFF_CKV + B_KV_LORA
OFF_CK = OFF_CQ + C_W
OFF_CV = OFF_CK + C_W
OFF_KPE = OFF_CV + C_W
WA_COLS = OFF_KPE + LANE

VMEM_LIMIT = 56 * 1024 * 1024


def _cparams(sem):
    return pltpu.CompilerParams(dimension_semantics=sem, vmem_limit_bytes=VMEM_LIMIT)


def _rms(x, width):
    ms = jnp.sum(x * x, axis=-1, keepdims=True) * (1.0 / width)
    return x * lax.rsqrt(ms + EPS)


def _rms_half(zc, lane_lo):
    sq = zc * zc
    lo = jnp.sum(jnp.where(lane_lo, sq, 0.0), axis=-1, keepdims=True)
    hi = jnp.sum(jnp.where(lane_lo, 0.0, sq), axis=-1, keepdims=True)
    ms = jnp.where(lane_lo, lo, hi) * (1.0 / 64.0)
    return zc * lax.rsqrt(ms + EPS)


def _rope(y, c, s1, s2, half):
    return y * c + pltpu.roll(y, LANE - half, 1) * s1 + pltpu.roll(y, half, 1) * s2


def _put_values_t(ref, head, val):
    base = head * VT_ROWS
    ref[base:base + LANE, :] = val.T.astype(BF16)
    ref[base + LANE:base + VT_ROWS, :] = jnp.ones((VT_ROWS - LANE, val.shape[0]), BF16)


def _dot(a, b):
    return jnp.dot(a, b, preferred_element_type=F32)


def _dot_nt(a, b):
    return lax.dot_general(a, b, (((1,), (1,)), ((), ())), preferred_element_type=F32)


def _in_proj_kernel(*refs, transposed, n_alias, interleaved):
    (x_ref, g_ref, wa_ref, wqb_ref, gaq_ref, gak_ref, gbqa_ref, gbkv_ref, gbqn_ref, gbqr_ref,
     gcq_ref, gck_ref, ca_ref, s1a_ref, s2a_ref, cb_ref, s1b_ref, s2b_ref) = refs[:18]
    (qa_ref, kaf_ref, kab_ref, vaf_ref, vab_ref, qb_ref, ckv_ref, kpe_ref,
     qc_ref, kcb_ref, vcb_ref, kcf_ref, vcf_ref) = refs[18 + n_alias:]
    d_model = x_ref.shape[1]

    def put(ref, c, val):
        if transposed:
            ref[c * LANE:(c + 1) * LANE, :] = val.T.astype(BF16)
        else:
            ref[:, c * LANE:(c + 1) * LANE] = val.astype(BF16)

    def put_cache(ref, head, val):
        if interleaved:
            ref[pl.ds(head, val.shape[0], stride=A_HEADS), :] = val
        else:
            ref[:, head * LANE:(head + 1) * LANE] = val

    h = (_rms(x_ref[...], d_model) * g_ref[...]).astype(BF16)
    lane_lo = lax.broadcasted_iota(jnp.int32, (1, LANE), 1) < 64
    ca, s1a, s2a = ca_ref[...], s1a_ref[...], s2a_ref[...]
    cb, s1b, s2b = cb_ref[...], s1b_ref[...], s2b_ref[...]

    def proj(off, width):
        return _dot(h, wa_ref[:, off:off + width])

    def chunk(c):
        return slice(c * LANE, (c + 1) * LANE)

    z = proj(OFF_AQ, A_W)
    for c in range(A_W // LANE):
        y = _rms_half(z[:, chunk(c)], lane_lo) * gaq_ref[:, chunk(c)]
        put(qa_ref, c, _rope(y, ca, s1a, s2a, A_ROT // 2))
    z = proj(OFF_AK, A_W)
    for c in range(A_W // LANE):
        y = _rms_half(z[:, chunk(c)], lane_lo) * gak_ref[:, chunk(c)]
        y = _rope(y, ca, s1a, s2a, A_ROT // 2)
        put_cache(kaf_ref, c, y)
        kab_ref[:, chunk(c)] = y.astype(BF16)
    z = proj(OFF_AV, AV_W)
    for c in range(A_HEADS):
        put_cache(vaf_ref, c, z[:, chunk(c)])
        if transposed:
            _put_values_t(vab_ref, c, z[:, chunk(c)])
        else:
            put(vab_ref, c, z[:, chunk(c)])

    z = proj(OFF_BQ, B_Q_LORA)
    cq = (_rms(z, B_Q_LORA) * gbqa_ref[...]).astype(BF16)
    qraw = _dot(cq, wqb_ref[...])
    nope_w = B_HEADS * B_NOPE
    roped = [_rope(qraw[:, nope_w + c * LANE:nope_w + (c + 1) * LANE], cb, s1b, s2b, B_ROPE // 2)
             for c in range(B_HEADS * B_ROPE // LANE)]
    for hh in range(B_HEADS):
        nope = qraw[:, hh * B_NOPE:(hh + 1) * B_NOPE]
        in_half = lane_lo if hh % 2 == 0 else jnp.logical_not(lane_lo)
        rsel = jnp.where(in_half, roped[hh // 2], 0.0)
        ss = jnp.sum(nope * nope, axis=-1, keepdims=True) + jnp.sum(rsel * rsel, axis=-1, keepdims=True)
        inv = lax.rsqrt(ss * (1.0 / B_QK) + EPS)
        put(qb_ref, 2 * hh, nope * inv * gbqn_ref[...])
        put(qb_ref, 2 * hh + 1, rsel * inv * gbqr_ref[...])
    z = proj(OFF_CKV, B_KV_LORA)
    ckv_ref[...] = _rms(z, B_KV_LORA) * gbkv_ref[...]
    z = proj(OFF_KPE, LANE)
    kpe_ref[...] = _rope(z, cb, s1b, s2b, B_ROPE // 2)[:, :B_ROPE]

    z = proj(OFF_CQ, C_W)
    for c in range(C_HEADS):
        put(qc_ref, c, _rms(z[:, chunk(c)], C_DIM) * gcq_ref[:, chunk(c)])
    z = proj(OFF_CK, C_W)
    for c in range(C_HEADS):
        y = _rms(z[:, chunk(c)], C_DIM) * gck_ref[:, chunk(c)]
        kcf_ref[:, chunk(c)] = y
        kcb_ref[:, chunk(c)] = y.astype(BF16)
    z = proj(OFF_CV, C_W)
    vcf_ref[...] = z
    for c in range(C_HEADS):
        if transposed:
            _put_values_t(vcb_ref, c, z[:, chunk(c)])
        else:
            put(vcb_ref, c, z[:, chunk(c)])


STACKED_OUTS = (1, 3, 6, 7)
INTERLEAVED_OUTS = (1, 3)


def _in_proj(x, lw, tabs, *, bm, seg_rows, keep, transposed, stack=None):
    n, d_model = x.shape
    nb = n // bm
    nbs = seg_rows // bm
    nbk = keep // bm
    ntab = tabs[0].shape[0] // bm

    def row(i):
        return (i, 0)

    def const(i):
        return (0, 0)

    def tab(i):
        return (i % ntab, 0)

    def kept(i):
        return ((i // nbs) * nbk + jnp.maximum(i % nbs - (nbs - nbk), 0), 0)

    def rows(width):
        return pl.BlockSpec((bm, width), row)

    def full(a):
        return pl.BlockSpec(a.shape, const, pipeline_mode=pl.Buffered(1))

    fl = '_flash' if transposed else ''
    gains = [lw['gaq' + fl], lw['gak'], lw['gbqa'], lw['gbkv'], lw['gbqn' + fl], lw['gbqr' + fl],
             lw['gcq' + fl], lw['gck']]
    in_specs = ([rows(d_model), full(lw['g_attn']), full(lw['w_a']), full(lw['w_qb'])]
                + [full(g) for g in gains]
                + [pl.BlockSpec((bm, LANE), tab) for _ in tabs])
    out_shape = [
        jax.ShapeDtypeStruct((n, A_W), BF16),
        jax.ShapeDtypeStruct((n, A_W), F32),
        jax.ShapeDtypeStruct((n, A_W), BF16),
        jax.ShapeDtypeStruct((n, AV_W), F32),
        jax.ShapeDtypeStruct((n, AV_W), BF16),
        jax.ShapeDtypeStruct((n, B_HEADS * B_PAD), BF16),
        jax.ShapeDtypeStruct((n, B_KV_LORA), F32),
        jax.ShapeDtypeStruct((n, B_ROPE), F32),
        jax.ShapeDtypeStruct((n, C_W), BF16),
        jax.ShapeDtypeStruct((n, C_W), BF16),
        jax.ShapeDtypeStruct((n, C_W), BF16),
        jax.ShapeDtypeStruct((n // seg_rows * keep, C_W), F32),
        jax.ShapeDtypeStruct((n // seg_rows * keep, C_W), F32),
    ]
    out_specs = [rows(s.shape[1]) for s in out_shape[:11]] + [pl.BlockSpec((bm, C_W), kept)] * 2
    if transposed:
        for k, width in ((0, A_W), (4, A_HEADS * VT_ROWS), (5, B_HEADS * B_PAD), (8, C_W),
                         (10, C_HEADS * VT_ROWS)):
            out_shape[k] = jax.ShapeDtypeStruct((width, n), BF16)
            out_specs[k] = pl.BlockSpec((width, bm), lambda i: (0, i))
    args = [x, lw['g_attn'], lw['w_a'], lw['w_qb'], *gains, *tabs]
    aliases = {}
    if stack is not None:
        layer, depth, buffers = stack
        for k in STACKED_OUTS:
            width = out_shape[k].shape[1]
            split = width // LANE if k in INTERLEAVED_OUTS else 1
            out_shape[k] = jax.ShapeDtypeStruct((depth * n * split, width // split), F32)
            out_specs[k] = pl.BlockSpec((bm * split, width // split), lambda i: (layer * nb + i, 0))
        if buffers is not None:
            for pos, k in enumerate(STACKED_OUTS):
                aliases[len(args)] = k
                args.append(buffers[pos])
                in_specs.append(pl.BlockSpec(memory_space=pl.ANY))
    return pl.pallas_call(
        functools.partial(_in_proj_kernel, transposed=transposed, n_alias=len(aliases),
                          interleaved=stack is not None),
        grid=(nb,),
        in_specs=in_specs,
        out_specs=out_specs,
        out_shape=out_shape,
        input_output_aliases=aliases,
        compiler_params=_cparams(("arbitrary",)),
        name="in_proj",
    )(*args)


def _mla_kv_kernel(ckv_ref, kpe_ref, w_ref, gn_ref, gr_ref, kb_ref, vb_ref, *, transposed):
    ckv = ckv_ref[...]
    kv = _dot(ckv.astype(BF16), w_ref[...])
    kpe = kpe_ref[...]
    kpe2 = jnp.concatenate([kpe, kpe], axis=1)
    ss_pe = jnp.sum(kpe * kpe, axis=-1, keepdims=True)
    lane_lo = lax.broadcasted_iota(jnp.int32, (1, LANE), 1) < 64
    nope_w = B_HEADS * B_NOPE
    for hh in range(B_HEADS):
        nope = kv[:, hh * B_NOPE:(hh + 1) * B_NOPE]
        ss = jnp.sum(nope * nope, axis=-1, keepdims=True) + ss_pe
        inv = lax.rsqrt(ss * (1.0 / B_QK) + EPS)
        in_half = lane_lo if hh % 2 == 0 else jnp.logical_not(lane_lo)
        kb_ref[:, hh * B_PAD:hh * B_PAD + LANE] = (nope * inv * gn_ref[...]).astype(BF16)
        kb_ref[:, hh * B_PAD + LANE:(hh + 1) * B_PAD] = jnp.where(in_half, kpe2 * inv * gr_ref[...], 0.0).astype(BF16)
    if transposed:
        for c in range(B_HEADS):
            _put_values_t(vb_ref, c, kv[:, nope_w + c * LANE:nope_w + (c + 1) * LANE])
    else:
        vb_ref[...] = kv[:, nope_w:].astype(BF16)


def _mla_kv(ckv, kpe, lw, *, bm, n=None, row0=0, transposed=False):
    n = ckv.shape[0] if n is None else n
    blk0 = row0 // bm
    assert row0 % bm == 0
    if transposed:
        vb_shape, vb_spec = (B_HEADS * VT_ROWS, n), pl.BlockSpec((B_HEADS * VT_ROWS, bm), lambda i: (0, i))
    else:
        vb_shape, vb_spec = (n, BV_W), pl.BlockSpec((bm, BV_W), lambda i: (i, 0))

    def row(i):
        return (i, 0)

    def src_row(i):
        return (blk0 + i, 0)

    def const(i):
        return (0, 0)

    return pl.pallas_call(
        functools.partial(_mla_kv_kernel, transposed=transposed),
        grid=(n // bm,),
        in_specs=[pl.BlockSpec((bm, B_KV_LORA), src_row), pl.BlockSpec((bm, B_ROPE), src_row),
                  pl.BlockSpec(lw['w_kvb'].shape, const), pl.BlockSpec((1, LANE), const),
                  pl.BlockSpec((1, LANE), const)],
        out_specs=[pl.BlockSpec((bm, B_HEADS * B_PAD), row), vb_spec],
        out_shape=[jax.ShapeDtypeStruct((n, B_HEADS * B_PAD), BF16), jax.ShapeDtypeStruct(vb_shape, BF16)],
        compiler_params=_cparams(("arbitrary",)),
        name="mla_kv",
    )(ckv, kpe, lw['w_kvb'], lw['gbkn'], lw['gbkr'])


def _split_components(q):
    lane_lo = lax.broadcasted_iota(jnp.int32, (1, LANE), 1) < 64
    zero = jnp.zeros_like(q)
    return jnp.concatenate([jnp.where(lane_lo, q, zero), jnp.where(lane_lo, zero, q)], axis=0)


def _diff_out(o, rows, lam_ref, sg_ref):
    o = o[:rows] - lam_ref[...] * o[rows:]
    return _rms(o, A_V_DIM) * sg_ref[...]


def _flash_queries(qt_ref, diff):
    qt = qt_ref[...]
    if diff:
        row_lo = lax.broadcasted_iota(jnp.int32, (qt.shape[0], 1), 0) < A_QK_DIM
        zero = jnp.zeros_like(qt)
        qt = jnp.concatenate([jnp.where(row_lo, qt, zero), jnp.where(row_lo, zero, qt)], axis=1)
    return qt


def _flash_finish(acc, o_ref, lam_ref, sg_ref, tq, diff):
    ot = acc[:A_V_DIM] / acc[A_V_DIM:A_V_DIM + 1]
    if diff:
        o = (ot[:, :tq] - lam_ref[:, :1] * ot[:, tq:]).T
        o = _rms(o, A_V_DIM) * sg_ref[...]
    else:
        o = ot.T
    o_ref[...] = o.astype(o_ref.dtype)


def _flash_plain_kernel(*refs, tq, tk, diff, group):
    if diff:
        qt_ref, k_ref, vt_ref, mask_ref, lam_ref, sg_ref, o_ref, acc_sc = refs
    else:
        qt_ref, k_ref, vt_ref, mask_ref, o_ref, acc_sc = refs
        lam_ref = sg_ref = None
    i = pl.program_id(2)
    qt = _flash_queries(qt_ref, diff)
    per_q = tq // tk

    mq = qt.shape[1]

    def block(j, diag):
        start = pl.multiple_of(j * tk, tk)
        if diag is None:
            pt = jnp.exp2(_dot(k_ref[pl.ds(start, tk), :], qt))
            return _dot(vt_ref[:, pl.ds(start, tk)], pt.astype(BF16))
        spans = [(c0 + diag * tk, c0 + tq) for c0 in range(0, mq, tq)]

        def cols(x):
            return x if diag == 0 else jnp.concatenate([x[:, a:b] for a, b in spans], axis=1)

        pt = jnp.exp2(_dot(k_ref[pl.ds(start, tk), :], cols(qt)))
        pt = jnp.where(cols(mask_ref[diag]) > 0, pt, 0.0)
        pv = _dot(vt_ref[:, pl.ds(start, tk)], pt.astype(BF16))
        if diag == 0:
            return pv
        pieces, off = [], 0
        for a, b in spans:
            pieces += [jnp.zeros((pv.shape[0], diag * tk), F32), pv[:, off:off + b - a]]
            off += b - a
        return jnp.concatenate(pieces, axis=1)

    acc_sc[...] = jnp.zeros(acc_sc.shape, F32)

    def add_blocks(j0, nblocks):
        acc = acc_sc[...]
        for jj in range(nblocks):
            acc = acc + block(j0 + jj, None)
        acc_sc[...] = acc

    def trips(jb, carry):
        add_blocks(jb * (group * per_q), group * per_q)
        return carry

    def leftover(r, carry):
        add_blocks((i // group * group + r) * per_q, per_q)
        return carry

    lax.fori_loop(0, i // group, trips, 0)
    if group > 1:
        lax.fori_loop(0, i % group, leftover, 0)
    acc = acc_sc[...]
    for jj in range(per_q):
        acc = acc + block(i * per_q + jj, jj)
    _flash_finish(acc, o_ref, lam_ref, sg_ref, tq, diff)


def _flash_kernel(*refs, tq, tk, diff):
    if diff:
        qt_ref, k_ref, vt_ref, mask_ref, lam_ref, sg_ref, o_ref, m_sc, a_sc, acc_sc, s_sc, p_sc = refs
    else:
        qt_ref, k_ref, vt_ref, mask_ref, o_ref, m_sc, a_sc, acc_sc, s_sc, p_sc = refs
        lam_ref = sg_ref = None
    i = pl.program_id(2)
    qt = _flash_queries(qt_ref, diff)
    per_q = tq // tk

    def scores(j):
        start = pl.multiple_of(j * tk, tk)
        return _dot(k_ref[pl.ds(start, tk), :], qt)

    def softmax(st, m_prev, diag):
        if diag is not None:
            st = jnp.where(mask_ref[diag] > 0, st, NEG)
        m_new = jnp.maximum(m_prev, jnp.max(st, axis=0, keepdims=True))
        return m_new, jnp.exp2(m_prev - m_new), jnp.exp2(st - m_new).astype(BF16)

    def weighted_values(j, acc, alpha, pt):
        start = pl.multiple_of(j * tk, tk)
        return alpha * acc + _dot(vt_ref[:, pl.ds(start, tk)], pt)

    m_sc[...] = jnp.full(m_sc.shape, NEG, F32)
    a_sc[...] = jnp.ones(a_sc.shape, F32)
    acc_sc[...] = jnp.zeros(acc_sc.shape, F32)
    p_sc[...] = jnp.zeros(p_sc.shape, BF16)
    s_sc[...] = scores(0)

    def body(jb, carry):
        m, alpha_prev, acc, p_prev, st = m_sc[...], a_sc[...], acc_sc[...], p_sc[...], s_sc[...]
        for jj in range(per_q):
            j = jb * per_q + jj
            st_next = scores(j + 1)
            m, alpha, pt = softmax(st, m, None)
            acc = weighted_values(jnp.maximum(j - 1, 0), acc, alpha_prev, p_prev)
            alpha_prev, p_prev, st = alpha, pt, st_next
        m_sc[...], a_sc[...], acc_sc[...], p_sc[...], s_sc[...] = m, alpha_prev, acc, p_prev, st
        return carry

    lax.fori_loop(0, i, body, 0)
    m, alpha_prev, acc, p_prev, st = m_sc[...], a_sc[...], acc_sc[...], p_sc[...], s_sc[...]
    for jj in range(per_q):
        j = i * per_q + jj
        st_next = scores(j + 1) if jj + 1 < per_q else None
        m, alpha, pt = softmax(st, m, jj)
        acc = weighted_values(jnp.maximum(j - 1, 0), acc, alpha_prev, p_prev)
        alpha_prev, p_prev, st = alpha, pt, st_next
    acc = weighted_values(i * per_q + per_q - 1, acc, alpha_prev, p_prev)
    _flash_finish(acc, o_ref, lam_ref, sg_ref, tq, diff)


def _flash(qt, k, vt, *, batch, heads, dk, tq, tk, diff, plain, group=1, lam=None, sg=None):
    n = k.shape[0]
    seq = n // batch
    nq = seq // tq
    mq = 2 * tq if diff else tq
    per_q = tq // tk
    kpos = jnp.arange(tq, dtype=jnp.int32).reshape(per_q, tk, 1)
    qpos = jnp.arange(mq, dtype=jnp.int32).reshape(1, 1, mq) % tq
    mask = ((kpos // CHUNK) <= (qpos // CHUNK)).astype(F32)
    in_specs = [pl.BlockSpec((dk, tq), lambda b, h, i: (h, b * nq + i)),
                pl.BlockSpec((seq, dk), lambda b, h, i: (b, h)),
                pl.BlockSpec((VT_ROWS, seq), lambda b, h, i: (h, b)),
                pl.BlockSpec((per_q, tk, mq), lambda b, h, i: (0, 0, 0), pipeline_mode=pl.Buffered(1))]
    args = [qt, k, vt, mask]
    if diff:
        in_specs += [pl.BlockSpec((1, LANE), lambda b, h, i: (0, 0))] * 2
        args += [lam, sg]
    scratch = [pltpu.VMEM((VT_ROWS, mq), F32)]
    if not plain:
        scratch = [pltpu.VMEM((1, mq), F32), pltpu.VMEM((1, mq), F32)] + scratch + [
            pltpu.VMEM((tk, mq), F32), pltpu.VMEM((tk, mq), BF16)]
    return pl.pallas_call(
        (functools.partial(_flash_plain_kernel, tq=tq, tk=tk, diff=diff, group=group) if plain
         else functools.partial(_flash_kernel, tq=tq, tk=tk, diff=diff)),
        grid=(batch, heads, nq),
        in_specs=in_specs,
        out_specs=pl.BlockSpec((tq, LANE), lambda b, h, i: (b * nq + i, h)),
        out_shape=jax.ShapeDtypeStruct((n, heads * LANE), BF16),
        scratch_shapes=scratch,
        compiler_params=_cparams(("parallel", "parallel", "arbitrary")),
        name=("flash_diff" if diff else "flash_mla") + ("_plain" if plain else ""),
    )(*args)


def _band_kernel(qt_ref, k_ref, vt_ref, bias_ref, o_ref, *, tq, nrel):
    i = pl.program_id(2)
    qt = qt_ref[...]

    def attend(rels):
        def keys(rel):
            return pl.ds(pl.multiple_of((i - rel) * tq, tq), tq)

        st = [_dot(k_ref[keys(rel), :], qt) + bias_ref[0, rel] for rel in rels]
        m = functools.reduce(jnp.maximum, [jnp.max(s, axis=0, keepdims=True) for s in st])
        acc = sum(_dot(vt_ref[:, keys(rel)], jnp.exp2(s - m).astype(BF16)) for rel, s in zip(rels, st))
        o_ref[...] = (acc[:C_DIM] / acc[C_DIM:C_DIM + 1]).T.astype(o_ref.dtype)

    for avail in range(nrel):
        cond = (i == avail) if avail < nrel - 1 else (i >= avail)
        pl.when(cond)(functools.partial(attend, tuple(range(avail + 1))))


def _band_bias_t(table, tq, nrel):
    bias = jnp.swapaxes(_band_bias(table, tq, nrel), 2, 3) * LOG2E
    k_ch = jnp.arange(tq).reshape(1, tq, 1) // CHUNK
    q_ch = (jnp.arange(nrel).reshape(nrel, 1, 1) * tq + jnp.arange(tq).reshape(1, 1, tq)) // CHUNK
    visible = (k_ch <= q_ch) & (k_ch >= q_ch - C_PAST_CHUNKS)
    return jnp.where(visible[None], bias, NEG)


def _band_bias(table, tq, nrel):
    heads = table.shape[0]
    c_all = nrel * tq
    n = tq + c_all - 1
    n_lo = tq - 1 - REL_CLIP
    n_hi = n - n_lo - (2 * REL_CLIP + 1)
    assert n_hi >= 0
    u = jnp.concatenate([jnp.broadcast_to(table[:, :1], (heads, max(n_lo, 0))), table[:, max(-n_lo, 0):],
                         jnp.broadcast_to(table[:, -1:], (heads, n_hi))], axis=1)
    w = jnp.concatenate([u[:, ::-1], jnp.zeros((heads, 1), table.dtype)], axis=1)
    m = jnp.tile(w, (1, tq))[:, :tq * n].reshape(heads, tq, n)
    t = m[:, :, tq - 1:tq - 1 + c_all]
    return t.reshape(heads, tq, nrel, tq).transpose(0, 2, 1, 3)[:, ::-1]


def _band(qt, k, vt, bias, *, batch, tq):
    n = k.shape[0]
    seq = n // batch
    nq = seq // tq
    nrel = bias.shape[1]
    return pl.pallas_call(
        functools.partial(_band_kernel, tq=tq, nrel=nrel),
        grid=(batch, C_HEADS, nq),
        in_specs=[pl.BlockSpec((LANE, tq), lambda b, h, i: (h, b * nq + i)),
                  pl.BlockSpec((seq, LANE), lambda b, h, i: (b, h)),
                  pl.BlockSpec((VT_ROWS, seq), lambda b, h, i: (h, b)),
                  pl.BlockSpec((1, nrel, tq, tq), lambda b, h, i: (h, 0, 0, 0))],
        out_specs=pl.BlockSpec((tq, LANE), lambda b, h, i: (b * nq + i, h)),
        out_shape=jax.ShapeDtypeStruct((n, C_W), BF16),
        compiler_params=_cparams(("parallel", "parallel", "arbitrary")),
        name="band",
    )(qt, k, vt, bias)


def _decode_kernel(*refs, heads, dk, diff, biased, interleaved):
    refs = list(refs)
    q_ref, kc_ref, vc_ref, kn_ref, vn_ref = refs[:5]
    rest = refs[5:]
    if biased:
        bc_ref, bn_ref, mc_ref, mn_ref = rest[:4]
        rest = rest[4:]
    if diff:
        lam_ref, sg_ref = rest[:2]
        rest = rest[2:]
    o_ref = rest[0]
    rows = q_ref.shape[0]
    for h in range(heads):
        qk = slice(h * dk, (h + 1) * dk)
        vv = slice(h * LANE, (h + 1) * LANE)
        q = q_ref[:, qk]
        if diff:
            q = _split_components(q)
        if interleaved:
            past = kc_ref.shape[1] // heads
            k_cache = kc_ref[0, pl.ds(h, past, stride=heads), :]
            v_cache = vc_ref[0, pl.ds(h, past, stride=heads), :]
        else:
            k_cache, v_cache = kc_ref[0, :, qk], vc_ref[0, :, vv]
        s_c = _dot_nt(q, k_cache.astype(BF16))
        s_n = _dot_nt(q, kn_ref[:, qk])
        if biased:
            s_c = jnp.where(mc_ref[...] > 0, s_c + bc_ref[h], NEG)
            s_n = jnp.where(mn_ref[...] > 0, s_n + bn_ref[h], NEG)
        m = jnp.maximum(jnp.max(s_c, axis=-1, keepdims=True), jnp.max(s_n, axis=-1, keepdims=True))
        p_c = jnp.exp(s_c - m)
        p_n = jnp.exp(s_n - m)
        l = jnp.sum(p_c, axis=-1, keepdims=True) + jnp.sum(p_n, axis=-1, keepdims=True)
        o = (_dot(p_c.astype(BF16), v_cache.astype(BF16)) + _dot(p_n.astype(BF16), vn_ref[:, vv])) / l
        if diff:
            o = _diff_out(o, rows, lam_ref, sg_ref)
        o_ref[:, vv] = o.astype(o_ref.dtype)


def _decode(q, kc, vc, kn, vn, *, heads, dk, rows, stream0=0, interleaved=False, diff=False, lam=None, sg=None,
            bias=None):
    n = q.shape[0]
    streams = n // rows
    past = kc.shape[1] // heads if interleaved else kc.shape[1]
    assert not interleaved or dk == LANE
    in_specs = [pl.BlockSpec((rows, heads * dk), lambda b: (b, 0)),
                pl.BlockSpec((1,) + kc.shape[1:], lambda b: (stream0 + b, 0, 0)),
                pl.BlockSpec((1,) + vc.shape[1:], lambda b: (stream0 + b, 0, 0)),
                pl.BlockSpec((rows, heads * dk), lambda b: (b, 0)),
                pl.BlockSpec((rows, heads * LANE), lambda b: (b, 0))]
    args = [q, kc, vc, kn, vn]
    if bias is not None:
        bias_c, bias_n, mask_c, mask_n = bias
        in_specs += [pl.BlockSpec((heads, rows, past), lambda b: (0, 0, 0)),
                     pl.BlockSpec((heads, rows, rows), lambda b: (0, 0, 0)),
                     pl.BlockSpec((rows, past), lambda b: (0, 0)),
                     pl.BlockSpec((rows, rows), lambda b: (0, 0))]
        args += [bias_c, bias_n, mask_c, mask_n]
    if diff:
        in_specs += [pl.BlockSpec((1, LANE), lambda b: (0, 0))] * 2
        args += [lam, sg]
    return pl.pallas_call(
        functools.partial(_decode_kernel, heads=heads, dk=dk, diff=diff, biased=bias is not None,
                          interleaved=interleaved),
        grid=(streams,),
        in_specs=in_specs,
        out_specs=pl.BlockSpec((rows, heads * LANE), lambda b: (b, 0)),
        out_shape=jax.ShapeDtypeStruct((n, heads * LANE), BF16),
        compiler_params=_cparams(("parallel",)),
        name="decode",
    )(*args)


def _merge_kernel(x_ref, g_ref, oa_ref, ob_ref, oc_ref, wga_ref, wgb_ref, wgc_ref, bga_ref, bgb_ref, bgc_ref,
                  wba_ref, wbb_ref, wbc_ref, wo_ref, out_ref, h_sc):
    j = pl.program_id(1)

    @pl.when(j == 0)
    def _():
        x = x_ref[...]
        h_sc[...] = (_rms(x, x.shape[1]) * g_ref[...]).astype(BF16)
        out_ref[...] = x

    h = h_sc[...]

    def branch(o_ref, wg_ref, bg_ref, wb_ref):
        gate = jax.nn.sigmoid(_dot(h, wg_ref[...]) + bg_ref[...])
        return gate * _dot(o_ref[...], wb_ref[...])

    merged = (branch(oa_ref, wga_ref, bga_ref, wba_ref) + branch(ob_ref, wgb_ref, bgb_ref, wbb_ref)
              + branch(oc_ref, wgc_ref, bgc_ref, wbc_ref))
    out_ref[...] += _dot(merged.astype(BF16), wo_ref[...])


def _merge(x, oa, ob, oc, lw, *, bm, bn):
    n, d_model = x.shape
    nbn = d_model // bn

    def row(i, j):
        return (i, 0)

    def const(i, j):
        return (0, 0)

    def col(k):
        return lambda i, j: (0, k * nbn + j)

    layer = lw['layer']

    def wcol(rows, k):
        return pl.BlockSpec((None, rows, bn), lambda i, j: (layer, 0, k * nbn + j))

    in_specs = [pl.BlockSpec((bm, d_model), row), pl.BlockSpec((1, d_model), const),
                pl.BlockSpec((bm, AV_W), row), pl.BlockSpec((bm, BV_W), row), pl.BlockSpec((bm, C_W), row),
                wcol(d_model, 0), wcol(d_model, 1), wcol(d_model, 2),
                pl.BlockSpec((1, bn), col(0)), pl.BlockSpec((1, bn), col(1)), pl.BlockSpec((1, bn), col(2)),
                wcol(AV_W, 0), wcol(BV_W, 0), wcol(C_W, 0),
                pl.BlockSpec((None, bn, d_model), lambda i, j: (layer, j, 0))]
    return pl.pallas_call(
        _merge_kernel,
        grid=(n // bm, nbn),
        in_specs=in_specs,
        out_specs=pl.BlockSpec((bm, d_model), row),
        out_shape=jax.ShapeDtypeStruct((n, d_model), F32),
        scratch_shapes=[pltpu.VMEM((bm, d_model), BF16)],
        compiler_params=_cparams(("parallel", "arbitrary")),
        name="merge",
    )(x, lw['g_attn'], oa, ob, oc, lw['w_g'], lw['w_g'], lw['w_g'], lw['b_g'], lw['b_g'], lw['b_g'],
      lw['w_br_a'], lw['w_br_b'], lw['w_br_c'], lw['w_o'])


def _mlp_kernel(x_ref, g_ref, wu_ref, wd_ref, out_ref, h_sc):
    j = pl.program_id(1)

    @pl.when(j == 0)
    def _():
        x = x_ref[...]
        h_sc[...] = (_rms(x, x.shape[1]) * g_ref[...]).astype(BF16)
        out_ref[...] = x

    u = jnp.maximum(_dot(h_sc[...], wu_ref[...]), 0.0)
    out_ref[...] += _dot((u * u).astype(BF16), wd_ref[...])


def _mlp(x, lw, *, bm, bf):
    n, d_model = x.shape
    d_ff = lw['w_up'].shape[2]
    layer = lw['layer']
    return pl.pallas_call(
        _mlp_kernel,
        grid=(n // bm, d_ff // bf),
        in_specs=[pl.BlockSpec((bm, d_model), lambda i, j: (i, 0)), pl.BlockSpec((1, d_model), lambda i, j: (0, 0)),
                  pl.BlockSpec((None, d_model, bf), lambda i, j: (layer, 0, j)),
                  pl.BlockSpec((None, bf, d_model), lambda i, j: (layer, j, 0))],
        out_specs=pl.BlockSpec((bm, d_model), lambda i, j: (i, 0)),
        out_shape=jax.ShapeDtypeStruct((n, d_model), F32),
        scratch_shapes=[pltpu.VMEM((bm, d_model), BF16)],
        compiler_params=_cparams(("parallel", "arbitrary")),
        name="mlp",
    )(x, lw['g_mlp'], lw['w_up'], lw['w_down'])


def _rope_tables(pos, rot):
    half = rot // 2
    inv = jnp.power(ROPE_THETA, -jnp.arange(half, dtype=F32) / half)
    ang = pos.astype(F32)[:, None] * inv[None, :]
    cos, sin = jnp.cos(ang), jnp.sin(ang)
    npos = pos.shape[0]
    one = jnp.ones((npos, 64 - rot), F32)
    c64 = jnp.concatenate([cos, cos, one], axis=1)
    s1 = jnp.concatenate([-sin, jnp.zeros((npos, 64 - half), F32)], axis=1)
    s2 = jnp.concatenate([jnp.zeros((npos, half), F32), sin, jnp.zeros((npos, 64 - rot), F32)], axis=1)
    return [jnp.tile(t, (1, 2)) for t in (c64, s1, s2)]


def _score_bound(gq, gk, d):
    return math.sqrt(d) * LOG2E * jnp.max(jnp.abs(gq)) * jnp.max(jnp.abs(gk))


def _layer_weights(l, p, stacks):
    d_model = p['w_in'].shape[1]
    w_in = p['w_in'][l]
    o = [0]
    for width in (A_W, A_W, AV_W, B_Q_LORA, B_KV_LORA + B_ROPE, C_W, C_W, C_W):
        o.append(o[-1] + width)
    cols = lambda a, b: w_in[:, a:b]
    w_a = jnp.concatenate([
        cols(o[0], o[1]), cols(o[1], o[2]), cols(o[2], o[3]), cols(o[3], o[4]),
        cols(o[4], o[4] + B_KV_LORA), cols(o[5], o[6]), cols(o[6], o[7]), cols(o[7], o[8]),
        cols(o[4] + B_KV_LORA, o[5]), jnp.zeros((d_model, LANE - B_ROPE), F32)], axis=1).astype(BF16)
    w_qb = p['w_qb'][l].reshape(B_Q_LORA, B_HEADS, B_QK)
    w_qb = jnp.concatenate([w_qb[:, :, :B_NOPE].reshape(B_Q_LORA, -1),
                            w_qb[:, :, B_NOPE:].reshape(B_Q_LORA, -1)], axis=1).astype(BF16)
    w_kvb = p['w_kvb'][l].reshape(B_KV_LORA, B_HEADS, B_NOPE + B_V_DIM)
    w_kvb = jnp.concatenate([w_kvb[:, :, :B_NOPE].reshape(B_KV_LORA, -1),
                             w_kvb[:, :, B_NOPE:].reshape(B_KV_LORA, -1)], axis=1).astype(BF16)
    lam_init = 0.8 - 0.6 * math.exp(-0.3 * l)
    lf = p['a_lambda'][l].astype(F32)
    lam = jnp.exp(jnp.sum(lf[0] * lf[1])) - jnp.exp(jnp.sum(lf[2] * lf[3])) + lam_init

    def tiled(g, reps, scale=1.0):
        return jnp.tile(g * scale, reps)[None, :]

    bqn = p['b_q_norm'][l]
    bkn = p['b_k_norm'][l]
    return {
        'g_attn': p['attn_norm'][l][None, :],
        'g_mlp': p['mlp_norm'][l][None, :],
        'w_a': w_a,
        'b_g': p['b_gate'][l][None, :],
        'w_qb': w_qb,
        'w_kvb': w_kvb,
        'gaq': tiled(p['a_q_norm'][l], 2 * A_HEADS, A_QK_DIM ** -0.5),
        'gak': tiled(p['a_k_norm'][l], 2 * A_HEADS),
        'gbqa': p['b_qa_norm'][l][None, :],
        'gbkv': p['b_kv_norm'][l][None, :],
        'gbqn': tiled(bqn[:B_NOPE], 1, B_QK ** -0.5),
        'gbqr': tiled(bqn[B_NOPE:], 2, B_QK ** -0.5),
        'gaq_flash': tiled(p['a_q_norm'][l], 2 * A_HEADS, A_QK_DIM ** -0.5 * LOG2E),
        'gbqn_flash': tiled(bqn[:B_NOPE], 1, B_QK ** -0.5 * LOG2E),
        'gbqr_flash': tiled(bqn[B_NOPE:], 2, B_QK ** -0.5 * LOG2E),
        'gcq_flash': tiled(p['c_q_norm'][l], C_HEADS, C_DIM ** -0.5 * LOG2E),
        'gbkn': tiled(bkn[:B_NOPE], 1),
        'gbkr': tiled(bkn[B_NOPE:], 2),
        'gcq': tiled(p['c_q_norm'][l], C_HEADS, C_DIM ** -0.5),
        'gck': tiled(p['c_k_norm'][l], C_HEADS),
        'lam': jnp.broadcast_to(lam, (1, LANE)).astype(F32),
        'sg': (p['a_sub_norm'][l] * (1.0 - lam_init))[None, :],
        'bound_a': _score_bound(p['a_q_norm'][l], p['a_k_norm'][l], A_QK_DIM),
        'bound_b': _score_bound(bqn, bkn, B_QK),
        'layer': l,
        **stacks,
    }


def _stacked_weights(p):
    gate_off = p['w_in'].shape[2] - 3 * p['w_in'].shape[1]
    stacks = {name: p[name].astype(BF16) for name in ('w_br_a', 'w_br_b', 'w_br_c', 'w_o', 'w_up', 'w_down')}
    stacks['w_g'] = p['w_in'][:, :, gate_off:].astype(BF16)
    return stacks


def _pick(n, prefs):
    for b in prefs:
        if n % b == 0:
            return b
    raise ValueError(f"no block size in {prefs} divides {n}")


def kernel(x_prompt, x_sample, cache_a_k, cache_a_v, cache_b_ckv, cache_b_kpe, cache_c_k, cache_c_v,
           attn_norm, w_in, b_gate, a_q_norm, a_k_norm, a_lambda, a_sub_norm,
           b_qa_norm, b_kv_norm, w_qb, w_kvb, b_q_norm, b_k_norm,
           c_q_norm, c_k_norm, c_rel_bias, w_br_a, w_br_b, w_br_c, w_o,
           mlp_norm, w_up, w_down):
    p = dict(attn_norm=attn_norm, w_in=w_in, b_gate=b_gate, a_q_norm=a_q_norm, a_k_norm=a_k_norm,
             a_lambda=a_lambda, a_sub_norm=a_sub_norm, b_qa_norm=b_qa_norm, b_kv_norm=b_kv_norm,
             w_qb=w_qb, w_kvb=w_kvb, b_q_norm=b_q_norm, b_k_norm=b_k_norm, c_q_norm=c_q_norm,
             c_k_norm=c_k_norm, w_br_a=w_br_a, w_br_b=w_br_b, w_br_c=w_br_c, w_o=w_o, mlp_norm=mlp_norm,
             w_up=w_up, w_down=w_down)
    depth = w_in.shape[0]
    stacks = _stacked_weights(p)
    batch, seq, d_model = x_prompt.shape
    streams, dec, _ = x_sample.shape
    past = cache_a_k.shape[2]
    c_past = cache_c_k.shape[2]
    c_keep = min(C_WINDOW, seq)
    n_p, n_s = batch * seq, streams * dec
    assert seq % CHUNK == 0 and c_keep % CHUNK == 0

    pos_p = jnp.arange(seq)
    pos_s = past + jnp.arange(dec)
    tabs_p = _rope_tables(pos_p, A_ROT) + _rope_tables(pos_p, B_ROPE)
    tabs_s = [jnp.tile(t, (streams, 1)) for t in _rope_tables(pos_s, A_ROT) + _rope_tables(pos_s, B_ROPE)]

    kc_pos = jnp.concatenate([past - c_past + jnp.arange(c_past), pos_s])
    q_ch, k_ch = pos_s // CHUNK, kc_pos // CHUNK
    c_mask = ((k_ch[None, :] <= q_ch[:, None]) & (k_ch[None, :] >= q_ch[:, None] - C_PAST_CHUNKS)).astype(F32)
    rel_s = jnp.clip(pos_s[:, None] - kc_pos[None, :], -REL_CLIP, REL_CLIP) + REL_CLIP

    bm_in = _pick(seq, (256, 128, 64))
    bm_tok = _pick(seq, (512, 256, 128, 64))
    t_a = _pick(seq, (512, 256, 128))
    t_b = _pick(seq, (1024, 512, 256, 128))
    tk_a, tk_b = t_a, t_b // 2
    tp_a = (t_b, t_b // 2)
    tp_b = (t_b, t_b // 2)
    t_c = _pick(seq, (512, 256, 128))
    nrel_c = C_WINDOW // t_c + 1
    bm_wide = _pick(seq, (1024, 512, 256, 128, 64))
    bn_merge = 512
    bf_mlp = 512
    bm_cache = _pick(streams * past, (512, 256, 128, 64, 16))

    ca_k = cache_a_k.reshape(depth * streams, past * A_HEADS, LANE)
    ca_v = cache_a_v.reshape(depth * streams, past * A_HEADS, LANE)
    cb_ckv = cache_b_ckv.reshape(depth * streams * past, B_KV_LORA)
    cb_kpe = cache_b_kpe.reshape(depth * streams * past, B_ROPE)
    cc_k = cache_c_k.reshape(depth * streams, c_past * C_HEADS, LANE)
    cc_v = cache_c_v.reshape(depth * streams, c_past * C_HEADS, LANE)

    xp = x_prompt.reshape(n_p, d_model)
    xs = x_sample.reshape(n_s, d_model)
    outs = [[] for _ in range(12)]
    stacked = None
    for l in range(depth):
        lw = _layer_weights(l, p, stacks)

        (qa, kaf, kab, vaf, vab, qb, ckv, kpe, qc, kcb, vcb, kcf, vcf) = _in_proj(
            xp, lw, tabs_p, bm=bm_in, seg_rows=seq, keep=c_keep, transposed=True, stack=(l, depth, stacked))
        stacked = (kaf, vaf, ckv, kpe)
        kb, vb = _mla_kv(ckv, kpe, lw, bm=bm_tok, n=n_p, row0=l * n_p, transposed=True)
        flash_a = functools.partial(_flash, qa, kab, vab, batch=batch, heads=A_HEADS, dk=LANE, diff=True,
                                    lam=lw['lam'], sg=lw['sg'])
        flash_b = functools.partial(_flash, qb, kb, vb, batch=batch, heads=B_HEADS, dk=B_PAD, diff=False)
        oa = lax.cond(lw['bound_a'] <= FLASH_PLAIN_MAX_LOG2,
                      functools.partial(flash_a, tq=tp_a[0], tk=tp_a[1], plain=True, group=2),
                      functools.partial(flash_a, tq=t_a, tk=tk_a, plain=False))
        ob = lax.cond(lw['bound_b'] <= FLASH_PLAIN_MAX_LOG2,
                      functools.partial(flash_b, tq=tp_b[0], tk=tp_b[1], plain=True, group=4),
                      functools.partial(flash_b, tq=t_b, tk=tk_b, plain=False))
        oc = _band(qc, kcb, vcb, _band_bias_t(c_rel_bias[l], t_c, nrel_c), batch=batch, tq=t_c)
        xp = _merge(xp, oa, ob, oc, lw, bm=bm_tok, bn=bn_merge)
        xp = _mlp(xp, lw, bm=bm_wide, bf=bf_mlp)
        for dst, val in zip(outs[4:6], (kcf, vcf)):
            dst.append(val)

        (qa, kaf, kab, vaf, vab, qb, ckv, kpe, qc, kcb, vcb, kcf, vcf) = _in_proj(
            xs, lw, tabs_s, bm=n_s, seg_rows=n_s, keep=n_s, transposed=False)
        kb, vb = _mla_kv(ckv, kpe, lw, bm=n_s)
        kb_c, vb_c = _mla_kv(cb_ckv, cb_kpe, lw, bm=bm_cache, n=streams * past, row0=l * streams * past)
        oa = _decode(qa, ca_k, ca_v, kab, vab, heads=A_HEADS, dk=LANE, rows=dec, stream0=l * streams,
                     interleaved=True, diff=True, lam=lw['lam'], sg=lw['sg'])
        ob = _decode(qb, kb_c.reshape(streams, past, B_HEADS * B_PAD), vb_c.reshape(streams, past, BV_W),
                     kb, vb, heads=B_HEADS, dk=B_PAD, rows=dec)
        bias_s = c_rel_bias[l][:, rel_s]
        oc = _decode(qc, cc_k, cc_v, kcb, vcb, heads=C_HEADS, dk=LANE, rows=dec, stream0=l * streams,
                     interleaved=True, bias=(bias_s[:, :, :c_past], bias_s[:, :, c_past:], c_mask[:, :c_past], c_mask[:, c_past:]))
        xs = _merge(xs, oa, ob, oc, lw, bm=n_s, bn=bn_merge)
        xs = _mlp(xs, lw, bm=n_s, bf=bf_mlp)
        for dst, val in zip(outs[6:], (kaf, vaf, ckv, kpe, kcf, vcf)):
            dst.append(val)

    st = list(stacked) + [jnp.stack(o) for o in outs[4:]]
    return (xp.reshape(batch, seq, d_model), xs.reshape(streams, dec, d_model),
            st[0].reshape(depth, batch, seq, A_HEADS, 2 * A_QK_DIM),
            st[1].reshape(depth, batch, seq, A_HEADS, A_V_DIM),
            st[2].reshape(depth, batch, seq, B_KV_LORA),
            st[3].reshape(depth, batch, seq, B_ROPE),
            st[4].reshape(depth, batch, c_keep, C_HEADS, C_DIM),
            st[5].reshape(depth, batch, c_keep, C_HEADS, C_DIM),
            st[6].reshape(depth, streams, dec, A_HEADS, 2 * A_QK_DIM),
            st[7].reshape(depth, streams, dec, A_HEADS, A_V_DIM),
            st[8].reshape(depth, streams, dec, B_KV_LORA),
            st[9].reshape(depth, streams, dec, B_ROPE),
            st[10].reshape(depth, streams, dec, C_HEADS, C_DIM),
            st[11].reshape(depth, streams, dec, C_HEADS, C_DIM))
```

```python
import functools
import math

import jax
import jax.numpy as jnp
from jax import lax
from jax.experimental import pallas as pl
from jax.experimental.pallas import tpu as pltpu

F32 = jnp.float32
BF16 = jnp.bfloat16

CHUNK = 64
ROPE_THETA = 500000.0
EPS = 1e-6
NEG = -1e30
LOG2E = math.log2(math.e)
FLASH_PLAIN_MAX_LOG2 = 50.0
LANE = 128

A_HEADS = 4
A_QK_DIM = 64
A_V_DIM = 128
A_ROT = A_QK_DIM // 4
B_HEADS = 8
B_Q_LORA = 512
B_KV_LORA = 256
B_NOPE = 128
B_ROPE = 64
B_V_DIM = 128
B_QK = B_NOPE + B_ROPE
B_PAD = 2 * LANE
BF16_SUBLANES = 16
VT_ROWS = LANE + BF16_SUBLANES
C_HEADS = 4
C_DIM = 128
C_PAST_CHUNKS = 8
C_WINDOW = C_PAST_CHUNKS * CHUNK
REL_CLIP = 128

A_W = A_HEADS * 2 * A_QK_DIM
AV_W = A_HEADS * A_V_DIM
BV_W = B_HEADS * B_V_DIM
C_W = C_HEADS * C_DIM

OFF_AQ = 0
OFF_AK = OFF_AQ + A_W
OFF_AV = OFF_AK + A_W
OFF_BQ = OFF_AV + AV_W
OFF_CKV = OFF_BQ + B_Q_LORA
OFF_CQ = OFF_CKV + B_KV_LORA
OFF_CK = OFF_CQ + C_W
OFF_CV = OFF_CK + C_W
OFF_KPE = OFF_CV + C_W
WA_COLS = OFF_KPE + LANE

VMEM_LIMIT = 56 * 1024 * 1024


VMEM_LIMIT_IN_PROJ = 60 * 1024 * 1024


def _cparams(sem, vmem=VMEM_LIMIT):
    return pltpu.CompilerParams(dimension_semantics=sem, vmem_limit_bytes=vmem)


def _rms(x, width):
    ms = jnp.sum(x * x, axis=-1, keepdims=True) * (1.0 / width)
    return x * lax.rsqrt(ms + EPS)


def _rms_half(zc, lane_lo):
    sq = zc * zc
    lo = jnp.sum(jnp.where(lane_lo, sq, 0.0), axis=-1, keepdims=True)
    hi = jnp.sum(jnp.where(lane_lo, 0.0, sq), axis=-1, keepdims=True)
    ms = jnp.where(lane_lo, lo, hi) * (1.0 / 64.0)
    return zc * lax.rsqrt(ms + EPS)


def _rope(y, c, s1, s2, half):
    return y * c + pltpu.roll(y, LANE - half, 1) * s1 + pltpu.roll(y, half, 1) * s2


def _put_values_t(ref, head, val):
    base = head * VT_ROWS
    ref[base:base + LANE, :] = val.T.astype(BF16)
    ref[base + LANE:base + VT_ROWS, :] = jnp.ones((VT_ROWS - LANE, val.shape[0]), BF16)


def _dot(a, b):
    return jnp.dot(a, b, preferred_element_type=F32)


def _dot_nt(a, b):
    return lax.dot_general(a, b, (((1,), (1,)), ((), ())), preferred_element_type=F32)


def _in_proj_kernel(*refs, transposed, n_alias, interleaved):
    (x_ref, g_ref, wa_ref, wqb_ref, gaq_ref, gak_ref, gbqa_ref, gbkv_ref, gbqn_ref, gbqr_ref,
     gcq_ref, gck_ref, ca_ref, s1a_ref, s2a_ref, cb_ref, s1b_ref, s2b_ref) = refs[:18]
    (qa_ref, kaf_ref, kab_ref, vaf_ref, vab_ref, qb_ref, ckv_ref, kpe_ref,
     qc_ref, kcb_ref, vcb_ref, kcf_ref, vcf_ref) = refs[18 + n_alias:]
    d_model = x_ref.shape[1]

    def put(ref, c, val):
        if transposed:
            ref[c * LANE:(c + 1) * LANE, :] = val.T.astype(BF16)
        else:
            ref[:, c * LANE:(c + 1) * LANE] = val.astype(BF16)

    def put_cache(ref, head, val):
        if interleaved:
            ref[pl.ds(head, val.shape[0], stride=A_HEADS), :] = val
        else:
            ref[:, head * LANE:(head + 1) * LANE] = val

    h = (_rms(x_ref[...], d_model) * g_ref[...]).astype(BF16)
    lane_lo = lax.broadcasted_iota(jnp.int32, (1, LANE), 1) < 64
    ca, s1a, s2a = ca_ref[...], s1a_ref[...], s2a_ref[...]
    cb, s1b, s2b = cb_ref[...], s1b_ref[...], s2b_ref[...]

    def proj(off, width):
        return _dot(h, wa_ref[:, off:off + width])

    def chunk(c):
        return slice(c * LANE, (c + 1) * LANE)

    z = proj(OFF_AQ, A_W)
    for c in range(A_W // LANE):
        y = _rms_half(z[:, chunk(c)], lane_lo) * gaq_ref[:, chunk(c)]
        put(qa_ref, c, _rope(y, ca, s1a, s2a, A_ROT // 2))
    z = proj(OFF_AK, A_W)
    for c in range(A_W // LANE):
        y = _rms_half(z[:, chunk(c)], lane_lo) * gak_ref[:, chunk(c)]
        y = _rope(y, ca, s1a, s2a, A_ROT // 2)
        put_cache(kaf_ref, c, y)
        kab_ref[:, chunk(c)] = y.astype(BF16)
    z = proj(OFF_AV, AV_W)
    for c in range(A_HEADS):
        put_cache(vaf_ref, c, z[:, chunk(c)])
        if transposed:
            _put_values_t(vab_ref, c, z[:, chunk(c)])
        else:
            put(vab_ref, c, z[:, chunk(c)])

    z = proj(OFF_BQ, B_Q_LORA)
    cq = (_rms(z, B_Q_LORA) * gbqa_ref[...]).astype(BF16)
    qraw = _dot(cq, wqb_ref[...])
    nope_w = B_HEADS * B_NOPE
    roped = [_rope(qraw[:, nope_w + c * LANE:nope_w + (c + 1) * LANE], cb, s1b, s2b, B_ROPE // 2)
             for c in range(B_HEADS * B_ROPE // LANE)]
    for hh in range(B_HEADS):
        nope = qraw[:, hh * B_NOPE:(hh + 1) * B_NOPE]
        in_half = lane_lo if hh % 2 == 0 else jnp.logical_not(lane_lo)
        rsel = jnp.where(in_half, roped[hh // 2], 0.0)
        ss = jnp.sum(nope * nope, axis=-1, keepdims=True) + jnp.sum(rsel * rsel, axis=-1, keepdims=True)
        inv = lax.rsqrt(ss * (1.0 / B_QK) + EPS)
        put(qb_ref, 2 * hh, nope * inv * gbqn_ref[...])
        put(qb_ref, 2 * hh + 1, rsel * inv * gbqr_ref[...])
    z = proj(OFF_CKV, B_KV_LORA)
    ckv_ref[...] = _rms(z, B_KV_LORA) * gbkv_ref[...]
    z = proj(OFF_KPE, LANE)
    kpe_ref[...] = _rope(z, cb, s1b, s2b, B_ROPE // 2)[:, :B_ROPE]

    z = proj(OFF_CQ, C_W)
    for c in range(C_HEADS):
        put(qc_ref, c, _rms(z[:, chunk(c)], C_DIM) * gcq_ref[:, chunk(c)])
    z = proj(OFF_CK, C_W)
    for c in range(C_HEADS):
        y = _rms(z[:, chunk(c)], C_DIM) * gck_ref[:, chunk(c)]
        kcf_ref[:, chunk(c)] = y
        kcb_ref[:, chunk(c)] = y.astype(BF16)
    z = proj(OFF_CV, C_W)
    vcf_ref[...] = z
    for c in range(C_HEADS):
        if transposed:
            _put_values_t(vcb_ref, c, z[:, chunk(c)])
        else:
            put(vcb_ref, c, z[:, chunk(c)])


STACKED_OUTS = (1, 3, 6, 7)
INTERLEAVED_OUTS = (1, 3)


def _in_proj(x, lw, tabs, *, bm, seg_rows, keep, transposed, stack=None):
    n, d_model = x.shape
    nb = n // bm
    nbs = seg_rows // bm
    nbk = keep // bm
    ntab = tabs[0].shape[0] // bm

    def row(i):
        return (i, 0)

    def const(i):
        return (0, 0)

    def tab(i):
        return (i % ntab, 0)

    def kept(i):
        return ((i // nbs) * nbk + jnp.maximum(i % nbs - (nbs - nbk), 0), 0)

    def rows(width):
        return pl.BlockSpec((bm, width), row)

    def full(a):
        return pl.BlockSpec(a.shape, const, pipeline_mode=pl.Buffered(1))

    fl = '_flash' if transposed else ''
    gains = [lw['gaq' + fl], lw['gak'], lw['gbqa'], lw['gbkv'], lw['gbqn' + fl], lw['gbqr' + fl],
             lw['gcq' + fl], lw['gck']]
    in_specs = ([rows(d_model), full(lw['g_attn']), full(lw['w_a']), full(lw['w_qb'])]
                + [full(g) for g in gains]
                + [pl.BlockSpec((bm, LANE), tab) for _ in tabs])
    out_shape = [
        jax.ShapeDtypeStruct((n, A_W), BF16),
        jax.ShapeDtypeStruct((n, A_W), F32),
        jax.ShapeDtypeStruct((n, A_W), BF16),
        jax.ShapeDtypeStruct((n, AV_W), F32),
        jax.ShapeDtypeStruct((n, AV_W), BF16),
        jax.ShapeDtypeStruct((n, B_HEADS * B_PAD), BF16),
        jax.ShapeDtypeStruct((n, B_KV_LORA), F32),
        jax.ShapeDtypeStruct((n, B_ROPE), F32),
        jax.ShapeDtypeStruct((n, C_W), BF16),
        jax.ShapeDtypeStruct((n, C_W), BF16),
        jax.ShapeDtypeStruct((n, C_W), BF16),
        jax.ShapeDtypeStruct((n // seg_rows * keep, C_W), F32),
        jax.ShapeDtypeStruct((n // seg_rows * keep, C_W), F32),
    ]
    out_specs = [rows(s.shape[1]) for s in out_shape[:11]] + [pl.BlockSpec((bm, C_W), kept)] * 2
    if transposed:
        for k, width in ((0, A_W), (4, A_HEADS * VT_ROWS), (5, B_HEADS * B_PAD), (8, C_W),
                         (10, C_HEADS * VT_ROWS)):
            out_shape[k] = jax.ShapeDtypeStruct((width, n), BF16)
            out_specs[k] = pl.BlockSpec((width, bm), lambda i: (0, i))
    args = [x, lw['g_attn'], lw['w_a'], lw['w_qb'], *gains, *tabs]
    aliases = {}
    if stack is not None:
        layer, depth, buffers = stack
        for k in STACKED_OUTS:
            width = out_shape[k].shape[1]
            split = width // LANE if k in INTERLEAVED_OUTS else 1
            out_shape[k] = jax.ShapeDtypeStruct((depth * n * split, width // split), F32)
            out_specs[k] = pl.BlockSpec((bm * split, width // split), lambda i: (layer * nb + i, 0))
        if buffers is not None:
            for pos, k in enumerate(STACKED_OUTS):
                aliases[len(args)] = k
                args.append(buffers[pos])
                in_specs.append(pl.BlockSpec(memory_space=pl.ANY))
    return pl.pallas_call(
        functools.partial(_in_proj_kernel, transposed=transposed, n_alias=len(aliases),
                          interleaved=stack is not None),
        grid=(nb,),
        in_specs=in_specs,
        out_specs=out_specs,
        out_shape=out_shape,
        input_output_aliases=aliases,
        compiler_params=_cparams(("arbitrary",), VMEM_LIMIT_IN_PROJ),
        name="in_proj",
    )(*args)


def _mla_rope_key(kpe):
    return jnp.concatenate([kpe, kpe], axis=1), jnp.sum(kpe * kpe, axis=-1, keepdims=True)


def _mla_head_key(hh, nope, kpe2, ss_pe, gn, gr):
    ss = jnp.sum(nope * nope, axis=-1, keepdims=True) + ss_pe
    inv = lax.rsqrt(ss * (1.0 / B_QK) + EPS)
    lane_lo = lax.broadcasted_iota(jnp.int32, (1, LANE), 1) < 64
    in_half = lane_lo if hh % 2 == 0 else jnp.logical_not(lane_lo)
    return (nope * inv * gn).astype(BF16), jnp.where(in_half, kpe2 * inv * gr, 0.0).astype(BF16)


def _mla_kv_kernel(ckv_ref, kpe_ref, w_ref, gn_ref, gr_ref, kb_ref, vb_ref, *, transposed):
    ckv = ckv_ref[...]
    kv = _dot(ckv.astype(BF16), w_ref[...])
    kpe2, ss_pe = _mla_rope_key(kpe_ref[...])
    nope_w = B_HEADS * B_NOPE
    for hh in range(B_HEADS):
        k_nope, k_rope = _mla_head_key(hh, kv[:, hh * B_NOPE:(hh + 1) * B_NOPE], kpe2, ss_pe,
                                       gn_ref[...], gr_ref[...])
        kb_ref[:, hh * B_PAD:hh * B_PAD + LANE] = k_nope
        kb_ref[:, hh * B_PAD + LANE:(hh + 1) * B_PAD] = k_rope
    if transposed:
        for c in range(B_HEADS):
            _put_values_t(vb_ref, c, kv[:, nope_w + c * LANE:nope_w + (c + 1) * LANE])
    else:
        vb_ref[...] = kv[:, nope_w:].astype(BF16)


def _mla_kv(ckv, kpe, lw, *, bm, n=None, row0=0, transposed=False):
    n = ckv.shape[0] if n is None else n
    blk0 = row0 // bm
    assert row0 % bm == 0
    if transposed:
        vb_shape, vb_spec = (B_HEADS * VT_ROWS, n), pl.BlockSpec((B_HEADS * VT_ROWS, bm), lambda i: (0, i))
    else:
        vb_shape, vb_spec = (n, BV_W), pl.BlockSpec((bm, BV_W), lambda i: (i, 0))

    def row(i):
        return (i, 0)

    def src_row(i):
        return (blk0 + i, 0)

    def const(i):
        return (0, 0)

    return pl.pallas_call(
        functools.partial(_mla_kv_kernel, transposed=transposed),
        grid=(n // bm,),
        in_specs=[pl.BlockSpec((bm, B_KV_LORA), src_row), pl.BlockSpec((bm, B_ROPE), src_row),
                  pl.BlockSpec(lw['w_kvb'].shape, const), pl.BlockSpec((1, LANE), const),
                  pl.BlockSpec((1, LANE), const)],
        out_specs=[pl.BlockSpec((bm, B_HEADS * B_PAD), row), vb_spec],
        out_shape=[jax.ShapeDtypeStruct((n, B_HEADS * B_PAD), BF16), jax.ShapeDtypeStruct(vb_shape, BF16)],
        compiler_params=_cparams(("arbitrary",)),
        name="mla_kv",
    )(ckv, kpe, lw['w_kvb'], lw['gbkn'], lw['gbkr'])


def _split_components(q):
    lane_lo = lax.broadcasted_iota(jnp.int32, (1, LANE), 1) < 64
    zero = jnp.zeros_like(q)
    return jnp.concatenate([jnp.where(lane_lo, q, zero), jnp.where(lane_lo, zero, q)], axis=0)


def _diff_out(o, rows, lam_ref, sg_ref):
    o = o[:rows] - lam_ref[...] * o[rows:]
    return _rms(o, A_V_DIM) * sg_ref[...]


def _flash_queries(qt_ref, diff):
    qt = qt_ref[...]
    if diff:
        row_lo = lax.broadcasted_iota(jnp.int32, (qt.shape[0], 1), 0) < A_QK_DIM
        zero = jnp.zeros_like(qt)
        qt = jnp.concatenate([jnp.where(row_lo, qt, zero), jnp.where(row_lo, zero, qt)], axis=1)
    return qt


def _flash_finish(acc, o_ref, lam_ref, sg_ref, tq, diff):
    ot = acc[:A_V_DIM] / acc[A_V_DIM:A_V_DIM + 1]
    if diff:
        o = (ot[:, :tq] - lam_ref[:, :1] * ot[:, tq:]).T
        o = _rms(o, A_V_DIM) * sg_ref[...]
    else:
        o = ot.T
    o_ref[...] = o.astype(o_ref.dtype)


def _flash_plain_kernel(*refs, tq, tk, diff, group):
    if diff:
        qt_ref, k_ref, vt_ref, mask_ref, lam_ref, sg_ref, o_ref, acc_sc = refs
    else:
        qt_ref, k_ref, vt_ref, mask_ref, o_ref, acc_sc = refs
        lam_ref = sg_ref = None
    i = pl.program_id(2)
    qt = _flash_queries(qt_ref, diff)
    per_q = tq // tk

    mq = qt.shape[1]

    def block(j, diag):
        start = pl.multiple_of(j * tk, tk)
        if diag is None:
            pt = jnp.exp2(_dot(k_ref[pl.ds(start, tk), :], qt))
            return _dot(vt_ref[:, pl.ds(start, tk)], pt.astype(BF16))
        spans = [(c0 + diag * tk, c0 + tq) for c0 in range(0, mq, tq)]

        def cols(x):
            return x if diag == 0 else jnp.concatenate([x[:, a:b] for a, b in spans], axis=1)

        pt = jnp.exp2(_dot(k_ref[pl.ds(start, tk), :], cols(qt)))
        pt = jnp.where(cols(mask_ref[diag]) > 0, pt, 0.0)
        pv = _dot(vt_ref[:, pl.ds(start, tk)], pt.astype(BF16))
        if diag == 0:
            return pv
        pieces, off = [], 0
        for a, b in spans:
            pieces += [jnp.zeros((pv.shape[0], diag * tk), F32), pv[:, off:off + b - a]]
            off += b - a
        return jnp.concatenate(pieces, axis=1)

    acc_sc[...] = jnp.zeros(acc_sc.shape, F32)

    def add_blocks(j0, nblocks):
        acc = acc_sc[...]
        for jj in range(nblocks):
            acc = acc + block(j0 + jj, None)
        acc_sc[...] = acc

    def trips(jb, carry):
        add_blocks(jb * (group * per_q), group * per_q)
        return carry

    def leftover(r, carry):
        add_blocks((i // group * group + r) * per_q, per_q)
        return carry

    lax.fori_loop(0, i // group, trips, 0)
    if group > 1:
        lax.fori_loop(0, i % group, leftover, 0)
    acc = acc_sc[...]
    for jj in range(per_q):
        acc = acc + block(i * per_q + jj, jj)
    _flash_finish(acc, o_ref, lam_ref, sg_ref, tq, diff)


def _flash_kernel(*refs, tq, tk, diff):
    if diff:
        qt_ref, k_ref, vt_ref, mask_ref, lam_ref, sg_ref, o_ref, m_sc, a_sc, acc_sc, s_sc, p_sc = refs
    else:
        qt_ref, k_ref, vt_ref, mask_ref, o_ref, m_sc, a_sc, acc_sc, s_sc, p_sc = refs
        lam_ref = sg_ref = None
    i = pl.program_id(2)
    qt = _flash_queries(qt_ref, diff)
    per_q = tq // tk

    def scores(j):
        start = pl.multiple_of(j * tk, tk)
        return _dot(k_ref[pl.ds(start, tk), :], qt)

    def softmax(st, m_prev, diag):
        if diag is not None:
            st = jnp.where(mask_ref[diag] > 0, st, NEG)
        m_new = jnp.maximum(m_prev, jnp.max(st, axis=0, keepdims=True))
        return m_new, jnp.exp2(m_prev - m_new), jnp.exp2(st - m_new).astype(BF16)

    def weighted_values(j, acc, alpha, pt):
        start = pl.multiple_of(j * tk, tk)
        return alpha * acc + _dot(vt_ref[:, pl.ds(start, tk)], pt)

    m_sc[...] = jnp.full(m_sc.shape, NEG, F32)
    a_sc[...] = jnp.ones(a_sc.shape, F32)
    acc_sc[...] = jnp.zeros(acc_sc.shape, F32)
    p_sc[...] = jnp.zeros(p_sc.shape, BF16)
    s_sc[...] = scores(0)

    def body(jb, carry):
        m, alpha_prev, acc, p_prev, st = m_sc[...], a_sc[...], acc_sc[...], p_sc[...], s_sc[...]
        for jj in range(per_q):
            j = jb * per_q + jj
            st_next = scores(j + 1)
            m, alpha, pt = softmax(st, m, None)
            acc = weighted_values(jnp.maximum(j - 1, 0), acc, alpha_prev, p_prev)
            alpha_prev, p_prev, st = alpha, pt, st_next
        m_sc[...], a_sc[...], acc_sc[...], p_sc[...], s_sc[...] = m, alpha_prev, acc, p_prev, st
        return carry

    lax.fori_loop(0, i, body, 0)
    m, alpha_prev, acc, p_prev, st = m_sc[...], a_sc[...], acc_sc[...], p_sc[...], s_sc[...]
    for jj in range(per_q):
        j = i * per_q + jj
        st_next = scores(j + 1) if jj + 1 < per_q else None
        m, alpha, pt = softmax(st, m, jj)
        acc = weighted_values(jnp.maximum(j - 1, 0), acc, alpha_prev, p_prev)
        alpha_prev, p_prev, st = alpha, pt, st_next
    acc = weighted_values(i * per_q + per_q - 1, acc, alpha_prev, p_prev)
    _flash_finish(acc, o_ref, lam_ref, sg_ref, tq, diff)


def _flash(qt, k, vt, *, batch, heads, dk, tq, tk, diff, plain, group=1, lam=None, sg=None):
    n = k.shape[0]
    seq = n // batch
    nq = seq // tq
    mq = 2 * tq if diff else tq
    per_q = tq // tk
    kpos = jnp.arange(tq, dtype=jnp.int32).reshape(per_q, tk, 1)
    qpos = jnp.arange(mq, dtype=jnp.int32).reshape(1, 1, mq) % tq
    mask = ((kpos // CHUNK) <= (qpos // CHUNK)).astype(F32)
    in_specs = [pl.BlockSpec((dk, tq), lambda b, h, i: (h, b * nq + i)),
                pl.BlockSpec((seq, dk), lambda b, h, i: (b, h)),
                pl.BlockSpec((VT_ROWS, seq), lambda b, h, i: (h, b)),
                pl.BlockSpec((per_q, tk, mq), lambda b, h, i: (0, 0, 0), pipeline_mode=pl.Buffered(1))]
    args = [qt, k, vt, mask]
    if diff:
        in_specs += [pl.BlockSpec((1, LANE), lambda b, h, i: (0, 0))] * 2
        args += [lam, sg]
    scratch = [pltpu.VMEM((VT_ROWS, mq), F32)]
    if not plain:
        scratch = [pltpu.VMEM((1, mq), F32), pltpu.VMEM((1, mq), F32)] + scratch + [
            pltpu.VMEM((tk, mq), F32), pltpu.VMEM((tk, mq), BF16)]
    return pl.pallas_call(
        (functools.partial(_flash_plain_kernel, tq=tq, tk=tk, diff=diff, group=group) if plain
         else functools.partial(_flash_kernel, tq=tq, tk=tk, diff=diff)),
        grid=(batch, heads, nq),
        in_specs=in_specs,
        out_specs=pl.BlockSpec((tq, LANE), lambda b, h, i: (b * nq + i, h)),
        out_shape=jax.ShapeDtypeStruct((n, heads * LANE), BF16),
        scratch_shapes=scratch,
        compiler_params=_cparams(("parallel", "parallel", "arbitrary")),
        name=("flash_diff" if diff else "flash_mla") + ("_plain" if plain else ""),
    )(*args)


def _band_kernel(qt_ref, k_ref, vt_ref, bias_ref, o_ref, *, tq, nrel):
    i = pl.program_id(2)
    qt = qt_ref[...]

    def attend(rels):
        def keys(rel):
            return pl.ds(pl.multiple_of((i - rel) * tq, tq), tq)

        st = [_dot(k_ref[keys(rel), :], qt) + bias_ref[0, rel] for rel in rels]
        m = functools.reduce(jnp.maximum, [jnp.max(s, axis=0, keepdims=True) for s in st])
        acc = sum(_dot(vt_ref[:, keys(rel)], jnp.exp2(s - m).astype(BF16)) for rel, s in zip(rels, st))
        o_ref[...] = (acc[:C_DIM] / acc[C_DIM:C_DIM + 1]).T.astype(o_ref.dtype)

    for avail in range(nrel):
        cond = (i == avail) if avail < nrel - 1 else (i >= avail)
        pl.when(cond)(functools.partial(attend, tuple(range(avail + 1))))


def _band_bias_t(table, tq, nrel):
    bias = jnp.swapaxes(_band_bias(table, tq, nrel), 2, 3) * LOG2E
    k_ch = jnp.arange(tq).reshape(1, tq, 1) // CHUNK
    q_ch = (jnp.arange(nrel).reshape(nrel, 1, 1) * tq + jnp.arange(tq).reshape(1, 1, tq)) // CHUNK
    visible = (k_ch <= q_ch) & (k_ch >= q_ch - C_PAST_CHUNKS)
    return jnp.where(visible[None], bias, NEG)


def _band_bias(table, tq, nrel):
    heads = table.shape[0]
    c_all = nrel * tq
    n = tq + c_all - 1
    n_lo = tq - 1 - REL_CLIP
    n_hi = n - n_lo - (2 * REL_CLIP + 1)
    assert n_hi >= 0
    u = jnp.concatenate([jnp.broadcast_to(table[:, :1], (heads, max(n_lo, 0))), table[:, max(-n_lo, 0):],
                         jnp.broadcast_to(table[:, -1:], (heads, n_hi))], axis=1)
    w = jnp.concatenate([u[:, ::-1], jnp.zeros((heads, 1), table.dtype)], axis=1)
    m = jnp.tile(w, (1, tq))[:, :tq * n].reshape(heads, tq, n)
    t = m[:, :, tq - 1:tq - 1 + c_all]
    return t.reshape(heads, tq, nrel, tq).transpose(0, 2, 1, 3)[:, ::-1]


def _band(qt, k, vt, bias, *, batch, tq):
    n = k.shape[0]
    seq = n // batch
    nq = seq // tq
    nrel = bias.shape[1]
    return pl.pallas_call(
        functools.partial(_band_kernel, tq=tq, nrel=nrel),
        grid=(batch, C_HEADS, nq),
        in_specs=[pl.BlockSpec((LANE, tq), lambda b, h, i: (h, b * nq + i)),
                  pl.BlockSpec((seq, LANE), lambda b, h, i: (b, h)),
                  pl.BlockSpec((VT_ROWS, seq), lambda b, h, i: (h, b)),
                  pl.BlockSpec((1, nrel, tq, tq), lambda b, h, i: (h, 0, 0, 0))],
        out_specs=pl.BlockSpec((tq, LANE), lambda b, h, i: (b * nq + i, h)),
        out_shape=jax.ShapeDtypeStruct((n, C_W), BF16),
        compiler_params=_cparams(("parallel", "parallel", "arbitrary")),
        name="band",
    )(qt, k, vt, bias)


def _decode_kernel(*refs, heads, dk, diff, biased, interleaved):
    refs = list(refs)
    q_ref, kc_ref, vc_ref, kn_ref, vn_ref = refs[:5]
    rest = refs[5:]
    if biased:
        bc_ref, bn_ref, mc_ref, mn_ref = rest[:4]
        rest = rest[4:]
    if diff:
        lam_ref, sg_ref = rest[:2]
        rest = rest[2:]
    o_ref = rest[0]
    rows = q_ref.shape[0]
    for h in range(heads):
        qk = slice(h * dk, (h + 1) * dk)
        vv = slice(h * LANE, (h + 1) * LANE)
        q = q_ref[:, qk]
        if diff:
            q = _split_components(q)
        if interleaved:
            past = kc_ref.shape[1] // heads
            k_cache = kc_ref[0, pl.ds(h, past, stride=heads), :]
            v_cache = vc_ref[0, pl.ds(h, past, stride=heads), :]
        else:
            k_cache, v_cache = kc_ref[0, :, qk], vc_ref[0, :, vv]
        s_c = _dot_nt(q, k_cache.astype(BF16))
        s_n = _dot_nt(q, kn_ref[:, qk])
        if biased:
            s_c = jnp.where(mc_ref[...] > 0, s_c + bc_ref[h], NEG)
            s_n = jnp.where(mn_ref[...] > 0, s_n + bn_ref[h], NEG)
        m = jnp.maximum(jnp.max(s_c, axis=-1, keepdims=True), jnp.max(s_n, axis=-1, keepdims=True))
        p_c = jnp.exp(s_c - m)
        p_n = jnp.exp(s_n - m)
        l = jnp.sum(p_c, axis=-1, keepdims=True) + jnp.sum(p_n, axis=-1, keepdims=True)
        o = (_dot(p_c.astype(BF16), v_cache.astype(BF16)) + _dot(p_n.astype(BF16), vn_ref[:, vv])) / l
        if diff:
            o = _diff_out(o, rows, lam_ref, sg_ref)
        o_ref[:, vv] = o.astype(o_ref.dtype)


def _decode(q, kc, vc, kn, vn, *, heads, dk, rows, stream0=0, interleaved=False, diff=False, lam=None, sg=None,
            bias=None):
    n = q.shape[0]
    streams = n // rows
    past = kc.shape[1] // heads if interleaved else kc.shape[1]
    assert not interleaved or dk == LANE
    in_specs = [pl.BlockSpec((rows, heads * dk), lambda b: (b, 0)),
                pl.BlockSpec((1,) + kc.shape[1:], lambda b: (stream0 + b, 0, 0)),
                pl.BlockSpec((1,) + vc.shape[1:], lambda b: (stream0 + b, 0, 0)),
                pl.BlockSpec((rows, heads * dk), lambda b: (b, 0)),
                pl.BlockSpec((rows, heads * LANE), lambda b: (b, 0))]
    args = [q, kc, vc, kn, vn]
    if bias is not None:
        bias_c, bias_n, mask_c, mask_n = bias
        in_specs += [pl.BlockSpec((heads, rows, past), lambda b: (0, 0, 0)),
                     pl.BlockSpec((heads, rows, rows), lambda b: (0, 0, 0)),
                     pl.BlockSpec((rows, past), lambda b: (0, 0)),
                     pl.BlockSpec((rows, rows), lambda b: (0, 0))]
        args += [bias_c, bias_n, mask_c, mask_n]
    if diff:
        in_specs += [pl.BlockSpec((1, LANE), lambda b: (0, 0))] * 2
        args += [lam, sg]
    return pl.pallas_call(
        functools.partial(_decode_kernel, heads=heads, dk=dk, diff=diff, biased=bias is not None,
                          interleaved=interleaved),
        grid=(streams,),
        in_specs=in_specs,
        out_specs=pl.BlockSpec((rows, heads * LANE), lambda b: (b, 0)),
        out_shape=jax.ShapeDtypeStruct((n, heads * LANE), BF16),
        compiler_params=_cparams(("parallel",)),
        name="decode",
    )(*args)


def _decode_mla_kernel(q_ref, ckv_ref, kpe_ref, w_ref, gn_ref, gr_ref, kn_ref, vn_ref, o_ref):
    ckv = ckv_ref[...].astype(BF16)
    kpe2, ss_pe = _mla_rope_key(kpe_ref[...])
    nope_w = B_HEADS * B_NOPE
    for hh in range(B_HEADS):
        qk = slice(hh * B_PAD, (hh + 1) * B_PAD)
        vv = slice(hh * LANE, (hh + 1) * LANE)
        if hh % 2 == 0:
            nope2 = _dot(ckv, w_ref[:, hh * B_NOPE:(hh + 2) * B_NOPE])
            v2 = _dot(ckv, w_ref[:, nope_w + hh * B_V_DIM:nope_w + (hh + 2) * B_V_DIM]).astype(BF16)
        half = slice((hh % 2) * LANE, (hh % 2 + 1) * LANE)
        k_cache = jnp.concatenate(_mla_head_key(hh, nope2[:, half], kpe2, ss_pe, gn_ref[...], gr_ref[...]), axis=1)
        v_cache = v2[:, half]
        q = q_ref[:, qk]
        s_c = _dot_nt(q, k_cache)
        s_n = _dot_nt(q, kn_ref[:, qk])
        m = jnp.maximum(jnp.max(s_c, axis=-1, keepdims=True), jnp.max(s_n, axis=-1, keepdims=True))
        p_c = jnp.exp(s_c - m)
        p_n = jnp.exp(s_n - m)
        l = jnp.sum(p_c, axis=-1, keepdims=True) + jnp.sum(p_n, axis=-1, keepdims=True)
        o = (_dot(p_c.astype(BF16), v_cache) + _dot(p_n.astype(BF16), vn_ref[:, vv])) / l
        o_ref[:, vv] = o.astype(o_ref.dtype)


def _decode_mla(q, ckv, kpe, kn, vn, lw, *, rows, past, stream0):
    n = q.shape[0]

    def new(b):
        return (b, 0)

    def cached(b):
        return (stream0 + b, 0)

    def const(b):
        return (0, 0)

    return pl.pallas_call(
        _decode_mla_kernel,
        grid=(n // rows,),
        in_specs=[pl.BlockSpec((rows, B_HEADS * B_PAD), new),
                  pl.BlockSpec((past, B_KV_LORA), cached), pl.BlockSpec((past, B_ROPE), cached),
                  pl.BlockSpec(lw['w_kvb'].shape, const), pl.BlockSpec((1, LANE), const),
                  pl.BlockSpec((1, LANE), const),
                  pl.BlockSpec((rows, B_HEADS * B_PAD), new), pl.BlockSpec((rows, BV_W), new)],
        out_specs=pl.BlockSpec((rows, BV_W), new),
        out_shape=jax.ShapeDtypeStruct((n, BV_W), BF16),
        compiler_params=_cparams(("parallel",)),
        name="decode_mla",
    )(q, ckv, kpe, lw['w_kvb'], lw['gbkn'], lw['gbkr'], kn, vn)


def _merge_kernel(x_ref, g_ref, oa_ref, ob_ref, oc_ref, wga_ref, wgb_ref, wgc_ref, bga_ref, bgb_ref, bgc_ref,
                  wba_ref, wbb_ref, wbc_ref, wo_ref, out_ref, h_sc):
    j = pl.program_id(1)

    @pl.when(j == 0)
    def _():
        x = x_ref[...]
        h_sc[...] = (_rms(x, x.shape[1]) * g_ref[...]).astype(BF16)
        out_ref[...] = x

    h = h_sc[...]

    def branch(o_ref, wg_ref, bg_ref, wb_ref):
        gate = jax.nn.sigmoid(_dot(h, wg_ref[...]) + bg_ref[...])
        return gate * _dot(o_ref[...], wb_ref[...])

    merged = (branch(oa_ref, wga_ref, bga_ref, wba_ref) + branch(ob_ref, wgb_ref, bgb_ref, wbb_ref)
              + branch(oc_ref, wgc_ref, bgc_ref, wbc_ref))
    out_ref[...] += _dot(merged.astype(BF16), wo_ref[...])


def _merge(x, oa, ob, oc, lw, *, bm, bn):
    n, d_model = x.shape
    nbn = d_model // bn

    def row(i, j):
        return (i, 0)

    def const(i, j):
        return (0, 0)

    def col(k):
        return lambda i, j: (0, k * nbn + j)

    layer = lw['layer']

    def wcol(rows, k):
        return pl.BlockSpec((None, rows, bn), lambda i, j: (layer, 0, k * nbn + j))

    in_specs = [pl.BlockSpec((bm, d_model), row), pl.BlockSpec((1, d_model), const),
                pl.BlockSpec((bm, AV_W), row), pl.BlockSpec((bm, BV_W), row), pl.BlockSpec((bm, C_W), row),
                wcol(d_model, 0), wcol(d_model, 1), wcol(d_model, 2),
                pl.BlockSpec((1, bn), col(0)), pl.BlockSpec((1, bn), col(1)), pl.BlockSpec((1, bn), col(2)),
                wcol(AV_W, 0), wcol(BV_W, 0), wcol(C_W, 0),
                pl.BlockSpec((None, bn, d_model), lambda i, j: (layer, j, 0))]
    return pl.pallas_call(
        _merge_kernel,
        grid=(n // bm, nbn),
        in_specs=in_specs,
        out_specs=pl.BlockSpec((bm, d_model), row),
        out_shape=jax.ShapeDtypeStruct((n, d_model), F32),
        scratch_shapes=[pltpu.VMEM((bm, d_model), BF16)],
        compiler_params=_cparams(("parallel", "arbitrary")),
        name="merge",
    )(x, lw['g_attn'], oa, ob, oc, lw['w_g'], lw['w_g'], lw['w_g'], lw['b_g'], lw['b_g'], lw['b_g'],
      lw['w_br_a'], lw['w_br_b'], lw['w_br_c'], lw['w_o'])


def _mlp_kernel(x_ref, g_ref, wu_ref, wd_ref, out_ref, h_sc):
    j = pl.program_id(1)

    @pl.when(j == 0)
    def _():
        x = x_ref[...]
        h_sc[...] = (_rms(x, x.shape[1]) * g_ref[...]).astype(BF16)
        out_ref[...] = x

    u = jnp.maximum(_dot(h_sc[...], wu_ref[...]), 0.0)
    out_ref[...] += _dot((u * u).astype(BF16), wd_ref[...])


def _mlp(x, lw, *, bm, bf):
    n, d_model = x.shape
    d_ff = lw['w_up'].shape[2]
    layer = lw['layer']
    return pl.pallas_call(
        _mlp_kernel,
        grid=(n // bm, d_ff // bf),
        in_specs=[pl.BlockSpec((bm, d_model), lambda i, j: (i, 0)), pl.BlockSpec((1, d_model), lambda i, j: (0, 0)),
                  pl.BlockSpec((None, d_model, bf), lambda i, j: (layer, 0, j)),
                  pl.BlockSpec((None, bf, d_model), lambda i, j: (layer, j, 0))],
        out_specs=pl.BlockSpec((bm, d_model), lambda i, j: (i, 0)),
        out_shape=jax.ShapeDtypeStruct((n, d_model), F32),
        scratch_shapes=[pltpu.VMEM((bm, d_model), BF16)],
        compiler_params=_cparams(("parallel", "arbitrary")),
        name="mlp",
    )(x, lw['g_mlp'], lw['w_up'], lw['w_down'])


def _rope_tables(pos, rot):
    half = rot // 2
    inv = jnp.power(ROPE_THETA, -jnp.arange(half, dtype=F32) / half)
    ang = pos.astype(F32)[:, None] * inv[None, :]
    cos, sin = jnp.cos(ang), jnp.sin(ang)
    npos = pos.shape[0]
    one = jnp.ones((npos, 64 - rot), F32)
    c64 = jnp.concatenate([cos, cos, one], axis=1)
    s1 = jnp.concatenate([-sin, jnp.zeros((npos, 64 - half), F32)], axis=1)
    s2 = jnp.concatenate([jnp.zeros((npos, half), F32), sin, jnp.zeros((npos, 64 - rot), F32)], axis=1)
    return [jnp.tile(t, (1, 2)) for t in (c64, s1, s2)]


def _score_bound(gq, gk, d):
    return math.sqrt(d) * LOG2E * jnp.max(jnp.abs(gq)) * jnp.max(jnp.abs(gk))


def _layer_weights(l, p, stacks):
    d_model = p['w_in'].shape[1]
    w_in = p['w_in'][l]
    o = [0]
    for width in (A_W, A_W, AV_W, B_Q_LORA, B_KV_LORA + B_ROPE, C_W, C_W, C_W):
        o.append(o[-1] + width)
    cols = lambda a, b: w_in[:, a:b]
    w_a = jnp.concatenate([
        cols(o[0], o[1]), cols(o[1], o[2]), cols(o[2], o[3]), cols(o[3], o[4]),
        cols(o[4], o[4] + B_KV_LORA), cols(o[5], o[6]), cols(o[6], o[7]), cols(o[7], o[8]),
        cols(o[4] + B_KV_LORA, o[5]), jnp.zeros((d_model, LANE - B_ROPE), F32)], axis=1).astype(BF16)
    w_qb = p['w_qb'][l].reshape(B_Q_LORA, B_HEADS, B_QK)
    w_qb = jnp.concatenate([w_qb[:, :, :B_NOPE].reshape(B_Q_LORA, -1),
                            w_qb[:, :, B_NOPE:].reshape(B_Q_LORA, -1)], axis=1).astype(BF16)
    w_kvb = p['w_kvb'][l].reshape(B_KV_LORA, B_HEADS, B_NOPE + B_V_DIM)
    w_kvb = jnp.concatenate([w_kvb[:, :, :B_NOPE].reshape(B_KV_LORA, -1),
                             w_kvb[:, :, B_NOPE:].reshape(B_KV_LORA, -1)], axis=1).astype(BF16)
    lam_init = 0.8 - 0.6 * math.exp(-0.3 * l)
    lf = p['a_lambda'][l].astype(F32)
    lam = jnp.exp(jnp.sum(lf[0] * lf[1])) - jnp.exp(jnp.sum(lf[2] * lf[3])) + lam_init

    def tiled(g, reps, scale=1.0):
        return jnp.tile(g * scale, reps)[None, :]

    bqn = p['b_q_norm'][l]
    bkn = p['b_k_norm'][l]
    return {
        'g_attn': p['attn_norm'][l][None, :],
        'g_mlp': p['mlp_norm'][l][None, :],
        'w_a': w_a,
        'b_g': p['b_gate'][l][None, :],
        'w_qb': w_qb,
        'w_kvb': w_kvb,
        'gaq': tiled(p['a_q_norm'][l], 2 * A_HEADS, A_QK_DIM ** -0.5),
        'gak': tiled(p['a_k_norm'][l], 2 * A_HEADS),
        'gbqa': p['b_qa_norm'][l][None, :],
        'gbkv': p['b_kv_norm'][l][None, :],
        'gbqn': tiled(bqn[:B_NOPE], 1, B_QK ** -0.5),
        'gbqr': tiled(bqn[B_NOPE:], 2, B_QK ** -0.5),
        'gaq_flash': tiled(p['a_q_norm'][l], 2 * A_HEADS, A_QK_DIM ** -0.5 * LOG2E),
        'gbqn_flash': tiled(bqn[:B_NOPE], 1, B_QK ** -0.5 * LOG2E),
        'gbqr_flash': tiled(bqn[B_NOPE:], 2, B_QK ** -0.5 * LOG2E),
        'gcq_flash': tiled(p['c_q_norm'][l], C_HEADS, C_DIM ** -0.5 * LOG2E),
        'gbkn': tiled(bkn[:B_NOPE], 1),
        'gbkr': tiled(bkn[B_NOPE:], 2),
        'gcq': tiled(p['c_q_norm'][l], C_HEADS, C_DIM ** -0.5),
        'gck': tiled(p['c_k_norm'][l], C_HEADS),
        'lam': jnp.broadcast_to(lam, (1, LANE)).astype(F32),
        'sg': (p['a_sub_norm'][l] * (1.0 - lam_init))[None, :],
        'bound_a': _score_bound(p['a_q_norm'][l], p['a_k_norm'][l], A_QK_DIM),
        'bound_b': _score_bound(bqn, bkn, B_QK),
        'layer': l,
        **stacks,
    }


def _stacked_weights(p):
    gate_off = p['w_in'].shape[2] - 3 * p['w_in'].shape[1]
    stacks = {name: p[name].astype(BF16) for name in ('w_br_a', 'w_br_b', 'w_br_c', 'w_o', 'w_up', 'w_down')}
    stacks['w_g'] = p['w_in'][:, :, gate_off:].astype(BF16)
    return stacks


def _pick(n, prefs):
    for b in prefs:
        if n % b == 0:
            return b
    raise ValueError(f"no block size in {prefs} divides {n}")


def kernel(x_prompt, x_sample, cache_a_k, cache_a_v, cache_b_ckv, cache_b_kpe, cache_c_k, cache_c_v,
           attn_norm, w_in, b_gate, a_q_norm, a_k_norm, a_lambda, a_sub_norm,
           b_qa_norm, b_kv_norm, w_qb, w_kvb, b_q_norm, b_k_norm,
           c_q_norm, c_k_norm, c_rel_bias, w_br_a, w_br_b, w_br_c, w_o,
           mlp_norm, w_up, w_down):
    p = dict(attn_norm=attn_norm, w_in=w_in, b_gate=b_gate, a_q_norm=a_q_norm, a_k_norm=a_k_norm,
             a_lambda=a_lambda, a_sub_norm=a_sub_norm, b_qa_norm=b_qa_norm, b_kv_norm=b_kv_norm,
             w_qb=w_qb, w_kvb=w_kvb, b_q_norm=b_q_norm, b_k_norm=b_k_norm, c_q_norm=c_q_norm,
             c_k_norm=c_k_norm, w_br_a=w_br_a, w_br_b=w_br_b, w_br_c=w_br_c, w_o=w_o, mlp_norm=mlp_norm,
             w_up=w_up, w_down=w_down)
    depth = w_in.shape[0]
    stacks = _stacked_weights(p)
    batch, seq, d_model = x_prompt.shape
    streams, dec, _ = x_sample.shape
    past = cache_a_k.shape[2]
    c_past = cache_c_k.shape[2]
    c_keep = min(C_WINDOW, seq)
    n_p, n_s = batch * seq, streams * dec
    assert seq % CHUNK == 0 and c_keep % CHUNK == 0

    pos_p = jnp.arange(seq)
    pos_s = past + jnp.arange(dec)
    tabs_p = _rope_tables(pos_p, A_ROT) + _rope_tables(pos_p, B_ROPE)
    tabs_s = [jnp.tile(t, (streams, 1)) for t in _rope_tables(pos_s, A_ROT) + _rope_tables(pos_s, B_ROPE)]

    kc_pos = jnp.concatenate([past - c_past + jnp.arange(c_past), pos_s])
    q_ch, k_ch = pos_s // CHUNK, kc_pos // CHUNK
    c_mask = ((k_ch[None, :] <= q_ch[:, None]) & (k_ch[None, :] >= q_ch[:, None] - C_PAST_CHUNKS)).astype(F32)
    rel_s = jnp.clip(pos_s[:, None] - kc_pos[None, :], -REL_CLIP, REL_CLIP) + REL_CLIP

    bm_in = _pick(seq, (512, 256, 128, 64))
    bm_tok = _pick(seq, (512, 256, 128, 64))
    t_a = _pick(seq, (512, 256, 128))
    t_b = _pick(seq, (1024, 512, 256, 128))
    tk_a, tk_b = t_a, t_b // 2
    tp_a = (t_b, t_b // 2)
    tp_b = (t_b, t_b // 2)
    t_c = _pick(seq, (512, 256, 128))
    nrel_c = C_WINDOW // t_c + 1
    bm_wide = _pick(seq, (1024, 512, 256, 128, 64))
    bn_merge = 512
    bf_mlp = 512

    ca_k = cache_a_k.reshape(depth * streams, past * A_HEADS, LANE)
    ca_v = cache_a_v.reshape(depth * streams, past * A_HEADS, LANE)
    cb_ckv = cache_b_ckv.reshape(depth * streams * past, B_KV_LORA)
    cb_kpe = cache_b_kpe.reshape(depth * streams * past, B_ROPE)
    cc_k = cache_c_k.reshape(depth * streams, c_past * C_HEADS, LANE)
    cc_v = cache_c_v.reshape(depth * streams, c_past * C_HEADS, LANE)

    xp = x_prompt.reshape(n_p, d_model)
    xs = x_sample.reshape(n_s, d_model)
    outs = [[] for _ in range(12)]
    stacked = None
    for l in range(depth):
        lw = _layer_weights(l, p, stacks)

        (qa, kaf, kab, vaf, vab, qb, ckv, kpe, qc, kcb, vcb, kcf, vcf) = _in_proj(
            xp, lw, tabs_p, bm=bm_in, seg_rows=seq, keep=c_keep, transposed=True, stack=(l, depth, stacked))
        stacked = (kaf, vaf, ckv, kpe)
        kb, vb = _mla_kv(ckv, kpe, lw, bm=bm_tok, n=n_p, row0=l * n_p, transposed=True)
        flash_a = functools.partial(_flash, qa, kab, vab, batch=batch, heads=A_HEADS, dk=LANE, diff=True,
                                    lam=lw['lam'], sg=lw['sg'])
        flash_b = functools.partial(_flash, qb, kb, vb, batch=batch, heads=B_HEADS, dk=B_PAD, diff=False)
        oa = lax.cond(lw['bound_a'] <= FLASH_PLAIN_MAX_LOG2,
                      functools.partial(flash_a, tq=tp_a[0], tk=tp_a[1], plain=True, group=2),
                      functools.partial(flash_a, tq=t_a, tk=tk_a, plain=False))
        ob = lax.cond(lw['bound_b'] <= FLASH_PLAIN_MAX_LOG2,
                      functools.partial(flash_b, tq=tp_b[0], tk=tp_b[1], plain=True, group=4),
                      functools.partial(flash_b, tq=t_b, tk=tk_b, plain=False))
        oc = _band(qc, kcb, vcb, _band_bias_t(c_rel_bias[l], t_c, nrel_c), batch=batch, tq=t_c)
        xp = _merge(xp, oa, ob, oc, lw, bm=bm_tok, bn=bn_merge)
        xp = _mlp(xp, lw, bm=bm_wide, bf=bf_mlp)
        for dst, val in zip(outs[4:6], (kcf, vcf)):
            dst.append(val)

        (qa, kaf, kab, vaf, vab, qb, ckv, kpe, qc, kcb, vcb, kcf, vcf) = _in_proj(
            xs, lw, tabs_s, bm=n_s, seg_rows=n_s, keep=n_s, transposed=False)
        kb, vb = _mla_kv(ckv, kpe, lw, bm=n_s)
        oa = _decode(qa, ca_k, ca_v, kab, vab, heads=A_HEADS, dk=LANE, rows=dec, stream0=l * streams,
                     interleaved=True, diff=True, lam=lw['lam'], sg=lw['sg'])
        ob = _decode_mla(qb, cb_ckv, cb_kpe, kb, vb, lw, rows=dec, past=past, stream0=l * streams)
        bias_s = c_rel_bias[l][:, rel_s]
        oc = _decode(qc, cc_k, cc_v, kcb, vcb, heads=C_HEADS, dk=LANE, rows=dec, stream0=l * streams,
                     interleaved=True, bias=(bias_s[:, :, :c_past], bias_s[:, :, c_past:], c_mask[:, :c_past], c_mask[:, c_past:]))
        xs = _merge(xs, oa, ob, oc, lw, bm=n_s, bn=bn_merge)
        xs = _mlp(xs, lw, bm=n_s, bf=bf_mlp)
        for dst, val in zip(outs[6:], (kaf, vaf, ckv, kpe, kcf, vcf)):
            dst.append(val)

    st = list(stacked) + [jnp.stack(o) for o in outs[4:]]
    return (xp.reshape(batch, seq, d_model), xs.reshape(streams, dec, d_model),
            st[0].reshape(depth, batch, seq, A_HEADS, 2 * A_QK_DIM),
            st[1].reshape(depth, batch, seq, A_HEADS, A_V_DIM),
            st[2].reshape(depth, batch, seq, B_KV_LORA),
            st[3].reshape(depth, batch, seq, B_ROPE),
            st[4].reshape(depth, batch, c_keep, C_HEADS, C_DIM),
            st[5].reshape(depth, batch, c_keep, C_HEADS, C_DIM),
            st[6].reshape(depth, streams, dec, A_HEADS, 2 * A_QK_DIM),
            st[7].reshape(depth, streams, dec, A_HEADS, A_V_DIM),
            st[8].reshape(depth, streams, dec, B_KV_LORA),
            st[9].reshape(depth, streams, dec, B_ROPE),
            st[10].reshape(depth, streams, dec, C_HEADS, C_DIM),
            st[11].reshape(depth, streams, dec, C_HEADS, C_DIM))
```

```python
import functools
import math

import jax
import jax.numpy as jnp
from jax import lax
from jax.experimental import pallas as pl
from jax.experimental.pallas import tpu as pltpu

F32 = jnp.float32
BF16 = jnp.bfloat16

CHUNK = 64
ROPE_THETA = 500000.0
EPS = 1e-6
NEG = -1e30
LOG2E = math.log2(math.e)
FLASH_PLAIN_MAX_LOG2 = 50.0
LANE = 128

A_HEADS = 4
A_QK_DIM = 64
A_V_DIM = 128
A_ROT = A_QK_DIM // 4
B_HEADS = 8
B_Q_LORA = 512
B_KV_LORA = 256
B_NOPE = 128
B_ROPE = 64
B_V_DIM = 128
B_QK = B_NOPE + B_ROPE
B_PAD = 2 * LANE
BF16_SUBLANES = 16
VT_ROWS = LANE + BF16_SUBLANES
C_HEADS = 4
C_DIM = 128
C_PAST_CHUNKS = 8
C_WINDOW = C_PAST_CHUNKS * CHUNK
REL_CLIP = 128

A_W = A_HEADS * 2 * A_QK_DIM
AV_W = A_HEADS * A_V_DIM
BV_W = B_HEADS * B_V_DIM
C_W = C_HEADS * C_DIM

OFF_AQ = 0
OFF_AK = OFF_AQ + A_W
OFF_AV = OFF_AK + A_W
OFF_BQ = OFF_AV + AV_W
OFF_CKV = OFF_BQ + B_Q_LORA
OFF_CQ = OFF_CKV + B_KV_LORA
OFF_CK = OFF_CQ + C_W
OFF_CV = OFF_CK + C_W
OFF_KPE = OFF_CV + C_W
WA_COLS = OFF_KPE + LANE

VMEM_LIMIT = 56 * 1024 * 1024


VMEM_LIMIT_IN_PROJ = 60 * 1024 * 1024


def _cparams(sem, vmem=VMEM_LIMIT):
    return pltpu.CompilerParams(dimension_semantics=sem, vmem_limit_bytes=vmem)


def _rms(x, width):
    ms = jnp.sum(x * x, axis=-1, keepdims=True) * (1.0 / width)
    return x * lax.rsqrt(ms + EPS)


def _rms_half(zc, lane_lo):
    sq = zc * zc
    lo = jnp.sum(jnp.where(lane_lo, sq, 0.0), axis=-1, keepdims=True)
    hi = jnp.sum(jnp.where(lane_lo, 0.0, sq), axis=-1, keepdims=True)
    ms = jnp.where(lane_lo, lo, hi) * (1.0 / 64.0)
    return zc * lax.rsqrt(ms + EPS)


def _rope(y, c, s1, s2, half):
    return y * c + pltpu.roll(y, LANE - half, 1) * s1 + pltpu.roll(y, half, 1) * s2


def _put_values_t(ref, head, val):
    base = head * VT_ROWS
    ref[base:base + LANE, :] = val.T.astype(BF16)
    ref[base + LANE:base + VT_ROWS, :] = jnp.ones((VT_ROWS - LANE, val.shape[0]), BF16)


def _dot(a, b):
    return jnp.dot(a, b, preferred_element_type=F32)


def _dot_nt(a, b):
    return lax.dot_general(a, b, (((1,), (1,)), ((), ())), preferred_element_type=F32)


def _in_proj_kernel(*refs, transposed, n_alias, interleaved):
    (x_ref, g_ref, wa_ref, wqb_ref, gaq_ref, gak_ref, gbqa_ref, gbkv_ref, gbqn_ref, gbqr_ref,
     gcq_ref, gck_ref, ca_ref, s1a_ref, s2a_ref, cb_ref, s1b_ref, s2b_ref) = refs[:18]
    (qa_ref, kaf_ref, kab_ref, vaf_ref, vab_ref, qb_ref, ckv_ref, kpe_ref,
     qc_ref, kcb_ref, vcb_ref, kcf_ref, vcf_ref) = refs[18 + n_alias:]
    d_model = x_ref.shape[1]

    def put(ref, c, val):
        if transposed:
            ref[c * LANE:(c + 1) * LANE, :] = val.T.astype(BF16)
        else:
            ref[:, c * LANE:(c + 1) * LANE] = val.astype(BF16)

    def put_cache(ref, head, val):
        if interleaved:
            ref[pl.ds(head, val.shape[0], stride=A_HEADS), :] = val
        else:
            ref[:, head * LANE:(head + 1) * LANE] = val

    h = (_rms(x_ref[...], d_model) * g_ref[...]).astype(BF16)
    lane_lo = lax.broadcasted_iota(jnp.int32, (1, LANE), 1) < 64
    ca, s1a, s2a = ca_ref[...], s1a_ref[...], s2a_ref[...]
    cb, s1b, s2b = cb_ref[...], s1b_ref[...], s2b_ref[...]

    def proj(off, width):
        return _dot(h, wa_ref[:, off:off + width])

    def chunk(c):
        return slice(c * LANE, (c + 1) * LANE)

    z = proj(OFF_AQ, A_W)
    for c in range(A_W // LANE):
        y = _rms_half(z[:, chunk(c)], lane_lo) * gaq_ref[:, chunk(c)]
        put(qa_ref, c, _rope(y, ca, s1a, s2a, A_ROT // 2))
    z = proj(OFF_AK, A_W)
    for c in range(A_W // LANE):
        y = _rms_half(z[:, chunk(c)], lane_lo) * gak_ref[:, chunk(c)]
        y = _rope(y, ca, s1a, s2a, A_ROT // 2)
        put_cache(kaf_ref, c, y)
        kab_ref[:, chunk(c)] = y.astype(BF16)
    z = proj(OFF_AV, AV_W)
    for c in range(A_HEADS):
        put_cache(vaf_ref, c, z[:, chunk(c)])
        if transposed:
            _put_values_t(vab_ref, c, z[:, chunk(c)])
        else:
            put(vab_ref, c, z[:, chunk(c)])

    z = proj(OFF_BQ, B_Q_LORA)
    cq = (_rms(z, B_Q_LORA) * gbqa_ref[...]).astype(BF16)
    qraw = _dot(cq, wqb_ref[...])
    nope_w = B_HEADS * B_NOPE
    roped = [_rope(qraw[:, nope_w + c * LANE:nope_w + (c + 1) * LANE], cb, s1b, s2b, B_ROPE // 2)
             for c in range(B_HEADS * B_ROPE // LANE)]
    for hh in range(B_HEADS):
        nope = qraw[:, hh * B_NOPE:(hh + 1) * B_NOPE]
        in_half = lane_lo if hh % 2 == 0 else jnp.logical_not(lane_lo)
        rsel = jnp.where(in_half, roped[hh // 2], 0.0)
        ss = jnp.sum(nope * nope, axis=-1, keepdims=True) + jnp.sum(rsel * rsel, axis=-1, keepdims=True)
        inv = lax.rsqrt(ss * (1.0 / B_QK) + EPS)
        put(qb_ref, 2 * hh, nope * inv * gbqn_ref[...])
        put(qb_ref, 2 * hh + 1, rsel * inv * gbqr_ref[...])
    z = proj(OFF_CKV, B_KV_LORA)
    ckv_ref[...] = _rms(z, B_KV_LORA) * gbkv_ref[...]
    z = proj(OFF_KPE, LANE)
    kpe_ref[...] = _rope(z, cb, s1b, s2b, B_ROPE // 2)[:, :B_ROPE]

    z = proj(OFF_CQ, C_W)
    for c in range(C_HEADS):
        put(qc_ref, c, _rms(z[:, chunk(c)], C_DIM) * gcq_ref[:, chunk(c)])
    z = proj(OFF_CK, C_W)
    for c in range(C_HEADS):
        y = _rms(z[:, chunk(c)], C_DIM) * gck_ref[:, chunk(c)]
        kcf_ref[:, chunk(c)] = y
        kcb_ref[:, chunk(c)] = y.astype(BF16)
    z = proj(OFF_CV, C_W)
    vcf_ref[...] = z
    for c in range(C_HEADS):
        if transposed:
            _put_values_t(vcb_ref, c, z[:, chunk(c)])
        else:
            put(vcb_ref, c, z[:, chunk(c)])


STACKED_OUTS = (1, 3, 6, 7)
INTERLEAVED_OUTS = (1, 3)


def _in_proj(x, lw, tabs, *, bm, seg_rows, keep, transposed, stack=None):
    n, d_model = x.shape
    nb = n // bm
    nbs = seg_rows // bm
    nbk = keep // bm
    ntab = tabs[0].shape[0] // bm

    def row(i):
        return (i, 0)

    def const(i):
        return (0, 0)

    def tab(i):
        return (i % ntab, 0)

    def kept(i):
        return ((i // nbs) * nbk + jnp.maximum(i % nbs - (nbs - nbk), 0), 0)

    def rows(width):
        return pl.BlockSpec((bm, width), row)

    def full(a):
        return pl.BlockSpec(a.shape, const, pipeline_mode=pl.Buffered(1))

    fl = '_flash' if transposed else ''
    gains = [lw['gaq' + fl], lw['gak'], lw['gbqa'], lw['gbkv'], lw['gbqn' + fl], lw['gbqr' + fl],
             lw['gcq' + fl], lw['gck']]
    in_specs = ([rows(d_model), full(lw['g_attn']), full(lw['w_a']), full(lw['w_qb'])]
                + [full(g) for g in gains]
                + [pl.BlockSpec((bm, LANE), tab) for _ in tabs])
    out_shape = [
        jax.ShapeDtypeStruct((n, A_W), BF16),
        jax.ShapeDtypeStruct((n, A_W), F32),
        jax.ShapeDtypeStruct((n, A_W), BF16),
        jax.ShapeDtypeStruct((n, AV_W), F32),
        jax.ShapeDtypeStruct((n, AV_W), BF16),
        jax.ShapeDtypeStruct((n, B_HEADS * B_PAD), BF16),
        jax.ShapeDtypeStruct((n, B_KV_LORA), F32),
        jax.ShapeDtypeStruct((n, B_ROPE), F32),
        jax.ShapeDtypeStruct((n, C_W), BF16),
        jax.ShapeDtypeStruct((n, C_W), BF16),
        jax.ShapeDtypeStruct((n, C_W), BF16),
        jax.ShapeDtypeStruct((n // seg_rows * keep, C_W), F32),
        jax.ShapeDtypeStruct((n // seg_rows * keep, C_W), F32),
    ]
    out_specs = [rows(s.shape[1]) for s in out_shape[:11]] + [pl.BlockSpec((bm, C_W), kept)] * 2
    if transposed:
        for k, width in ((0, A_W), (4, A_HEADS * VT_ROWS), (5, B_HEADS * B_PAD), (8, C_W),
                         (10, C_HEADS * VT_ROWS)):
            out_shape[k] = jax.ShapeDtypeStruct((width, n), BF16)
            out_specs[k] = pl.BlockSpec((width, bm), lambda i: (0, i))
    args = [x, lw['g_attn'], lw['w_a'], lw['w_qb'], *gains, *tabs]
    aliases = {}
    if stack is not None:
        layer, depth, buffers = stack
        for k in STACKED_OUTS:
            width = out_shape[k].shape[1]
            split = width // LANE if k in INTERLEAVED_OUTS else 1
            out_shape[k] = jax.ShapeDtypeStruct((depth * n * split, width // split), F32)
            out_specs[k] = pl.BlockSpec((bm * split, width // split), lambda i: (layer * nb + i, 0))
        if buffers is not None:
            for pos, k in enumerate(STACKED_OUTS):
                aliases[len(args)] = k
                args.append(buffers[pos])
                in_specs.append(pl.BlockSpec(memory_space=pl.ANY))
    return pl.pallas_call(
        functools.partial(_in_proj_kernel, transposed=transposed, n_alias=len(aliases),
                          interleaved=stack is not None),
        grid=(nb,),
        in_specs=in_specs,
        out_specs=out_specs,
        out_shape=out_shape,
        input_output_aliases=aliases,
        compiler_params=_cparams(("arbitrary",), VMEM_LIMIT_IN_PROJ),
        name="in_proj",
    )(*args)


def _mla_rope_key(kpe):
    return jnp.concatenate([kpe, kpe], axis=1), jnp.sum(kpe * kpe, axis=-1, keepdims=True)


def _mla_head_key(hh, nope, kpe2, ss_pe, gn, gr):
    ss = jnp.sum(nope * nope, axis=-1, keepdims=True) + ss_pe
    inv = lax.rsqrt(ss * (1.0 / B_QK) + EPS)
    lane_lo = lax.broadcasted_iota(jnp.int32, (1, LANE), 1) < 64
    in_half = lane_lo if hh % 2 == 0 else jnp.logical_not(lane_lo)
    return (nope * inv * gn).astype(BF16), jnp.where(in_half, kpe2 * inv * gr, 0.0).astype(BF16)


def _mla_kv_kernel(ckv_ref, kpe_ref, w_ref, gn_ref, gr_ref, kb_ref, vb_ref, *, transposed):
    ckv = ckv_ref[...]
    kv = _dot(ckv.astype(BF16), w_ref[...])
    kpe2, ss_pe = _mla_rope_key(kpe_ref[...])
    nope_w = B_HEADS * B_NOPE
    for hh in range(B_HEADS):
        k_nope, k_rope = _mla_head_key(hh, kv[:, hh * B_NOPE:(hh + 1) * B_NOPE], kpe2, ss_pe,
                                       gn_ref[...], gr_ref[...])
        kb_ref[:, hh * B_PAD:hh * B_PAD + LANE] = k_nope
        kb_ref[:, hh * B_PAD + LANE:(hh + 1) * B_PAD] = k_rope
    if transposed:
        for c in range(B_HEADS):
            _put_values_t(vb_ref, c, kv[:, nope_w + c * LANE:nope_w + (c + 1) * LANE])
    else:
        vb_ref[...] = kv[:, nope_w:].astype(BF16)


def _mla_kv(ckv, kpe, lw, *, bm, n=None, row0=0, transposed=False):
    n = ckv.shape[0] if n is None else n
    blk0 = row0 // bm
    assert row0 % bm == 0
    if transposed:
        vb_shape, vb_spec = (B_HEADS * VT_ROWS, n), pl.BlockSpec((B_HEADS * VT_ROWS, bm), lambda i: (0, i))
    else:
        vb_shape, vb_spec = (n, BV_W), pl.BlockSpec((bm, BV_W), lambda i: (i, 0))

    def row(i):
        return (i, 0)

    def src_row(i):
        return (blk0 + i, 0)

    def const(i):
        return (0, 0)

    return pl.pallas_call(
        functools.partial(_mla_kv_kernel, transposed=transposed),
        grid=(n // bm,),
        in_specs=[pl.BlockSpec((bm, B_KV_LORA), src_row), pl.BlockSpec((bm, B_ROPE), src_row),
                  pl.BlockSpec(lw['w_kvb'].shape, const), pl.BlockSpec((1, LANE), const),
                  pl.BlockSpec((1, LANE), const)],
        out_specs=[pl.BlockSpec((bm, B_HEADS * B_PAD), row), vb_spec],
        out_shape=[jax.ShapeDtypeStruct((n, B_HEADS * B_PAD), BF16), jax.ShapeDtypeStruct(vb_shape, BF16)],
        compiler_params=_cparams(("arbitrary",)),
        name="mla_kv",
    )(ckv, kpe, lw['w_kvb'], lw['gbkn'], lw['gbkr'])


def _split_components(q):
    lane_lo = lax.broadcasted_iota(jnp.int32, (1, LANE), 1) < 64
    zero = jnp.zeros_like(q)
    return jnp.concatenate([jnp.where(lane_lo, q, zero), jnp.where(lane_lo, zero, q)], axis=0)


def _diff_out(o, rows, lam_ref, sg_ref):
    o = o[:rows] - lam_ref[...] * o[rows:]
    return _rms(o, A_V_DIM) * sg_ref[...]


def _flash_queries(qt_ref, diff):
    qt = qt_ref[...]
    if diff:
        row_lo = lax.broadcasted_iota(jnp.int32, (qt.shape[0], 1), 0) < A_QK_DIM
        zero = jnp.zeros_like(qt)
        qt = jnp.concatenate([jnp.where(row_lo, qt, zero), jnp.where(row_lo, zero, qt)], axis=1)
    return qt


def _flash_finish(acc, o_ref, lam_ref, sg_ref, tq, diff):
    ot = acc[:A_V_DIM] / acc[A_V_DIM:A_V_DIM + 1]
    if diff:
        o = (ot[:, :tq] - lam_ref[:, :1] * ot[:, tq:]).T
        o = _rms(o, A_V_DIM) * sg_ref[...]
    else:
        o = ot.T
    o_ref[...] = o.astype(o_ref.dtype)


def _flash_plain_kernel(*refs, tq, tk, diff, group):
    if diff:
        qt_ref, k_ref, vt_ref, mask_ref, lam_ref, sg_ref, o_ref, acc_sc = refs
    else:
        qt_ref, k_ref, vt_ref, mask_ref, o_ref, acc_sc = refs
        lam_ref = sg_ref = None
    i = pl.program_id(2)
    qt = _flash_queries(qt_ref, diff)
    per_q = tq // tk

    mq = qt.shape[1]

    def block(j, diag):
        start = pl.multiple_of(j * tk, tk)
        if diag is None:
            pt = jnp.exp2(_dot(k_ref[pl.ds(start, tk), :], qt))
            return _dot(vt_ref[:, pl.ds(start, tk)], pt.astype(BF16))
        spans = [(c0 + diag * tk, c0 + tq) for c0 in range(0, mq, tq)]

        def cols(x):
            return x if diag == 0 else jnp.concatenate([x[:, a:b] for a, b in spans], axis=1)

        pt = jnp.exp2(_dot(k_ref[pl.ds(start, tk), :], cols(qt)))
        pt = jnp.where(cols(mask_ref[diag]) > 0, pt, 0.0)
        pv = _dot(vt_ref[:, pl.ds(start, tk)], pt.astype(BF16))
        if diag == 0:
            return pv
        pieces, off = [], 0
        for a, b in spans:
            pieces += [jnp.zeros((pv.shape[0], diag * tk), F32), pv[:, off:off + b - a]]
            off += b - a
        return jnp.concatenate(pieces, axis=1)

    acc_sc[...] = jnp.zeros(acc_sc.shape, F32)

    def add_blocks(j0, nblocks):
        acc = acc_sc[...]
        for jj in range(nblocks):
            acc = acc + block(j0 + jj, None)
        acc_sc[...] = acc

    def trips(jb, carry):
        add_blocks(jb * (group * per_q), group * per_q)
        return carry

    def leftover(r, carry):
        add_blocks((i // group * group + r) * per_q, per_q)
        return carry

    lax.fori_loop(0, i // group, trips, 0)
    if group > 1:
        lax.fori_loop(0, i % group, leftover, 0)
    acc = acc_sc[...]
    for jj in range(per_q):
        acc = acc + block(i * per_q + jj, jj)
    _flash_finish(acc, o_ref, lam_ref, sg_ref, tq, diff)


def _flash_kernel(*refs, tq, tk, diff):
    if diff:
        qt_ref, k_ref, vt_ref, mask_ref, lam_ref, sg_ref, o_ref, m_sc, a_sc, acc_sc, s_sc, p_sc = refs
    else:
        qt_ref, k_ref, vt_ref, mask_ref, o_ref, m_sc, a_sc, acc_sc, s_sc, p_sc = refs
        lam_ref = sg_ref = None
    i = pl.program_id(2)
    qt = _flash_queries(qt_ref, diff)
    per_q = tq // tk

    def scores(j):
        start = pl.multiple_of(j * tk, tk)
        return _dot(k_ref[pl.ds(start, tk), :], qt)

    def softmax(st, m_prev, diag):
        if diag is not None:
            st = jnp.where(mask_ref[diag] > 0, st, NEG)
        m_new = jnp.maximum(m_prev, jnp.max(st, axis=0, keepdims=True))
        return m_new, jnp.exp2(m_prev - m_new), jnp.exp2(st - m_new).astype(BF16)

    def weighted_values(j, acc, alpha, pt):
        start = pl.multiple_of(j * tk, tk)
        return alpha * acc + _dot(vt_ref[:, pl.ds(start, tk)], pt)

    m_sc[...] = jnp.full(m_sc.shape, NEG, F32)
    a_sc[...] = jnp.ones(a_sc.shape, F32)
    acc_sc[...] = jnp.zeros(acc_sc.shape, F32)
    p_sc[...] = jnp.zeros(p_sc.shape, BF16)
    s_sc[...] = scores(0)

    def body(jb, carry):
        m, alpha_prev, acc, p_prev, st = m_sc[...], a_sc[...], acc_sc[...], p_sc[...], s_sc[...]
        for jj in range(per_q):
            j = jb * per_q + jj
            st_next = scores(j + 1)
            m, alpha, pt = softmax(st, m, None)
            acc = weighted_values(jnp.maximum(j - 1, 0), acc, alpha_prev, p_prev)
            alpha_prev, p_prev, st = alpha, pt, st_next
        m_sc[...], a_sc[...], acc_sc[...], p_sc[...], s_sc[...] = m, alpha_prev, acc, p_prev, st
        return carry

    lax.fori_loop(0, i, body, 0)
    m, alpha_prev, acc, p_prev, st = m_sc[...], a_sc[...], acc_sc[...], p_sc[...], s_sc[...]
    for jj in range(per_q):
        j = i * per_q + jj
        st_next = scores(j + 1) if jj + 1 < per_q else None
        m, alpha, pt = softmax(st, m, jj)
        acc = weighted_values(jnp.maximum(j - 1, 0), acc, alpha_prev, p_prev)
        alpha_prev, p_prev, st = alpha, pt, st_next
    acc = weighted_values(i * per_q + per_q - 1, acc, alpha_prev, p_prev)
    _flash_finish(acc, o_ref, lam_ref, sg_ref, tq, diff)


def _flash(qt, k, vt, *, batch, heads, dk, tq, tk, diff, plain, group=1, lam=None, sg=None):
    n = k.shape[0]
    seq = n // batch
    nq = seq // tq
    mq = 2 * tq if diff else tq
    per_q = tq // tk
    kpos = jnp.arange(tq, dtype=jnp.int32).reshape(per_q, tk, 1)
    qpos = jnp.arange(mq, dtype=jnp.int32).reshape(1, 1, mq) % tq
    mask = ((kpos // CHUNK) <= (qpos // CHUNK)).astype(F32)
    in_specs = [pl.BlockSpec((dk, tq), lambda b, h, i: (h, b * nq + i)),
                pl.BlockSpec((seq, dk), lambda b, h, i: (b, h)),
                pl.BlockSpec((VT_ROWS, seq), lambda b, h, i: (h, b)),
                pl.BlockSpec((per_q, tk, mq), lambda b, h, i: (0, 0, 0), pipeline_mode=pl.Buffered(1))]
    args = [qt, k, vt, mask]
    if diff:
        in_specs += [pl.BlockSpec((1, LANE), lambda b, h, i: (0, 0))] * 2
        args += [lam, sg]
    scratch = [pltpu.VMEM((VT_ROWS, mq), F32)]
    if not plain:
        scratch = [pltpu.VMEM((1, mq), F32), pltpu.VMEM((1, mq), F32)] + scratch + [
            pltpu.VMEM((tk, mq), F32), pltpu.VMEM((tk, mq), BF16)]
    return pl.pallas_call(
        (functools.partial(_flash_plain_kernel, tq=tq, tk=tk, diff=diff, group=group) if plain
         else functools.partial(_flash_kernel, tq=tq, tk=tk, diff=diff)),
        grid=(batch, heads, nq),
        in_specs=in_specs,
        out_specs=pl.BlockSpec((tq, LANE), lambda b, h, i: (b * nq + i, h)),
        out_shape=jax.ShapeDtypeStruct((n, heads * LANE), BF16),
        scratch_shapes=scratch,
        compiler_params=_cparams(("parallel", "parallel", "arbitrary")),
        name=("flash_diff" if diff else "flash_mla") + ("_plain" if plain else ""),
    )(*args)


def _band_kernel(qt_ref, k_ref, vt_ref, bias_ref, o_ref, *, tq, nrel):
    i = pl.program_id(2)
    qt = qt_ref[...]

    def attend(rels):
        def keys(rel):
            return pl.ds(pl.multiple_of((i - rel) * tq, tq), tq)

        st = [_dot(k_ref[keys(rel), :], qt) + bias_ref[0, rel] for rel in rels]
        m = functools.reduce(jnp.maximum, [jnp.max(s, axis=0, keepdims=True) for s in st])
        acc = sum(_dot(vt_ref[:, keys(rel)], jnp.exp2(s - m).astype(BF16)) for rel, s in zip(rels, st))
        o_ref[...] = (acc[:C_DIM] / acc[C_DIM:C_DIM + 1]).T.astype(o_ref.dtype)

    for avail in range(nrel):
        cond = (i == avail) if avail < nrel - 1 else (i >= avail)
        pl.when(cond)(functools.partial(attend, tuple(range(avail + 1))))


def _band_bias_kernel(u_ref, o_ref, *, tq, nrel):
    width = u_ref.shape[2]
    rolled = pltpu.roll(jnp.broadcast_to(u_ref[0], (tq, width)), 0, 1, stride=1, stride_axis=0)
    band = rolled[:, tq:tq + nrel * tq]
    k_ch = lax.broadcasted_iota(jnp.int32, band.shape, 0) // CHUNK
    q_ch = lax.broadcasted_iota(jnp.int32, band.shape, 1) // CHUNK
    band = jnp.where((k_ch <= q_ch) & (k_ch >= q_ch - C_PAST_CHUNKS), band, NEG)
    for rel in range(nrel):
        o_ref[0, rel] = band[:, rel * tq:(rel + 1) * tq]


def _band_bias_t(tables, tq, nrel):
    groups = tables.shape[0]
    width = 1
    while width < (nrel + 1) * tq:
        width *= 2
    n_lo = tq - REL_CLIP
    assert n_lo >= 0
    u = jnp.concatenate([jnp.broadcast_to(tables[:, :1], (groups, n_lo)), tables,
                         jnp.broadcast_to(tables[:, -1:], (groups, width - n_lo - tables.shape[1]))], axis=1)
    return pl.pallas_call(
        functools.partial(_band_bias_kernel, tq=tq, nrel=nrel),
        grid=(groups,),
        in_specs=[pl.BlockSpec((1, 1, width), lambda g: (g, 0, 0))],
        out_specs=pl.BlockSpec((1, nrel, tq, tq), lambda g: (g, 0, 0, 0)),
        out_shape=jax.ShapeDtypeStruct((groups, nrel, tq, tq), F32),
        compiler_params=_cparams(("parallel",)),
        name="band_bias",
    )((u * LOG2E)[:, None, :])


def _band(qt, k, vt, bias, *, batch, tq, layer):
    n = k.shape[0]
    seq = n // batch
    nq = seq // tq
    nrel = bias.shape[1]
    return pl.pallas_call(
        functools.partial(_band_kernel, tq=tq, nrel=nrel),
        grid=(batch, C_HEADS, nq),
        in_specs=[pl.BlockSpec((LANE, tq), lambda b, h, i: (h, b * nq + i)),
                  pl.BlockSpec((seq, LANE), lambda b, h, i: (b, h)),
                  pl.BlockSpec((VT_ROWS, seq), lambda b, h, i: (h, b)),
                  pl.BlockSpec((1, nrel, tq, tq), lambda b, h, i: (layer * C_HEADS + h, 0, 0, 0))],
        out_specs=pl.BlockSpec((tq, LANE), lambda b, h, i: (b * nq + i, h)),
        out_shape=jax.ShapeDtypeStruct((n, C_W), BF16),
        compiler_params=_cparams(("parallel", "parallel", "arbitrary")),
        name="band",
    )(qt, k, vt, bias)


def _decode_kernel(*refs, heads, dk, diff, biased, interleaved):
    refs = list(refs)
    q_ref, kc_ref, vc_ref, kn_ref, vn_ref = refs[:5]
    rest = refs[5:]
    if biased:
        bc_ref, bn_ref, mc_ref, mn_ref = rest[:4]
        rest = rest[4:]
    if diff:
        lam_ref, sg_ref = rest[:2]
        rest = rest[2:]
    o_ref = rest[0]
    rows = q_ref.shape[0]
    for h in range(heads):
        qk = slice(h * dk, (h + 1) * dk)
        vv = slice(h * LANE, (h + 1) * LANE)
        q = q_ref[:, qk]
        if diff:
            q = _split_components(q)
        if interleaved:
            past = kc_ref.shape[1] // heads
            k_cache = kc_ref[0, pl.ds(h, past, stride=heads), :]
            v_cache = vc_ref[0, pl.ds(h, past, stride=heads), :]
        else:
            k_cache, v_cache = kc_ref[0, :, qk], vc_ref[0, :, vv]
        s_c = _dot_nt(q, k_cache.astype(BF16))
        s_n = _dot_nt(q, kn_ref[:, qk])
        if biased:
            s_c = jnp.where(mc_ref[...] > 0, s_c + bc_ref[h], NEG)
            s_n = jnp.where(mn_ref[...] > 0, s_n + bn_ref[h], NEG)
        m = jnp.maximum(jnp.max(s_c, axis=-1, keepdims=True), jnp.max(s_n, axis=-1, keepdims=True))
        p_c = jnp.exp(s_c - m)
        p_n = jnp.exp(s_n - m)
        l = jnp.sum(p_c, axis=-1, keepdims=True) + jnp.sum(p_n, axis=-1, keepdims=True)
        o = (_dot(p_c.astype(BF16), v_cache.astype(BF16)) + _dot(p_n.astype(BF16), vn_ref[:, vv])) / l
        if diff:
            o = _diff_out(o, rows, lam_ref, sg_ref)
        o_ref[:, vv] = o.astype(o_ref.dtype)


def _decode(q, kc, vc, kn, vn, *, heads, dk, rows, stream0=0, interleaved=False, diff=False, lam=None, sg=None,
            bias=None):
    n = q.shape[0]
    streams = n // rows
    past = kc.shape[1] // heads if interleaved else kc.shape[1]
    assert not interleaved or dk == LANE
    in_specs = [pl.BlockSpec((rows, heads * dk), lambda b: (b, 0)),
                pl.BlockSpec((1,) + kc.shape[1:], lambda b: (stream0 + b, 0, 0)),
                pl.BlockSpec((1,) + vc.shape[1:], lambda b: (stream0 + b, 0, 0)),
                pl.BlockSpec((rows, heads * dk), lambda b: (b, 0)),
                pl.BlockSpec((rows, heads * LANE), lambda b: (b, 0))]
    args = [q, kc, vc, kn, vn]
    if bias is not None:
        bias_c, bias_n, mask_c, mask_n = bias
        in_specs += [pl.BlockSpec((heads, rows, past), lambda b: (0, 0, 0)),
                     pl.BlockSpec((heads, rows, rows), lambda b: (0, 0, 0)),
                     pl.BlockSpec((rows, past), lambda b: (0, 0)),
                     pl.BlockSpec((rows, rows), lambda b: (0, 0))]
        args += [bias_c, bias_n, mask_c, mask_n]
    if diff:
        in_specs += [pl.BlockSpec((1, LANE), lambda b: (0, 0))] * 2
        args += [lam, sg]
    return pl.pallas_call(
        functools.partial(_decode_kernel, heads=heads, dk=dk, diff=diff, biased=bias is not None,
                          interleaved=interleaved),
        grid=(streams,),
        in_specs=in_specs,
        out_specs=pl.BlockSpec((rows, heads * LANE), lambda b: (b, 0)),
        out_shape=jax.ShapeDtypeStruct((n, heads * LANE), BF16),
        compiler_params=_cparams(("parallel",)),
        name="decode",
    )(*args)


def _decode_mla_kernel(q_ref, ckv_ref, kpe_ref, w_ref, gn_ref, gr_ref, kn_ref, vn_ref, o_ref):
    ckv = ckv_ref[...].astype(BF16)
    kpe2, ss_pe = _mla_rope_key(kpe_ref[...])
    nope_w = B_HEADS * B_NOPE
    for hh in range(B_HEADS):
        qk = slice(hh * B_PAD, (hh + 1) * B_PAD)
        vv = slice(hh * LANE, (hh + 1) * LANE)
        if hh % 2 == 0:
            nope2 = _dot(ckv, w_ref[:, hh * B_NOPE:(hh + 2) * B_NOPE])
            v2 = _dot(ckv, w_ref[:, nope_w + hh * B_V_DIM:nope_w + (hh + 2) * B_V_DIM]).astype(BF16)
        half = slice((hh % 2) * LANE, (hh % 2 + 1) * LANE)
        k_cache = jnp.concatenate(_mla_head_key(hh, nope2[:, half], kpe2, ss_pe, gn_ref[...], gr_ref[...]), axis=1)
        v_cache = v2[:, half]
        q = q_ref[:, qk]
        s_c = _dot_nt(q, k_cache)
        s_n = _dot_nt(q, kn_ref[:, qk])
        m = jnp.maximum(jnp.max(s_c, axis=-1, keepdims=True), jnp.max(s_n, axis=-1, keepdims=True))
        p_c = jnp.exp(s_c - m)
        p_n = jnp.exp(s_n - m)
        l = jnp.sum(p_c, axis=-1, keepdims=True) + jnp.sum(p_n, axis=-1, keepdims=True)
        o = (_dot(p_c.astype(BF16), v_cache) + _dot(p_n.astype(BF16), vn_ref[:, vv])) / l
        o_ref[:, vv] = o.astype(o_ref.dtype)


def _decode_mla(q, ckv, kpe, kn, vn, lw, *, rows, past, stream0):
    n = q.shape[0]

    def new(b):
        return (b, 0)

    def cached(b):
        return (stream0 + b, 0)

    def const(b):
        return (0, 0)

    return pl.pallas_call(
        _decode_mla_kernel,
        grid=(n // rows,),
        in_specs=[pl.BlockSpec((rows, B_HEADS * B_PAD), new),
                  pl.BlockSpec((past, B_KV_LORA), cached), pl.BlockSpec((past, B_ROPE), cached),
                  pl.BlockSpec(lw['w_kvb'].shape, const), pl.BlockSpec((1, LANE), const),
                  pl.BlockSpec((1, LANE), const),
                  pl.BlockSpec((rows, B_HEADS * B_PAD), new), pl.BlockSpec((rows, BV_W), new)],
        out_specs=pl.BlockSpec((rows, BV_W), new),
        out_shape=jax.ShapeDtypeStruct((n, BV_W), BF16),
        compiler_params=_cparams(("parallel",)),
        name="decode_mla",
    )(q, ckv, kpe, lw['w_kvb'], lw['gbkn'], lw['gbkr'], kn, vn)


def _merge_kernel(x_ref, g_ref, oa_ref, ob_ref, oc_ref, wga_ref, wgb_ref, wgc_ref, bga_ref, bgb_ref, bgc_ref,
                  wba_ref, wbb_ref, wbc_ref, wo_ref, out_ref, h_sc):
    j = pl.program_id(1)

    @pl.when(j == 0)
    def _():
        x = x_ref[...]
        h_sc[...] = (_rms(x, x.shape[1]) * g_ref[...]).astype(BF16)
        out_ref[...] = x

    h = h_sc[...]

    def branch(o_ref, wg_ref, bg_ref, wb_ref):
        gate = jax.nn.sigmoid(_dot(h, wg_ref[...]) + bg_ref[...])
        return gate * _dot(o_ref[...], wb_ref[...])

    merged = (branch(oa_ref, wga_ref, bga_ref, wba_ref) + branch(ob_ref, wgb_ref, bgb_ref, wbb_ref)
              + branch(oc_ref, wgc_ref, bgc_ref, wbc_ref))
    out_ref[...] += _dot(merged.astype(BF16), wo_ref[...])


def _merge(x, oa, ob, oc, lw, *, bm, bn):
    n, d_model = x.shape
    nbn = d_model // bn

    def row(i, j):
        return (i, 0)

    def const(i, j):
        return (0, 0)

    def col(k):
        return lambda i, j: (0, k * nbn + j)

    layer = lw['layer']

    def wcol(rows, k):
        return pl.BlockSpec((None, rows, bn), lambda i, j: (layer, 0, k * nbn + j))

    in_specs = [pl.BlockSpec((bm, d_model), row), pl.BlockSpec((1, d_model), const),
                pl.BlockSpec((bm, AV_W), row), pl.BlockSpec((bm, BV_W), row), pl.BlockSpec((bm, C_W), row),
                wcol(d_model, 0), wcol(d_model, 1), wcol(d_model, 2),
                pl.BlockSpec((1, bn), col(0)), pl.BlockSpec((1, bn), col(1)), pl.BlockSpec((1, bn), col(2)),
                wcol(AV_W, 0), wcol(BV_W, 0), wcol(C_W, 0),
                pl.BlockSpec((None, bn, d_model), lambda i, j: (layer, j, 0))]
    return pl.pallas_call(
        _merge_kernel,
        grid=(n // bm, nbn),
        in_specs=in_specs,
        out_specs=pl.BlockSpec((bm, d_model), row),
        out_shape=jax.ShapeDtypeStruct((n, d_model), F32),
        scratch_shapes=[pltpu.VMEM((bm, d_model), BF16)],
        compiler_params=_cparams(("parallel", "arbitrary")),
        name="merge",
    )(x, lw['g_attn'], oa, ob, oc, lw['w_g'], lw['w_g'], lw['w_g'], lw['b_g'], lw['b_g'], lw['b_g'],
      lw['w_br_a'], lw['w_br_b'], lw['w_br_c'], lw['w_o'])


def _mlp_kernel(x_ref, g_ref, wu_ref, wd_ref, out_ref, h_sc):
    j = pl.program_id(1)

    @pl.when(j == 0)
    def _():
        x = x_ref[...]
        h_sc[...] = (_rms(x, x.shape[1]) * g_ref[...]).astype(BF16)
        out_ref[...] = x

    u = jnp.maximum(_dot(h_sc[...], wu_ref[...]), 0.0)
    out_ref[...] += _dot((u * u).astype(BF16), wd_ref[...])


def _mlp(x, lw, *, bm, bf):
    n, d_model = x.shape
    d_ff = lw['w_up'].shape[2]
    layer = lw['layer']
    return pl.pallas_call(
        _mlp_kernel,
        grid=(n // bm, d_ff // bf),
        in_specs=[pl.BlockSpec((bm, d_model), lambda i, j: (i, 0)), pl.BlockSpec((1, d_model), lambda i, j: (0, 0)),
                  pl.BlockSpec((None, d_model, bf), lambda i, j: (layer, 0, j)),
                  pl.BlockSpec((None, bf, d_model), lambda i, j: (layer, j, 0))],
        out_specs=pl.BlockSpec((bm, d_model), lambda i, j: (i, 0)),
        out_shape=jax.ShapeDtypeStruct((n, d_model), F32),
        scratch_shapes=[pltpu.VMEM((bm, d_model), BF16)],
        compiler_params=_cparams(("parallel", "arbitrary")),
        name="mlp",
    )(x, lw['g_mlp'], lw['w_up'], lw['w_down'])


def _rope_tables(pos, rot):
    half = rot // 2
    inv = jnp.power(ROPE_THETA, -jnp.arange(half, dtype=F32) / half)
    ang = pos.astype(F32)[:, None] * inv[None, :]
    cos, sin = jnp.cos(ang), jnp.sin(ang)
    npos = pos.shape[0]
    one = jnp.ones((npos, 64 - rot), F32)
    c64 = jnp.concatenate([cos, cos, one], axis=1)
    s1 = jnp.concatenate([-sin, jnp.zeros((npos, 64 - half), F32)], axis=1)
    s2 = jnp.concatenate([jnp.zeros((npos, half), F32), sin, jnp.zeros((npos, 64 - rot), F32)], axis=1)
    return [jnp.tile(t, (1, 2)) for t in (c64, s1, s2)]


def _score_bound(gq, gk, d):
    return math.sqrt(d) * LOG2E * jnp.max(jnp.abs(gq)) * jnp.max(jnp.abs(gk))


def _layer_weights(l, p, stacks):
    d_model = p['w_in'].shape[1]
    w_in = p['w_in'][l]
    o = [0]
    for width in (A_W, A_W, AV_W, B_Q_LORA, B_KV_LORA + B_ROPE, C_W, C_W, C_W):
        o.append(o[-1] + width)
    cols = lambda a, b: w_in[:, a:b]
    w_a = jnp.concatenate([
        cols(o[0], o[1]), cols(o[1], o[2]), cols(o[2], o[3]), cols(o[3], o[4]),
        cols(o[4], o[4] + B_KV_LORA), cols(o[5], o[6]), cols(o[6], o[7]), cols(o[7], o[8]),
        cols(o[4] + B_KV_LORA, o[5]), jnp.zeros((d_model, LANE - B_ROPE), F32)], axis=1).astype(BF16)
    w_qb = p['w_qb'][l].reshape(B_Q_LORA, B_HEADS, B_QK)
    w_qb = jnp.concatenate([w_qb[:, :, :B_NOPE].reshape(B_Q_LORA, -1),
                            w_qb[:, :, B_NOPE:].reshape(B_Q_LORA, -1)], axis=1).astype(BF16)
    w_kvb = p['w_kvb'][l].reshape(B_KV_LORA, B_HEADS, B_NOPE + B_V_DIM)
    w_kvb = jnp.concatenate([w_kvb[:, :, :B_NOPE].reshape(B_KV_LORA, -1),
                             w_kvb[:, :, B_NOPE:].reshape(B_KV_LORA, -1)], axis=1).astype(BF16)
    lam_init = 0.8 - 0.6 * math.exp(-0.3 * l)
    lf = p['a_lambda'][l].astype(F32)
    lam = jnp.exp(jnp.sum(lf[0] * lf[1])) - jnp.exp(jnp.sum(lf[2] * lf[3])) + lam_init

    def tiled(g, reps, scale=1.0):
        return jnp.tile(g * scale, reps)[None, :]

    bqn = p['b_q_norm'][l]
    bkn = p['b_k_norm'][l]
    return {
        'g_attn': p['attn_norm'][l][None, :],
        'g_mlp': p['mlp_norm'][l][None, :],
        'w_a': w_a,
        'b_g': p['b_gate'][l][None, :],
        'w_qb': w_qb,
        'w_kvb': w_kvb,
        'gaq': tiled(p['a_q_norm'][l], 2 * A_HEADS, A_QK_DIM ** -0.5),
        'gak': tiled(p['a_k_norm'][l], 2 * A_HEADS),
        'gbqa': p['b_qa_norm'][l][None, :],
        'gbkv': p['b_kv_norm'][l][None, :],
        'gbqn': tiled(bqn[:B_NOPE], 1, B_QK ** -0.5),
        'gbqr': tiled(bqn[B_NOPE:], 2, B_QK ** -0.5),
        'gaq_flash': tiled(p['a_q_norm'][l], 2 * A_HEADS, A_QK_DIM ** -0.5 * LOG2E),
        'gbqn_flash': tiled(bqn[:B_NOPE], 1, B_QK ** -0.5 * LOG2E),
        'gbqr_flash': tiled(bqn[B_NOPE:], 2, B_QK ** -0.5 * LOG2E),
        'gcq_flash': tiled(p['c_q_norm'][l], C_HEADS, C_DIM ** -0.5 * LOG2E),
        'gbkn': tiled(bkn[:B_NOPE], 1),
        'gbkr': tiled(bkn[B_NOPE:], 2),
        'gcq': tiled(p['c_q_norm'][l], C_HEADS, C_DIM ** -0.5),
        'gck': tiled(p['c_k_norm'][l], C_HEADS),
        'lam': jnp.broadcast_to(lam, (1, LANE)).astype(F32),
        'sg': (p['a_sub_norm'][l] * (1.0 - lam_init))[None, :],
        'bound_a': _score_bound(p['a_q_norm'][l], p['a_k_norm'][l], A_QK_DIM),
        'bound_b': _score_bound(bqn, bkn, B_QK),
        'layer': l,
        **stacks,
    }


def _stacked_weights(p):
    gate_off = p['w_in'].shape[2] - 3 * p['w_in'].shape[1]
    stacks = {name: p[name].astype(BF16) for name in ('w_br_a', 'w_br_b', 'w_br_c', 'w_o', 'w_up', 'w_down')}
    stacks['w_g'] = p['w_in'][:, :, gate_off:].astype(BF16)
    return stacks


def _pick(n, prefs):
    for b in prefs:
        if n % b == 0:
            return b
    raise ValueError(f"no block size in {prefs} divides {n}")


def kernel(x_prompt, x_sample, cache_a_k, cache_a_v, cache_b_ckv, cache_b_kpe, cache_c_k, cache_c_v,
           attn_norm, w_in, b_gate, a_q_norm, a_k_norm, a_lambda, a_sub_norm,
           b_qa_norm, b_kv_norm, w_qb, w_kvb, b_q_norm, b_k_norm,
           c_q_norm, c_k_norm, c_rel_bias, w_br_a, w_br_b, w_br_c, w_o,
           mlp_norm, w_up, w_down):
    p = dict(attn_norm=attn_norm, w_in=w_in, b_gate=b_gate, a_q_norm=a_q_norm, a_k_norm=a_k_norm,
             a_lambda=a_lambda, a_sub_norm=a_sub_norm, b_qa_norm=b_qa_norm, b_kv_norm=b_kv_norm,
             w_qb=w_qb, w_kvb=w_kvb, b_q_norm=b_q_norm, b_k_norm=b_k_norm, c_q_norm=c_q_norm,
             c_k_norm=c_k_norm, w_br_a=w_br_a, w_br_b=w_br_b, w_br_c=w_br_c, w_o=w_o, mlp_norm=mlp_norm,
             w_up=w_up, w_down=w_down)
    depth = w_in.shape[0]
    stacks = _stacked_weights(p)
    batch, seq, d_model = x_prompt.shape
    streams, dec, _ = x_sample.shape
    past = cache_a_k.shape[2]
    c_past = cache_c_k.shape[2]
    c_keep = min(C_WINDOW, seq)
    n_p, n_s = batch * seq, streams * dec
    assert seq % CHUNK == 0 and c_keep % CHUNK == 0

    pos_p = jnp.arange(seq)
    pos_s = past + jnp.arange(dec)
    tabs_p = _rope_tables(pos_p, A_ROT) + _rope_tables(pos_p, B_ROPE)
    tabs_s = [jnp.tile(t, (streams, 1)) for t in _rope_tables(pos_s, A_ROT) + _rope_tables(pos_s, B_ROPE)]

    kc_pos = jnp.concatenate([past - c_past + jnp.arange(c_past), pos_s])
    q_ch, k_ch = pos_s // CHUNK, kc_pos // CHUNK
    c_mask = ((k_ch[None, :] <= q_ch[:, None]) & (k_ch[None, :] >= q_ch[:, None] - C_PAST_CHUNKS)).astype(F32)
    rel_s = jnp.clip(pos_s[:, None] - kc_pos[None, :], -REL_CLIP, REL_CLIP) + REL_CLIP

    bm_in = _pick(seq, (512, 256, 128, 64))
    bm_tok = _pick(seq, (512, 256, 128, 64))
    t_a = _pick(seq, (512, 256, 128))
    t_b = _pick(seq, (1024, 512, 256, 128))
    tk_a, tk_b = t_a, t_b // 2
    tp_a = (t_b, t_b // 2)
    tp_b = (t_b, t_b // 2)
    t_c = _pick(seq, (512, 256, 128))
    nrel_c = C_WINDOW // t_c + 1
    band_bias = _band_bias_t(c_rel_bias.reshape(depth * C_HEADS, 2 * REL_CLIP + 1), t_c, nrel_c)
    bm_wide = _pick(seq, (1024, 512, 256, 128, 64))
    bn_merge = 512
    bf_mlp = 512

    ca_k = cache_a_k.reshape(depth * streams, past * A_HEADS, LANE)
    ca_v = cache_a_v.reshape(depth * streams, past * A_HEADS, LANE)
    cb_ckv = cache_b_ckv.reshape(depth * streams * past, B_KV_LORA)
    cb_kpe = cache_b_kpe.reshape(depth * streams * past, B_ROPE)
    cc_k = cache_c_k.reshape(depth * streams, c_past * C_HEADS, LANE)
    cc_v = cache_c_v.reshape(depth * streams, c_past * C_HEADS, LANE)

    xp = x_prompt.reshape(n_p, d_model)
    xs = x_sample.reshape(n_s, d_model)
    outs = [[] for _ in range(12)]
    stacked = None
    for l in range(depth):
        lw = _layer_weights(l, p, stacks)

        (qa, kaf, kab, vaf, vab, qb, ckv, kpe, qc, kcb, vcb, kcf, vcf) = _in_proj(
            xp, lw, tabs_p, bm=bm_in, seg_rows=seq, keep=c_keep, transposed=True, stack=(l, depth, stacked))
        stacked = (kaf, vaf, ckv, kpe)
        kb, vb = _mla_kv(ckv, kpe, lw, bm=bm_tok, n=n_p, row0=l * n_p, transposed=True)
        flash_a = functools.partial(_flash, qa, kab, vab, batch=batch, heads=A_HEADS, dk=LANE, diff=True,
                                    lam=lw['lam'], sg=lw['sg'])
        flash_b = functools.partial(_flash, qb, kb, vb, batch=batch, heads=B_HEADS, dk=B_PAD, diff=False)
        oa = lax.cond(lw['bound_a'] <= FLASH_PLAIN_MAX_LOG2,
                      functools.partial(flash_a, tq=tp_a[0], tk=tp_a[1], plain=True, group=2),
                      functools.partial(flash_a, tq=t_a, tk=tk_a, plain=False))
        ob = lax.cond(lw['bound_b'] <= FLASH_PLAIN_MAX_LOG2,
                      functools.partial(flash_b, tq=tp_b[0], tk=tp_b[1], plain=True, group=4),
                      functools.partial(flash_b, tq=t_b, tk=tk_b, plain=False))
        oc = _band(qc, kcb, vcb, band_bias, batch=batch, tq=t_c, layer=l)
        xp = _merge(xp, oa, ob, oc, lw, bm=bm_tok, bn=bn_merge)
        xp = _mlp(xp, lw, bm=bm_wide, bf=bf_mlp)
        for dst, val in zip(outs[4:6], (kcf, vcf)):
            dst.append(val)

        (qa, kaf, kab, vaf, vab, qb, ckv, kpe, qc, kcb, vcb, kcf, vcf) = _in_proj(
            xs, lw, tabs_s, bm=n_s, seg_rows=n_s, keep=n_s, transposed=False)
        kb, vb = _mla_kv(ckv, kpe, lw, bm=n_s)
        oa = _decode(qa, ca_k, ca_v, kab, vab, heads=A_HEADS, dk=LANE, rows=dec, stream0=l * streams,
                     interleaved=True, diff=True, lam=lw['lam'], sg=lw['sg'])
        ob = _decode_mla(qb, cb_ckv, cb_kpe, kb, vb, lw, rows=dec, past=past, stream0=l * streams)
        bias_s = c_rel_bias[l][:, rel_s]
        oc = _decode(qc, cc_k, cc_v, kcb, vcb, heads=C_HEADS, dk=LANE, rows=dec, stream0=l * streams,
                     interleaved=True, bias=(bias_s[:, :, :c_past], bias_s[:, :, c_past:], c_mask[:, :c_past], c_mask[:, c_past:]))
        xs = _merge(xs, oa, ob, oc, lw, bm=n_s, bn=bn_merge)
        xs = _mlp(xs, lw, bm=n_s, bf=bf_mlp)
        for dst, val in zip(outs[6:], (kaf, vaf, ckv, kpe, kcf, vcf)):
            dst.append(val)

    st = list(stacked) + [jnp.stack(o) for o in outs[4:]]
    return (xp.reshape(batch, seq, d_model), xs.reshape(streams, dec, d_model),
            st[0].reshape(depth, batch, seq, A_HEADS, 2 * A_QK_DIM),
            st[1].reshape(depth, batch, seq, A_HEADS, A_V_DIM),
            st[2].reshape(depth, batch, seq, B_KV_LORA),
            st[3].reshape(depth, batch, seq, B_ROPE),
            st[4].reshape(depth, batch, c_keep, C_HEADS, C_DIM),
            st[5].reshape(depth, batch, c_keep, C_HEADS, C_DIM),
            st[6].reshape(depth, streams, dec, A_HEADS, 2 * A_QK_DIM),
            st[7].reshape(depth, streams, dec, A_HEADS, A_V_DIM),
            st[8].reshape(depth, streams, dec, B_KV_LORA),
            st[9].reshape(depth, streams, dec, B_ROPE),
            st[10].reshape(depth, streams, dec, C_HEADS, C_DIM),
            st[11].reshape(depth, streams, dec, C_HEADS, C_DIM))
```

```python
import functools
import math

import jax
import jax.numpy as jnp
from jax import lax
from jax.experimental import pallas as pl
from jax.experimental.pallas import tpu as pltpu

F32 = jnp.float32
BF16 = jnp.bfloat16

CHUNK = 64
ROPE_THETA = 500000.0
EPS = 1e-6
NEG = -1e30
LOG2E = math.log2(math.e)
FLASH_PLAIN_MAX_LOG2 = 50.0
LANE = 128

A_HEADS = 4
A_QK_DIM = 64
A_V_DIM = 128
A_ROT = A_QK_DIM // 4
B_HEADS = 8
B_Q_LORA = 512
B_KV_LORA = 256
B_NOPE = 128
B_ROPE = 64
B_V_DIM = 128
B_QK = B_NOPE + B_ROPE
B_PAD = 2 * LANE
BF16_SUBLANES = 16
VT_ROWS = LANE + BF16_SUBLANES
C_HEADS = 4
C_DIM = 128
C_PAST_CHUNKS = 8
C_WINDOW = C_PAST_CHUNKS * CHUNK
REL_CLIP = 128

A_W = A_HEADS * 2 * A_QK_DIM
AV_W = A_HEADS * A_V_DIM
BV_W = B_HEADS * B_V_DIM
C_W = C_HEADS * C_DIM

OFF_AQ = 0
OFF_AK = OFF_AQ + A_W
OFF_AV = OFF_AK + A_W
OFF_BQ = OFF_AV + AV_W
OFF_CKV = OFF_BQ + B_Q_LORA
OFF_CQ = OFF_CKV + B_KV_LORA
OFF_CK = OFF_CQ + C_W
OFF_CV = OFF_CK + C_W
OFF_KPE = OFF_CV + C_W
WA_COLS = OFF_KPE + LANE

VMEM_LIMIT = 56 * 1024 * 1024


VMEM_LIMIT_IN_PROJ = 60 * 1024 * 1024


def _cparams(sem, vmem=VMEM_LIMIT):
    return pltpu.CompilerParams(dimension_semantics=sem, vmem_limit_bytes=vmem)


def _rms(x, width):
    ms = jnp.sum(x * x, axis=-1, keepdims=True) * (1.0 / width)
    return x * lax.rsqrt(ms + EPS)


def _rms_half(zc, lane_lo):
    sq = zc * zc
    lo = jnp.sum(jnp.where(lane_lo, sq, 0.0), axis=-1, keepdims=True)
    hi = jnp.sum(jnp.where(lane_lo, 0.0, sq), axis=-1, keepdims=True)
    ms = jnp.where(lane_lo, lo, hi) * (1.0 / 64.0)
    return zc * lax.rsqrt(ms + EPS)


def _rope(y, c, s1, s2, half):
    return y * c + pltpu.roll(y, LANE - half, 1) * s1 + pltpu.roll(y, half, 1) * s2


def _put_values_t(ref, head, val):
    base = head * VT_ROWS
    ref[base:base + LANE, :] = val.T.astype(BF16)
    ref[base + LANE:base + VT_ROWS, :] = jnp.ones((VT_ROWS - LANE, val.shape[0]), BF16)


def _dot(a, b):
    return jnp.dot(a, b, preferred_element_type=F32)


def _dot_nt(a, b):
    return lax.dot_general(a, b, (((1,), (1,)), ((), ())), preferred_element_type=F32)


def _in_proj_kernel(*refs, transposed, n_alias, interleaved):
    (x_ref, g_ref, wa_ref, wqb_ref, gaq_ref, gak_ref, gbqa_ref, gbkv_ref, gbqn_ref, gbqr_ref,
     gcq_ref, gck_ref, ca_ref, s1a_ref, s2a_ref, cb_ref, s1b_ref, s2b_ref) = refs[:18]
    (qa_ref, kaf_ref, kab_ref, vaf_ref, vab_ref, qb_ref, ckv_ref, kpe_ref,
     qc_ref, kcb_ref, vcb_ref, kcf_ref, vcf_ref) = refs[18 + n_alias:]
    d_model = x_ref.shape[1]

    def put(ref, c, val):
        if transposed:
            ref[c * LANE:(c + 1) * LANE, :] = val.T.astype(BF16)
        else:
            ref[:, c * LANE:(c + 1) * LANE] = val.astype(BF16)

    def put_cache(ref, head, val):
        if interleaved:
            ref[pl.ds(head, val.shape[0], stride=A_HEADS), :] = val
        else:
            ref[:, head * LANE:(head + 1) * LANE] = val

    h = (_rms(x_ref[...], d_model) * g_ref[...]).astype(BF16)
    lane_lo = lax.broadcasted_iota(jnp.int32, (1, LANE), 1) < 64
    ca, s1a, s2a = ca_ref[...], s1a_ref[...], s2a_ref[...]
    cb, s1b, s2b = cb_ref[...], s1b_ref[...], s2b_ref[...]

    def proj(off, width):
        return _dot(h, wa_ref[:, off:off + width])

    def chunk(c):
        return slice(c * LANE, (c + 1) * LANE)

    z = proj(OFF_AQ, A_W)
    for c in range(A_W // LANE):
        y = _rms_half(z[:, chunk(c)], lane_lo) * gaq_ref[:, chunk(c)]
        put(qa_ref, c, _rope(y, ca, s1a, s2a, A_ROT // 2))
    z = proj(OFF_AK, A_W)
    for c in range(A_W // LANE):
        y = _rms_half(z[:, chunk(c)], lane_lo) * gak_ref[:, chunk(c)]
        y = _rope(y, ca, s1a, s2a, A_ROT // 2)
        put_cache(kaf_ref, c, y)
        kab_ref[:, chunk(c)] = y.astype(BF16)
    z = proj(OFF_AV, AV_W)
    for c in range(A_HEADS):
        put_cache(vaf_ref, c, z[:, chunk(c)])
        if transposed:
            _put_values_t(vab_ref, c, z[:, chunk(c)])
        else:
            put(vab_ref, c, z[:, chunk(c)])

    z = proj(OFF_BQ, B_Q_LORA)
    cq = (_rms(z, B_Q_LORA) * gbqa_ref[...]).astype(BF16)
    qraw = _dot(cq, wqb_ref[...])
    nope_w = B_HEADS * B_NOPE
    roped = [_rope(qraw[:, nope_w + c * LANE:nope_w + (c + 1) * LANE], cb, s1b, s2b, B_ROPE // 2)
             for c in range(B_HEADS * B_ROPE // LANE)]
    for hh in range(B_HEADS):
        nope = qraw[:, hh * B_NOPE:(hh + 1) * B_NOPE]
        in_half = lane_lo if hh % 2 == 0 else jnp.logical_not(lane_lo)
        rsel = jnp.where(in_half, roped[hh // 2], 0.0)
        ss = jnp.sum(nope * nope, axis=-1, keepdims=True) + jnp.sum(rsel * rsel, axis=-1, keepdims=True)
        inv = lax.rsqrt(ss * (1.0 / B_QK) + EPS)
        put(qb_ref, 2 * hh, nope * inv * gbqn_ref[...])
        put(qb_ref, 2 * hh + 1, rsel * inv * gbqr_ref[...])
    z = proj(OFF_CKV, B_KV_LORA)
    ckv_ref[...] = _rms(z, B_KV_LORA) * gbkv_ref[...]
    z = proj(OFF_KPE, LANE)
    kpe_ref[...] = _rope(z, cb, s1b, s2b, B_ROPE // 2)[:, :B_ROPE]

    z = proj(OFF_CQ, C_W)
    for c in range(C_HEADS):
        put(qc_ref, c, _rms(z[:, chunk(c)], C_DIM) * gcq_ref[:, chunk(c)])
    z = proj(OFF_CK, C_W)
    for c in range(C_HEADS):
        y = _rms(z[:, chunk(c)], C_DIM) * gck_ref[:, chunk(c)]
        kcf_ref[:, chunk(c)] = y
        kcb_ref[:, chunk(c)] = y.astype(BF16)
    z = proj(OFF_CV, C_W)
    vcf_ref[...] = z
    for c in range(C_HEADS):
        if transposed:
            _put_values_t(vcb_ref, c, z[:, chunk(c)])
        else:
            put(vcb_ref, c, z[:, chunk(c)])


STACKED_OUTS = (1, 3, 6, 7)
INTERLEAVED_OUTS = (1, 3)


def _in_proj(x, lw, tabs, *, bm, seg_rows, keep, transposed, stack=None):
    n, d_model = x.shape
    nb = n // bm
    nbs = seg_rows // bm
    nbk = keep // bm
    ntab = tabs[0].shape[0] // bm

    def row(i):
        return (i, 0)

    def const(i):
        return (0, 0)

    def tab(i):
        return (i % ntab, 0)

    def kept(i):
        return ((i // nbs) * nbk + jnp.maximum(i % nbs - (nbs - nbk), 0), 0)

    def rows(width):
        return pl.BlockSpec((bm, width), row)

    def full(a):
        return pl.BlockSpec(a.shape, const, pipeline_mode=pl.Buffered(1))

    fl = '_flash' if transposed else ''
    gains = [lw['gaq' + fl], lw['gak'], lw['gbqa'], lw['gbkv'], lw['gbqn' + fl], lw['gbqr' + fl],
             lw['gcq' + fl], lw['gck']]
    in_specs = ([rows(d_model), full(lw['g_attn']), full(lw['w_a']), full(lw['w_qb'])]
                + [full(g) for g in gains]
                + [pl.BlockSpec((bm, LANE), tab) for _ in tabs])
    out_shape = [
        jax.ShapeDtypeStruct((n, A_W), BF16),
        jax.ShapeDtypeStruct((n, A_W), F32),
        jax.ShapeDtypeStruct((n, A_W), BF16),
        jax.ShapeDtypeStruct((n, AV_W), F32),
        jax.ShapeDtypeStruct((n, AV_W), BF16),
        jax.ShapeDtypeStruct((n, B_HEADS * B_PAD), BF16),
        jax.ShapeDtypeStruct((n, B_KV_LORA), F32),
        jax.ShapeDtypeStruct((n, B_ROPE), F32),
        jax.ShapeDtypeStruct((n, C_W), BF16),
        jax.ShapeDtypeStruct((n, C_W), BF16),
        jax.ShapeDtypeStruct((n, C_W), BF16),
        jax.ShapeDtypeStruct((n // seg_rows * keep, C_W), F32),
        jax.ShapeDtypeStruct((n // seg_rows * keep, C_W), F32),
    ]
    out_specs = [rows(s.shape[1]) for s in out_shape[:11]] + [pl.BlockSpec((bm, C_W), kept)] * 2
    if transposed:
        for k, width in ((0, A_W), (4, A_HEADS * VT_ROWS), (5, B_HEADS * B_PAD), (8, C_W),
                         (10, C_HEADS * VT_ROWS)):
            out_shape[k] = jax.ShapeDtypeStruct((width, n), BF16)
            out_specs[k] = pl.BlockSpec((width, bm), lambda i: (0, i))
    args = [x, lw['g_attn'], lw['w_a'], lw['w_qb'], *gains, *tabs]
    aliases = {}
    if stack is not None:
        layer, depth, buffers = stack
        for k in STACKED_OUTS:
            width = out_shape[k].shape[1]
            split = width // LANE if k in INTERLEAVED_OUTS else 1
            out_shape[k] = jax.ShapeDtypeStruct((depth * n * split, width // split), F32)
            out_specs[k] = pl.BlockSpec((bm * split, width // split), lambda i: (layer * nb + i, 0))
        if buffers is not None:
            for pos, k in enumerate(STACKED_OUTS):
                aliases[len(args)] = k
                args.append(buffers[pos])
                in_specs.append(pl.BlockSpec(memory_space=pl.ANY))
    return pl.pallas_call(
        functools.partial(_in_proj_kernel, transposed=transposed, n_alias=len(aliases),
                          interleaved=stack is not None),
        grid=(nb,),
        in_specs=in_specs,
        out_specs=out_specs,
        out_shape=out_shape,
        input_output_aliases=aliases,
        compiler_params=_cparams(("arbitrary",), VMEM_LIMIT_IN_PROJ),
        name="in_proj",
    )(*args)


def _mla_rope_key(kpe):
    return jnp.concatenate([kpe, kpe], axis=1), jnp.sum(kpe * kpe, axis=-1, keepdims=True)


def _mla_head_key(hh, nope, kpe2, ss_pe, gn, gr):
    ss = jnp.sum(nope * nope, axis=-1, keepdims=True) + ss_pe
    inv = lax.rsqrt(ss * (1.0 / B_QK) + EPS)
    lane_lo = lax.broadcasted_iota(jnp.int32, (1, LANE), 1) < 64
    in_half = lane_lo if hh % 2 == 0 else jnp.logical_not(lane_lo)
    return (nope * inv * gn).astype(BF16), jnp.where(in_half, kpe2 * inv * gr, 0.0).astype(BF16)


def _mla_kv_kernel(ckv_ref, kpe_ref, w_ref, gn_ref, gr_ref, kb_ref, vb_ref, *, transposed):
    ckv = ckv_ref[...]
    kv = _dot(ckv.astype(BF16), w_ref[...])
    kpe2, ss_pe = _mla_rope_key(kpe_ref[...])
    nope_w = B_HEADS * B_NOPE
    for hh in range(B_HEADS):
        k_nope, k_rope = _mla_head_key(hh, kv[:, hh * B_NOPE:(hh + 1) * B_NOPE], kpe2, ss_pe,
                                       gn_ref[...], gr_ref[...])
        kb_ref[:, hh * B_PAD:hh * B_PAD + LANE] = k_nope
        kb_ref[:, hh * B_PAD + LANE:(hh + 1) * B_PAD] = k_rope
    if transposed:
        for c in range(B_HEADS):
            _put_values_t(vb_ref, c, kv[:, nope_w + c * LANE:nope_w + (c + 1) * LANE])
    else:
        vb_ref[...] = kv[:, nope_w:].astype(BF16)


def _mla_kv(ckv, kpe, lw, *, bm, n=None, row0=0, transposed=False):
    n = ckv.shape[0] if n is None else n
    blk0 = row0 // bm
    assert row0 % bm == 0
    if transposed:
        vb_shape, vb_spec = (B_HEADS * VT_ROWS, n), pl.BlockSpec((B_HEADS * VT_ROWS, bm), lambda i: (0, i))
    else:
        vb_shape, vb_spec = (n, BV_W), pl.BlockSpec((bm, BV_W), lambda i: (i, 0))

    def row(i):
        return (i, 0)

    def src_row(i):
        return (blk0 + i, 0)

    def const(i):
        return (0, 0)

    return pl.pallas_call(
        functools.partial(_mla_kv_kernel, transposed=transposed),
        grid=(n // bm,),
        in_specs=[pl.BlockSpec((bm, B_KV_LORA), src_row), pl.BlockSpec((bm, B_ROPE), src_row),
                  pl.BlockSpec(lw['w_kvb'].shape, const), pl.BlockSpec((1, LANE), const),
                  pl.BlockSpec((1, LANE), const)],
        out_specs=[pl.BlockSpec((bm, B_HEADS * B_PAD), row), vb_spec],
        out_shape=[jax.ShapeDtypeStruct((n, B_HEADS * B_PAD), BF16), jax.ShapeDtypeStruct(vb_shape, BF16)],
        compiler_params=_cparams(("arbitrary",)),
        name="mla_kv",
    )(ckv, kpe, lw['w_kvb'], lw['gbkn'], lw['gbkr'])


def _split_components(q):
    lane_lo = lax.broadcasted_iota(jnp.int32, (1, LANE), 1) < 64
    zero = jnp.zeros_like(q)
    return jnp.concatenate([jnp.where(lane_lo, q, zero), jnp.where(lane_lo, zero, q)], axis=0)


def _diff_out(o, rows, lam_ref, sg_ref):
    o = o[:rows] - lam_ref[...] * o[rows:]
    return _rms(o, A_V_DIM) * sg_ref[...]


def _flash_queries(qt_ref, diff):
    qt = qt_ref[...]
    if diff:
        row_lo = lax.broadcasted_iota(jnp.int32, (qt.shape[0], 1), 0) < A_QK_DIM
        zero = jnp.zeros_like(qt)
        qt = jnp.concatenate([jnp.where(row_lo, qt, zero), jnp.where(row_lo, zero, qt)], axis=1)
    return qt


def _flash_finish(acc, o_ref, lam_ref, sg_ref, tq, diff):
    ot = acc[:A_V_DIM] / acc[A_V_DIM:A_V_DIM + 1]
    if diff:
        o = (ot[:, :tq] - lam_ref[:, :1] * ot[:, tq:]).T
        o = _rms(o, A_V_DIM) * sg_ref[...]
    else:
        o = ot.T
    o_ref[...] = o.astype(o_ref.dtype)


def _flash_plain_kernel(*refs, tq, tk, diff, group):
    if diff:
        qt_ref, k_ref, vt_ref, mask_ref, lam_ref, sg_ref, o_ref, acc_sc = refs
    else:
        qt_ref, k_ref, vt_ref, mask_ref, o_ref, acc_sc = refs
        lam_ref = sg_ref = None
    i = pl.program_id(2)
    qt = _flash_queries(qt_ref, diff)
    per_q = tq // tk

    mq = qt.shape[1]

    def block(j, diag):
        start = pl.multiple_of(j * tk, tk)
        if diag is None:
            pt = jnp.exp2(_dot(k_ref[pl.ds(start, tk), :], qt))
            return _dot(vt_ref[:, pl.ds(start, tk)], pt.astype(BF16))
        spans = [(c0 + diag * tk, c0 + tq) for c0 in range(0, mq, tq)]

        def cols(x):
            return x if diag == 0 else jnp.concatenate([x[:, a:b] for a, b in spans], axis=1)

        pt = jnp.exp2(_dot(k_ref[pl.ds(start, tk), :], cols(qt)))
        pt = jnp.where(cols(mask_ref[diag]) > 0, pt, 0.0)
        pv = _dot(vt_ref[:, pl.ds(start, tk)], pt.astype(BF16))
        if diag == 0:
            return pv
        pieces, off = [], 0
        for a, b in spans:
            pieces += [jnp.zeros((pv.shape[0], diag * tk), F32), pv[:, off:off + b - a]]
            off += b - a
        return jnp.concatenate(pieces, axis=1)

    acc_sc[...] = jnp.zeros(acc_sc.shape, F32)

    def add_blocks(j0, nblocks):
        acc = acc_sc[...]
        for jj in range(nblocks):
            acc = acc + block(j0 + jj, None)
        acc_sc[...] = acc

    def trips(jb, carry):
        add_blocks(jb * (group * per_q), group * per_q)
        return carry

    def leftover(r, carry):
        add_blocks((i // group * group + r) * per_q, per_q)
        return carry

    lax.fori_loop(0, i // group, trips, 0)
    if group > 1:
        lax.fori_loop(0, i % group, leftover, 0)
    acc = acc_sc[...]
    for jj in range(per_q):
        acc = acc + block(i * per_q + jj, jj)
    _flash_finish(acc, o_ref, lam_ref, sg_ref, tq, diff)


def _flash_kernel(*refs, tq, tk, diff):
    if diff:
        qt_ref, k_ref, vt_ref, mask_ref, lam_ref, sg_ref, o_ref, m_sc, a_sc, acc_sc, s_sc, p_sc = refs
    else:
        qt_ref, k_ref, vt_ref, mask_ref, o_ref, m_sc, a_sc, acc_sc, s_sc, p_sc = refs
        lam_ref = sg_ref = None
    i = pl.program_id(2)
    qt = _flash_queries(qt_ref, diff)
    per_q = tq // tk

    def scores(j):
        start = pl.multiple_of(j * tk, tk)
        return _dot(k_ref[pl.ds(start, tk), :], qt)

    def softmax(st, m_prev, diag):
        if diag is not None:
            st = jnp.where(mask_ref[diag] > 0, st, NEG)
        m_new = jnp.maximum(m_prev, jnp.max(st, axis=0, keepdims=True))
        return m_new, jnp.exp2(m_prev - m_new), jnp.exp2(st - m_new).astype(BF16)

    def weighted_values(j, acc, alpha, pt):
        start = pl.multiple_of(j * tk, tk)
        return alpha * acc + _dot(vt_ref[:, pl.ds(start, tk)], pt)

    m_sc[...] = jnp.full(m_sc.shape, NEG, F32)
    a_sc[...] = jnp.ones(a_sc.shape, F32)
    acc_sc[...] = jnp.zeros(acc_sc.shape, F32)
    p_sc[...] = jnp.zeros(p_sc.shape, BF16)
    s_sc[...] = scores(0)

    def body(jb, carry):
        m, alpha_prev, acc, p_prev, st = m_sc[...], a_sc[...], acc_sc[...], p_sc[...], s_sc[...]
        for jj in range(per_q):
            j = jb * per_q + jj
            st_next = scores(j + 1)
            m, alpha, pt = softmax(st, m, None)
            acc = weighted_values(jnp.maximum(j - 1, 0), acc, alpha_prev, p_prev)
            alpha_prev, p_prev, st = alpha, pt, st_next
        m_sc[...], a_sc[...], acc_sc[...], p_sc[...], s_sc[...] = m, alpha_prev, acc, p_prev, st
        return carry

    lax.fori_loop(0, i, body, 0)
    m, alpha_prev, acc, p_prev, st = m_sc[...], a_sc[...], acc_sc[...], p_sc[...], s_sc[...]
    for jj in range(per_q):
        j = i * per_q + jj
        st_next = scores(j + 1) if jj + 1 < per_q else None
        m, alpha, pt = softmax(st, m, jj)
        acc = weighted_values(jnp.maximum(j - 1, 0), acc, alpha_prev, p_prev)
        alpha_prev, p_prev, st = alpha, pt, st_next
    acc = weighted_values(i * per_q + per_q - 1, acc, alpha_prev, p_prev)
    _flash_finish(acc, o_ref, lam_ref, sg_ref, tq, diff)


def _flash(qt, k, vt, *, batch, heads, dk, tq, tk, diff, plain, group=1, lam=None, sg=None):
    n = k.shape[0]
    seq = n // batch
    nq = seq // tq
    mq = 2 * tq if diff else tq
    per_q = tq // tk
    kpos = jnp.arange(tq, dtype=jnp.int32).reshape(per_q, tk, 1)
    qpos = jnp.arange(mq, dtype=jnp.int32).reshape(1, 1, mq) % tq
    mask = ((kpos // CHUNK) <= (qpos // CHUNK)).astype(F32)
    in_specs = [pl.BlockSpec((dk, tq), lambda b, h, i: (h, b * nq + i)),
                pl.BlockSpec((seq, dk), lambda b, h, i: (b, h)),
                pl.BlockSpec((VT_ROWS, seq), lambda b, h, i: (h, b)),
                pl.BlockSpec((per_q, tk, mq), lambda b, h, i: (0, 0, 0), pipeline_mode=pl.Buffered(1))]
    args = [qt, k, vt, mask]
    if diff:
        in_specs += [pl.BlockSpec((1, LANE), lambda b, h, i: (0, 0))] * 2
        args += [lam, sg]
    scratch = [pltpu.VMEM((VT_ROWS, mq), F32)]
    if not plain:
        scratch = [pltpu.VMEM((1, mq), F32), pltpu.VMEM((1, mq), F32)] + scratch + [
            pltpu.VMEM((tk, mq), F32), pltpu.VMEM((tk, mq), BF16)]
    return pl.pallas_call(
        (functools.partial(_flash_plain_kernel, tq=tq, tk=tk, diff=diff, group=group) if plain
         else functools.partial(_flash_kernel, tq=tq, tk=tk, diff=diff)),
        grid=(batch, heads, nq),
        in_specs=in_specs,
        out_specs=pl.BlockSpec((tq, LANE), lambda b, h, i: (b * nq + i, h)),
        out_shape=jax.ShapeDtypeStruct((n, heads * LANE), BF16),
        scratch_shapes=scratch,
        compiler_params=_cparams(("parallel", "parallel", "arbitrary")),
        name=("flash_diff" if diff else "flash_mla") + ("_plain" if plain else ""),
    )(*args)


def _band_kernel(qt_ref, k_ref, vt_ref, bias_ref, o_ref, *, tq, nrel):
    i = pl.program_id(2)
    qt = qt_ref[...]

    def attend(rels):
        def keys(rel):
            return pl.ds(pl.multiple_of((i - rel) * tq, tq), tq)

        st = [_dot(k_ref[keys(rel), :], qt) + bias_ref[0, rel] for rel in rels]
        m = functools.reduce(jnp.maximum, [jnp.max(s, axis=0, keepdims=True) for s in st])
        acc = sum(_dot(vt_ref[:, keys(rel)], jnp.exp2(s - m).astype(BF16)) for rel, s in zip(rels, st))
        o_ref[...] = (acc[:C_DIM] / acc[C_DIM:C_DIM + 1]).T.astype(o_ref.dtype)

    for avail in range(nrel):
        cond = (i == avail) if avail < nrel - 1 else (i >= avail)
        pl.when(cond)(functools.partial(attend, tuple(range(avail + 1))))


def _band_bias_kernel(u_ref, o_ref, *, tq, nrel):
    width = u_ref.shape[2]
    rolled = pltpu.roll(jnp.broadcast_to(u_ref[0], (tq, width)), 0, 1, stride=1, stride_axis=0)
    band = rolled[:, tq:tq + nrel * tq]
    k_ch = lax.broadcasted_iota(jnp.int32, band.shape, 0) // CHUNK
    q_ch = lax.broadcasted_iota(jnp.int32, band.shape, 1) // CHUNK
    band = jnp.where((k_ch <= q_ch) & (k_ch >= q_ch - C_PAST_CHUNKS), band, NEG)
    for rel in range(nrel):
        o_ref[0, rel] = band[:, rel * tq:(rel + 1) * tq]


def _band_bias_t(tables, tq, nrel):
    groups = tables.shape[0]
    width = 1
    while width < (nrel + 1) * tq:
        width *= 2
    n_lo = tq - REL_CLIP
    assert n_lo >= 0
    u = jnp.concatenate([jnp.broadcast_to(tables[:, :1], (groups, n_lo)), tables,
                         jnp.broadcast_to(tables[:, -1:], (groups, width - n_lo - tables.shape[1]))], axis=1)
    return pl.pallas_call(
        functools.partial(_band_bias_kernel, tq=tq, nrel=nrel),
        grid=(groups,),
        in_specs=[pl.BlockSpec((1, 1, width), lambda g: (g, 0, 0))],
        out_specs=pl.BlockSpec((1, nrel, tq, tq), lambda g: (g, 0, 0, 0)),
        out_shape=jax.ShapeDtypeStruct((groups, nrel, tq, tq), F32),
        compiler_params=_cparams(("parallel",)),
        name="band_bias",
    )((u * LOG2E)[:, None, :])


def _band(qt, k, vt, bias, *, batch, tq, layer):
    n = k.shape[0]
    seq = n // batch
    nq = seq // tq
    nrel = bias.shape[1]
    return pl.pallas_call(
        functools.partial(_band_kernel, tq=tq, nrel=nrel),
        grid=(batch, C_HEADS, nq),
        in_specs=[pl.BlockSpec((LANE, tq), lambda b, h, i: (h, b * nq + i)),
                  pl.BlockSpec((seq, LANE), lambda b, h, i: (b, h)),
                  pl.BlockSpec((VT_ROWS, seq), lambda b, h, i: (h, b)),
                  pl.BlockSpec((1, nrel, tq, tq), lambda b, h, i: (layer * C_HEADS + h, 0, 0, 0))],
        out_specs=pl.BlockSpec((tq, LANE), lambda b, h, i: (b * nq + i, h)),
        out_shape=jax.ShapeDtypeStruct((n, C_W), BF16),
        compiler_params=_cparams(("parallel", "parallel", "arbitrary")),
        name="band",
    )(qt, k, vt, bias)


def _decode_kernel(*refs, heads, dk, diff, biased, interleaved):
    refs = list(refs)
    q_ref, kc_ref, vc_ref, kn_ref, vn_ref = refs[:5]
    rest = refs[5:]
    if biased:
        bc_ref, bn_ref, mc_ref, mn_ref = rest[:4]
        rest = rest[4:]
    if diff:
        lam_ref, sg_ref = rest[:2]
        rest = rest[2:]
    o_ref = rest[0]
    rows = q_ref.shape[0]
    for h in range(heads):
        qk = slice(h * dk, (h + 1) * dk)
        vv = slice(h * LANE, (h + 1) * LANE)
        q = q_ref[:, qk]
        if diff:
            q = _split_components(q)
        if interleaved:
            past = kc_ref.shape[1] // heads
            k_cache = kc_ref[0, pl.ds(h, past, stride=heads), :]
            v_cache = vc_ref[0, pl.ds(h, past, stride=heads), :]
        else:
            k_cache, v_cache = kc_ref[0, :, qk], vc_ref[0, :, vv]
        s_c = _dot_nt(q, k_cache.astype(BF16))
        s_n = _dot_nt(q, kn_ref[:, qk])
        if biased:
            s_c = jnp.where(mc_ref[...] > 0, s_c + bc_ref[h], NEG)
            s_n = jnp.where(mn_ref[...] > 0, s_n + bn_ref[h], NEG)
        m = jnp.maximum(jnp.max(s_c, axis=-1, keepdims=True), jnp.max(s_n, axis=-1, keepdims=True))
        p_c = jnp.exp(s_c - m)
        p_n = jnp.exp(s_n - m)
        l = jnp.sum(p_c, axis=-1, keepdims=True) + jnp.sum(p_n, axis=-1, keepdims=True)
        o = (_dot(p_c.astype(BF16), v_cache.astype(BF16)) + _dot(p_n.astype(BF16), vn_ref[:, vv])) / l
        if diff:
            o = _diff_out(o, rows, lam_ref, sg_ref)
        o_ref[:, vv] = o.astype(o_ref.dtype)


def _decode(q, kc, vc, kn, vn, *, heads, dk, rows, stream0=0, interleaved=False, diff=False, lam=None, sg=None,
            bias=None):
    n = q.shape[0]
    streams = n // rows
    past = kc.shape[1] // heads if interleaved else kc.shape[1]
    assert not interleaved or dk == LANE
    in_specs = [pl.BlockSpec((rows, heads * dk), lambda b: (b, 0)),
                pl.BlockSpec((1,) + kc.shape[1:], lambda b: (stream0 + b, 0, 0)),
                pl.BlockSpec((1,) + vc.shape[1:], lambda b: (stream0 + b, 0, 0)),
                pl.BlockSpec((rows, heads * dk), lambda b: (b, 0)),
                pl.BlockSpec((rows, heads * LANE), lambda b: (b, 0))]
    args = [q, kc, vc, kn, vn]
    if bias is not None:
        bias_c, bias_n, mask_c, mask_n = bias
        in_specs += [pl.BlockSpec((heads, rows, past), lambda b: (0, 0, 0)),
                     pl.BlockSpec((heads, rows, rows), lambda b: (0, 0, 0)),
                     pl.BlockSpec((rows, past), lambda b: (0, 0)),
                     pl.BlockSpec((rows, rows), lambda b: (0, 0))]
        args += [bias_c, bias_n, mask_c, mask_n]
    if diff:
        in_specs += [pl.BlockSpec((1, LANE), lambda b: (0, 0))] * 2
        args += [lam, sg]
    return pl.pallas_call(
        functools.partial(_decode_kernel, heads=heads, dk=dk, diff=diff, biased=bias is not None,
                          interleaved=interleaved),
        grid=(streams,),
        in_specs=in_specs,
        out_specs=pl.BlockSpec((rows, heads * LANE), lambda b: (b, 0)),
        out_shape=jax.ShapeDtypeStruct((n, heads * LANE), BF16),
        compiler_params=_cparams(("parallel",)),
        name="decode",
    )(*args)


def _decode_mla_kernel(q_ref, ckv_ref, kpe_ref, w_ref, gn_ref, gr_ref, kn_ref, vn_ref, o_ref):
    ckv = ckv_ref[...].astype(BF16)
    kpe2, ss_pe = _mla_rope_key(kpe_ref[...])
    nope_w = B_HEADS * B_NOPE
    for hh in range(B_HEADS):
        qk = slice(hh * B_PAD, (hh + 1) * B_PAD)
        vv = slice(hh * LANE, (hh + 1) * LANE)
        if hh % 2 == 0:
            nope2 = _dot(ckv, w_ref[:, hh * B_NOPE:(hh + 2) * B_NOPE])
            v2 = _dot(ckv, w_ref[:, nope_w + hh * B_V_DIM:nope_w + (hh + 2) * B_V_DIM]).astype(BF16)
        half = slice((hh % 2) * LANE, (hh % 2 + 1) * LANE)
        k_cache = jnp.concatenate(_mla_head_key(hh, nope2[:, half], kpe2, ss_pe, gn_ref[...], gr_ref[...]), axis=1)
        v_cache = v2[:, half]
        q = q_ref[:, qk]
        s_c = _dot_nt(q, k_cache)
        s_n = _dot_nt(q, kn_ref[:, qk])
        m = jnp.maximum(jnp.max(s_c, axis=-1, keepdims=True), jnp.max(s_n, axis=-1, keepdims=True))
        p_c = jnp.exp(s_c - m)
        p_n = jnp.exp(s_n - m)
        l = jnp.sum(p_c, axis=-1, keepdims=True) + jnp.sum(p_n, axis=-1, keepdims=True)
        o = (_dot(p_c.astype(BF16), v_cache) + _dot(p_n.astype(BF16), vn_ref[:, vv])) / l
        o_ref[:, vv] = o.astype(o_ref.dtype)


def _decode_mla(q, ckv, kpe, kn, vn, lw, *, rows, past, stream0):
    n = q.shape[0]

    def new(b):
        return (b, 0)

    def cached(b):
        return (stream0 + b, 0)

    def const(b):
        return (0, 0)

    return pl.pallas_call(
        _decode_mla_kernel,
        grid=(n // rows,),
        in_specs=[pl.BlockSpec((rows, B_HEADS * B_PAD), new),
                  pl.BlockSpec((past, B_KV_LORA), cached), pl.BlockSpec((past, B_ROPE), cached),
                  pl.BlockSpec(lw['w_kvb'].shape, const), pl.BlockSpec((1, LANE), const),
                  pl.BlockSpec((1, LANE), const),
                  pl.BlockSpec((rows, B_HEADS * B_PAD), new), pl.BlockSpec((rows, BV_W), new)],
        out_specs=pl.BlockSpec((rows, BV_W), new),
        out_shape=jax.ShapeDtypeStruct((n, BV_W), BF16),
        compiler_params=_cparams(("parallel",)),
        name="decode_mla",
    )(q, ckv, kpe, lw['w_kvb'], lw['gbkn'], lw['gbkr'], kn, vn)


def _merge_kernel(x_ref, g_ref, oa_ref, ob_ref, oc_ref, wga_ref, wgb_ref, wgc_ref, bga_ref, bgb_ref, bgc_ref,
                  wba_ref, wbb_ref, wbc_ref, wo_ref, out_ref, h_sc):
    j = pl.program_id(1)

    @pl.when(j == 0)
    def _():
        x = x_ref[...]
        h_sc[...] = (_rms(x, x.shape[1]) * g_ref[...]).astype(BF16)
        out_ref[...] = x

    h = h_sc[...]

    def branch(o_ref, wg_ref, bg_ref, wb_ref):
        gate = jax.nn.sigmoid(_dot(h, wg_ref[...]) + bg_ref[...])
        return gate * _dot(o_ref[...], wb_ref[...])

    merged = (branch(oa_ref, wga_ref, bga_ref, wba_ref) + branch(ob_ref, wgb_ref, bgb_ref, wbb_ref)
              + branch(oc_ref, wgc_ref, bgc_ref, wbc_ref))
    out_ref[...] += _dot(merged.astype(BF16), wo_ref[...])


def _merge(x, oa, ob, oc, lw, *, bm, bn):
    n, d_model = x.shape
    nbn = d_model // bn

    def row(i, j):
        return (i, 0)

    def const(i, j):
        return (0, 0)

    def col(k):
        return lambda i, j: (0, k * nbn + j)

    layer = lw['layer']

    def wcol(rows, k):
        return pl.BlockSpec((None, rows, bn), lambda i, j: (layer, 0, k * nbn + j))

    in_specs = [pl.BlockSpec((bm, d_model), row), pl.BlockSpec((1, d_model), const),
                pl.BlockSpec((bm, AV_W), row), pl.BlockSpec((bm, BV_W), row), pl.BlockSpec((bm, C_W), row),
                wcol(d_model, 0), wcol(d_model, 1), wcol(d_model, 2),
                pl.BlockSpec((1, bn), col(0)), pl.BlockSpec((1, bn), col(1)), pl.BlockSpec((1, bn), col(2)),
                wcol(AV_W, 0), wcol(BV_W, 0), wcol(C_W, 0),
                pl.BlockSpec((None, bn, d_model), lambda i, j: (layer, j, 0))]
    return pl.pallas_call(
        _merge_kernel,
        grid=(n // bm, nbn),
        in_specs=in_specs,
        out_specs=pl.BlockSpec((bm, d_model), row),
        out_shape=jax.ShapeDtypeStruct((n, d_model), F32),
        scratch_shapes=[pltpu.VMEM((bm, d_model), BF16)],
        compiler_params=_cparams(("parallel", "arbitrary")),
        name="merge",
    )(x, lw['g_attn'], oa, ob, oc, lw['w_g'], lw['w_g'], lw['w_g'], lw['b_g'], lw['b_g'], lw['b_g'],
      lw['w_br_a'], lw['w_br_b'], lw['w_br_c'], lw['w_o'])


def _mlp_kernel(x_ref, g_ref, wu_ref, wd_ref, out_ref, h_sc):
    j = pl.program_id(1)

    @pl.when(j == 0)
    def _():
        x = x_ref[...]
        h_sc[...] = (_rms(x, x.shape[1]) * g_ref[...]).astype(BF16)
        out_ref[...] = x

    u = jnp.maximum(_dot(h_sc[...], wu_ref[...]), 0.0)
    out_ref[...] += _dot((u * u).astype(BF16), wd_ref[...])


def _mlp(x, lw, *, bm, bf):
    n, d_model = x.shape
    d_ff = lw['w_up'].shape[2]
    layer = lw['layer']
    return pl.pallas_call(
        _mlp_kernel,
        grid=(n // bm, d_ff // bf),
        in_specs=[pl.BlockSpec((bm, d_model), lambda i, j: (i, 0)), pl.BlockSpec((1, d_model), lambda i, j: (0, 0)),
                  pl.BlockSpec((None, d_model, bf), lambda i, j: (layer, 0, j)),
                  pl.BlockSpec((None, bf, d_model), lambda i, j: (layer, j, 0))],
        out_specs=pl.BlockSpec((bm, d_model), lambda i, j: (i, 0)),
        out_shape=jax.ShapeDtypeStruct((n, d_model), F32),
        scratch_shapes=[pltpu.VMEM((bm, d_model), BF16)],
        compiler_params=_cparams(("parallel", "arbitrary"), VMEM_LIMIT_IN_PROJ),
        name="mlp",
    )(x, lw['g_mlp'], lw['w_up'], lw['w_down'])


def _rope_tables(pos, rot):
    half = rot // 2
    inv = jnp.power(ROPE_THETA, -jnp.arange(half, dtype=F32) / half)
    ang = pos.astype(F32)[:, None] * inv[None, :]
    cos, sin = jnp.cos(ang), jnp.sin(ang)
    npos = pos.shape[0]
    one = jnp.ones((npos, 64 - rot), F32)
    c64 = jnp.concatenate([cos, cos, one], axis=1)
    s1 = jnp.concatenate([-sin, jnp.zeros((npos, 64 - half), F32)], axis=1)
    s2 = jnp.concatenate([jnp.zeros((npos, half), F32), sin, jnp.zeros((npos, 64 - rot), F32)], axis=1)
    return [jnp.tile(t, (1, 2)) for t in (c64, s1, s2)]


def _score_bound(gq, gk, d):
    return math.sqrt(d) * LOG2E * jnp.max(jnp.abs(gq)) * jnp.max(jnp.abs(gk))


def _layer_weights(l, p, stacks):
    d_model = p['w_in'].shape[1]
    w_in = p['w_in'][l]
    o = [0]
    for width in (A_W, A_W, AV_W, B_Q_LORA, B_KV_LORA + B_ROPE, C_W, C_W, C_W):
        o.append(o[-1] + width)
    cols = lambda a, b: w_in[:, a:b]
    w_a = jnp.concatenate([
        cols(o[0], o[1]), cols(o[1], o[2]), cols(o[2], o[3]), cols(o[3], o[4]),
        cols(o[4], o[4] + B_KV_LORA), cols(o[5], o[6]), cols(o[6], o[7]), cols(o[7], o[8]),
        cols(o[4] + B_KV_LORA, o[5]), jnp.zeros((d_model, LANE - B_ROPE), F32)], axis=1).astype(BF16)
    w_qb = p['w_qb'][l].reshape(B_Q_LORA, B_HEADS, B_QK)
    w_qb = jnp.concatenate([w_qb[:, :, :B_NOPE].reshape(B_Q_LORA, -1),
                            w_qb[:, :, B_NOPE:].reshape(B_Q_LORA, -1)], axis=1).astype(BF16)
    w_kvb = p['w_kvb'][l].reshape(B_KV_LORA, B_HEADS, B_NOPE + B_V_DIM)
    w_kvb = jnp.concatenate([w_kvb[:, :, :B_NOPE].reshape(B_KV_LORA, -1),
                             w_kvb[:, :, B_NOPE:].reshape(B_KV_LORA, -1)], axis=1).astype(BF16)
    lam_init = 0.8 - 0.6 * math.exp(-0.3 * l)
    lf = p['a_lambda'][l].astype(F32)
    lam = jnp.exp(jnp.sum(lf[0] * lf[1])) - jnp.exp(jnp.sum(lf[2] * lf[3])) + lam_init

    def tiled(g, reps, scale=1.0):
        return jnp.tile(g * scale, reps)[None, :]

    bqn = p['b_q_norm'][l]
    bkn = p['b_k_norm'][l]
    return {
        'g_attn': p['attn_norm'][l][None, :],
        'g_mlp': p['mlp_norm'][l][None, :],
        'w_a': w_a,
        'b_g': p['b_gate'][l][None, :],
        'w_qb': w_qb,
        'w_kvb': w_kvb,
        'gaq': tiled(p['a_q_norm'][l], 2 * A_HEADS, A_QK_DIM ** -0.5),
        'gak': tiled(p['a_k_norm'][l], 2 * A_HEADS),
        'gbqa': p['b_qa_norm'][l][None, :],
        'gbkv': p['b_kv_norm'][l][None, :],
        'gbqn': tiled(bqn[:B_NOPE], 1, B_QK ** -0.5),
        'gbqr': tiled(bqn[B_NOPE:], 2, B_QK ** -0.5),
        'gaq_flash': tiled(p['a_q_norm'][l], 2 * A_HEADS, A_QK_DIM ** -0.5 * LOG2E),
        'gbqn_flash': tiled(bqn[:B_NOPE], 1, B_QK ** -0.5 * LOG2E),
        'gbqr_flash': tiled(bqn[B_NOPE:], 2, B_QK ** -0.5 * LOG2E),
        'gcq_flash': tiled(p['c_q_norm'][l], C_HEADS, C_DIM ** -0.5 * LOG2E),
        'gbkn': tiled(bkn[:B_NOPE], 1),
        'gbkr': tiled(bkn[B_NOPE:], 2),
        'gcq': tiled(p['c_q_norm'][l], C_HEADS, C_DIM ** -0.5),
        'gck': tiled(p['c_k_norm'][l], C_HEADS),
        'lam': jnp.broadcast_to(lam, (1, LANE)).astype(F32),
        'sg': (p['a_sub_norm'][l] * (1.0 - lam_init))[None, :],
        'bound_a': _score_bound(p['a_q_norm'][l], p['a_k_norm'][l], A_QK_DIM),
        'bound_b': _score_bound(bqn, bkn, B_QK),
        'layer': l,
        **stacks,
    }


def _stacked_weights(p):
    gate_off = p['w_in'].shape[2] - 3 * p['w_in'].shape[1]
    stacks = {name: p[name].astype(BF16) for name in ('w_br_a', 'w_br_b', 'w_br_c', 'w_o', 'w_up', 'w_down')}
    stacks['w_g'] = p['w_in'][:, :, gate_off:].astype(BF16)
    return stacks


def _pick(n, prefs):
    for b in prefs:
        if n % b == 0:
            return b
    raise ValueError(f"no block size in {prefs} divides {n}")


def kernel(x_prompt, x_sample, cache_a_k, cache_a_v, cache_b_ckv, cache_b_kpe, cache_c_k, cache_c_v,
           attn_norm, w_in, b_gate, a_q_norm, a_k_norm, a_lambda, a_sub_norm,
           b_qa_norm, b_kv_norm, w_qb, w_kvb, b_q_norm, b_k_norm,
           c_q_norm, c_k_norm, c_rel_bias, w_br_a, w_br_b, w_br_c, w_o,
           mlp_norm, w_up, w_down):
    p = dict(attn_norm=attn_norm, w_in=w_in, b_gate=b_gate, a_q_norm=a_q_norm, a_k_norm=a_k_norm,
             a_lambda=a_lambda, a_sub_norm=a_sub_norm, b_qa_norm=b_qa_norm, b_kv_norm=b_kv_norm,
             w_qb=w_qb, w_kvb=w_kvb, b_q_norm=b_q_norm, b_k_norm=b_k_norm, c_q_norm=c_q_norm,
             c_k_norm=c_k_norm, w_br_a=w_br_a, w_br_b=w_br_b, w_br_c=w_br_c, w_o=w_o, mlp_norm=mlp_norm,
             w_up=w_up, w_down=w_down)
    depth = w_in.shape[0]
    stacks = _stacked_weights(p)
    batch, seq, d_model = x_prompt.shape
    streams, dec, _ = x_sample.shape
    past = cache_a_k.shape[2]
    c_past = cache_c_k.shape[2]
    c_keep = min(C_WINDOW, seq)
    n_p, n_s = batch * seq, streams * dec
    assert seq % CHUNK == 0 and c_keep % CHUNK == 0

    pos_p = jnp.arange(seq)
    pos_s = past + jnp.arange(dec)
    tabs_p = _rope_tables(pos_p, A_ROT) + _rope_tables(pos_p, B_ROPE)
    tabs_s = [jnp.tile(t, (streams, 1)) for t in _rope_tables(pos_s, A_ROT) + _rope_tables(pos_s, B_ROPE)]

    kc_pos = jnp.concatenate([past - c_past + jnp.arange(c_past), pos_s])
    q_ch, k_ch = pos_s // CHUNK, kc_pos // CHUNK
    c_mask = ((k_ch[None, :] <= q_ch[:, None]) & (k_ch[None, :] >= q_ch[:, None] - C_PAST_CHUNKS)).astype(F32)
    rel_s = jnp.clip(pos_s[:, None] - kc_pos[None, :], -REL_CLIP, REL_CLIP) + REL_CLIP

    bm_in = _pick(seq, (512, 256, 128, 64))
    bm_tok = _pick(seq, (512, 256, 128, 64))
    t_a = _pick(seq, (512, 256, 128))
    t_b = _pick(seq, (1024, 512, 256, 128))
    tk_a, tk_b = t_a, t_b // 2
    tp_a = (t_b, t_b // 2)
    tp_b = (t_b, t_b // 2)
    t_c = _pick(seq, (512, 256, 128))
    nrel_c = C_WINDOW // t_c + 1
    band_bias = _band_bias_t(c_rel_bias.reshape(depth * C_HEADS, 2 * REL_CLIP + 1), t_c, nrel_c)
    bm_wide = _pick(seq, (1024, 512, 256, 128, 64))
    bn_merge = 512
    bf_mlp = 1024

    ca_k = cache_a_k.reshape(depth * streams, past * A_HEADS, LANE)
    ca_v = cache_a_v.reshape(depth * streams, past * A_HEADS, LANE)
    cb_ckv = cache_b_ckv.reshape(depth * streams * past, B_KV_LORA)
    cb_kpe = cache_b_kpe.reshape(depth * streams * past, B_ROPE)
    cc_k = cache_c_k.reshape(depth * streams, c_past * C_HEADS, LANE)
    cc_v = cache_c_v.reshape(depth * streams, c_past * C_HEADS, LANE)

    xp = x_prompt.reshape(n_p, d_model)
    xs = x_sample.reshape(n_s, d_model)
    outs = [[] for _ in range(12)]
    stacked = None
    for l in range(depth):
        lw = _layer_weights(l, p, stacks)

        (qa, kaf, kab, vaf, vab, qb, ckv, kpe, qc, kcb, vcb, kcf, vcf) = _in_proj(
            xp, lw, tabs_p, bm=bm_in, seg_rows=seq, keep=c_keep, transposed=True, stack=(l, depth, stacked))
        stacked = (kaf, vaf, ckv, kpe)
        kb, vb = _mla_kv(ckv, kpe, lw, bm=bm_tok, n=n_p, row0=l * n_p, transposed=True)
        flash_a = functools.partial(_flash, qa, kab, vab, batch=batch, heads=A_HEADS, dk=LANE, diff=True,
                                    lam=lw['lam'], sg=lw['sg'])
        flash_b = functools.partial(_flash, qb, kb, vb, batch=batch, heads=B_HEADS, dk=B_PAD, diff=False)
        oa = lax.cond(lw['bound_a'] <= FLASH_PLAIN_MAX_LOG2,
                      functools.partial(flash_a, tq=tp_a[0], tk=tp_a[1], plain=True, group=2),
                      functools.partial(flash_a, tq=t_a, tk=tk_a, plain=False))
        ob = lax.cond(lw['bound_b'] <= FLASH_PLAIN_MAX_LOG2,
                      functools.partial(flash_b, tq=tp_b[0], tk=tp_b[1], plain=True, group=4),
                      functools.partial(flash_b, tq=t_b, tk=tk_b, plain=False))
        oc = _band(qc, kcb, vcb, band_bias, batch=batch, tq=t_c, layer=l)
        xp = _merge(xp, oa, ob, oc, lw, bm=bm_tok, bn=bn_merge)
        xp = _mlp(xp, lw, bm=bm_wide, bf=bf_mlp)
        for dst, val in zip(outs[4:6], (kcf, vcf)):
            dst.append(val)

        (qa, kaf, kab, vaf, vab, qb, ckv, kpe, qc, kcb, vcb, kcf, vcf) = _in_proj(
            xs, lw, tabs_s, bm=n_s, seg_rows=n_s, keep=n_s, transposed=False)
        kb, vb = _mla_kv(ckv, kpe, lw, bm=n_s)
        oa = _decode(qa, ca_k, ca_v, kab, vab, heads=A_HEADS, dk=LANE, rows=dec, stream0=l * streams,
                     interleaved=True, diff=True, lam=lw['lam'], sg=lw['sg'])
        ob = _decode_mla(qb, cb_ckv, cb_kpe, kb, vb, lw, rows=dec, past=past, stream0=l * streams)
        bias_s = c_rel_bias[l][:, rel_s]
        oc = _decode(qc, cc_k, cc_v, kcb, vcb, heads=C_HEADS, dk=LANE, rows=dec, stream0=l * streams,
                     interleaved=True, bias=(bias_s[:, :, :c_past], bias_s[:, :, c_past:], c_mask[:, :c_past], c_mask[:, c_past:]))
        xs = _merge(xs, oa, ob, oc, lw, bm=n_s, bn=bn_merge)
        xs = _mlp(xs, lw, bm=n_s, bf=bf_mlp)
        for dst, val in zip(outs[6:], (kaf, vaf, ckv, kpe, kcf, vcf)):
            dst.append(val)

    st = list(stacked) + [jnp.stack(o) for o in outs[4:]]
    return (xp.reshape(batch, seq, d_model), xs.reshape(streams, dec, d_model),
            st[0].reshape(depth, batch, seq, A_HEADS, 2 * A_QK_DIM),
            st[1].reshape(depth, batch, seq, A_HEADS, A_V_DIM),
            st[2].reshape(depth, batch, seq, B_KV_LORA),
            st[3].reshape(depth, batch, seq, B_ROPE),
            st[4].reshape(depth, batch, c_keep, C_HEADS, C_DIM),
            st[5].reshape(depth, batch, c_keep, C_HEADS, C_DIM),
            st[6].reshape(depth, streams, dec, A_HEADS, 2 * A_QK_DIM),
            st[7].reshape(depth, streams, dec, A_HEADS, A_V_DIM),
            st[8].reshape(depth, streams, dec, B_KV_LORA),
            st[9].reshape(depth, streams, dec, B_ROPE),
            st[10].reshape(depth, streams, dec, C_HEADS, C_DIM),
            st[11].reshape(depth, streams, dec, C_HEADS, C_DIM))
```
